```python
import math
import jax, jax.numpy as jnp
from jax import lax
import numpy as np

D_MODEL = 2048
BATCH = 2
SEQ = 4096
DEPTH = 2
DEC_BATCH = 32
DEC_SEQ = 8
PAST_LEN = 8192
PAGE_SIZE = 128

D_MIX = D_MODEL
D_A = D_MIX // 2
H_A = 8
DK_A = D_A // H_A
DV_A = D_A // H_A
D_B = D_MIX - D_A
POOL_WINDOWS = (2, 4, 8, 16)
N_POOL_GROUPS = len(POOL_WINDOWS)
POOL_GW = D_B // N_POOL_GROUPS
POOL_MAX = max(POOL_WINDOWS)
D_C = D_MIX // 2
N_LRU_BLOCKS = 8
LRU_BS = D_C // N_LRU_BLOCKS
LRU_C = 8.0
H_D = 8
HD_D = (D_MIX - D_C) // H_D
D_D = H_D * HD_D
CONV_W = 4
CHUNK = 64
Q_BLOCK = 128
N_MEM = 256
H_X = 4
HD_X = 128
D_X = H_X * HD_X
D_FF = ((8 * D_MODEL // 3 + 127) // 128) * 128
EPS = 1e-6
N_EVEN = (DEPTH + 1) // 2
N_ODD = DEPTH // 2
D_IN_EVEN = 4 * D_A + 2 * H_A + D_B
D_IN_ODD = 2 * D_C + 3 * D_D + H_D
FOX_BIAS_INIT = 4.0
FOX_CACHE_BIAS = 7.0

kernel_name = 'hybrid_deltanet_pool_rglru_fox_step'


def rmsnorm(x, g):
    xf = x.astype(jnp.float32)
    y = xf * lax.rsqrt(jnp.mean(xf * xf, axis=-1, keepdims=True) + EPS)
    return (y * g.astype(jnp.float32)).astype(x.dtype)


def l2norm(x):
    xf = x.astype(jnp.float32)
    return xf * lax.rsqrt(jnp.sum(xf * xf, axis=-1, keepdims=True) + EPS)


def swiglu(h, wg, wu, wd):
    return (jax.nn.silu(h @ wg) * (h @ wu)) @ wd


def causal_conv(x, buf, w):
    T = x.shape[1]
    xp = jnp.concatenate([buf.astype(x.dtype), x], axis=1)
    y = xp[:, 0:T] * w[0]
    for i in range(1, CONV_W):
        y = y + xp[:, i:i + T] * w[i]
    return y, xp[:, T:]


def gated_delta_rule(q, k, v, g, beta, s0):
    B, T, H, DK = q.shape
    DV = v.shape[-1]
    C = CHUNK if T % CHUNK == 0 else T
    n = T // C

    def chunks(a):
        a = a.reshape((B, n, C) + a.shape[2:])
        return jnp.moveaxis(jnp.moveaxis(a, 1, 0), 3, 2)

    qc, kc, vc, gc, bc = chunks(q), chunks(k), chunks(v), chunks(g), chunks(beta)
    G = jnp.cumsum(gc, axis=-1)
    incl = jnp.tril(jnp.ones((C, C), dtype=bool))
    strict = jnp.tril(jnp.ones((C, C), dtype=bool), -1)
    diff = G[..., :, None] - G[..., None, :]
    dec = jnp.where(incl, jnp.exp(jnp.where(incl, diff, 0.0)), 0.0)
    kk = jnp.einsum('nbhid,nbhjd->nbhij', kc, kc)
    a_mat = jnp.eye(C, dtype=jnp.float32) + jnp.where(strict, bc[..., :, None] * kk * dec, 0.0)
    gamma = jnp.exp(G)
    rhs = jnp.concatenate([(bc * gamma)[..., None] * kc, bc[..., None] * vc], axis=-1)
    sol = lax.linalg.triangular_solve(a_mat, rhs, left_side=True, lower=True, unit_diagonal=True)
    w_c, u_c = sol[..., :DK], sol[..., DK:]
    attn = jnp.einsum('nbhid,nbhjd->nbhij', qc, kc) * dec
    q_dec = qc * gamma[..., None]
    k_dec = kc * jnp.exp(G[..., -1:] - G)[..., None]
    g_last = jnp.exp(G[..., -1])

    def step(s, inp):
        w_i, u_i, a_i, qd_i, kd_i, gl_i = inp
        u = u_i - jnp.einsum('bhcd,bhde->bhce', w_i, s)
        o = jnp.einsum('bhcd,bhde->bhce', qd_i, s) + jnp.einsum('bhij,bhje->bhie', a_i, u)
        s = gl_i[..., None, None] * s + jnp.einsum('bhcd,bhce->bhde', kd_i, u)
        return s, o

    s_final, o = lax.scan(step, s0, (w_c, u_c, attn, q_dec, k_dec, g_last))
    o = jnp.moveaxis(jnp.moveaxis(o, 2, 3), 0, 1).reshape(B, T, H, DV)
    return o, s_final


def pool_mixer(u, buf, pos0, pool_w, pool_scale):
    B, T, _ = u.shape
    up = jnp.concatenate([buf.astype(jnp.float32), u.astype(jnp.float32)], axis=1)
    csum = jnp.concatenate([jnp.zeros((B, 1, D_B), jnp.float32), jnp.cumsum(up, axis=1)], axis=1)
    pos = pos0 + jnp.arange(T)
    means = []
    for gi, wsz in enumerate(POOL_WINDOWS):
        cols = slice(gi * POOL_GW, (gi + 1) * POOL_GW)
        wsum = csum[:, POOL_MAX:, cols] - csum[:, POOL_MAX - wsz:POOL_MAX - wsz + T, cols]
        cnt = jnp.minimum(wsz, pos + 1).astype(jnp.float32)
        means.append(wsum / cnt[None, :, None])
    d = jnp.concatenate(means, axis=-1) - up[:, POOL_MAX - 1:]
    y = jnp.einsum('btgc,gcd->btgd', d.reshape(B, T, N_POOL_GROUPS, POOL_GW),
                   pool_w.astype(jnp.float32)).reshape(B, T, D_B)
    return y * pool_scale.astype(jnp.float32), up[:, T:].astype(u.dtype)


def rglru(x, h0, wa, ba, wx, bx, lam):
    B, T, _ = x.shape
    xb = x.reshape(B, T, N_LRU_BLOCKS, LRU_BS)
    r = jax.nn.sigmoid(jnp.einsum('btnc,ncd->btnd', xb, wa.astype(jnp.float32)).reshape(B, T, D_C) + ba.astype(jnp.float32))
    i = jax.nn.sigmoid(jnp.einsum('btnc,ncd->btnd', xb, wx.astype(jnp.float32)).reshape(B, T, D_C) + bx.astype(jnp.float32))
    log_a = -LRU_C * r * jax.nn.softplus(-lam.astype(jnp.float32))
    a = jnp.exp(log_a)
    b = jnp.sqrt(-jnp.expm1(2.0 * log_a)) * (i * x)

    def combine(c1, c2):
        a1, b1 = c1
        a2, b2 = c2
        return a1 * a2, a2 * b1 + b2

    a_cum, b_cum = lax.associative_scan(combine, (a, b), axis=1)
    hs = b_cum + a_cum * h0[:, None, :]
    return hs, hs[:, -1]


def fox_prompt(q, k, v, logf):
    B, T, H, hd = q.shape
    nb = T // Q_BLOCK
    F = jnp.cumsum(logf, axis=1).transpose(0, 2, 1)
    kf = k.astype(jnp.float32)
    vf = v.astype(jnp.float32)
    qb = jnp.moveaxis(q.astype(jnp.float32).reshape(B, nb, Q_BLOCK, H, hd), 1, 0)
    Fb = jnp.moveaxis(F.reshape(B, H, nb, Q_BLOCK), 2, 0)
    kpos = jnp.arange(T)

    def block(args):
        bi, qi, fi = args
        s = jnp.einsum('bqhd,bkhd->bhqk', qi, kf) * (hd ** -0.5) + fi[..., None] - F[:, :, None, :]
        qpos = bi * Q_BLOCK + jnp.arange(Q_BLOCK)
        s = jnp.where(kpos[None, :] <= qpos[:, None], s, -jnp.inf)
        p = jax.nn.softmax(s, axis=-1)
        return jnp.einsum('bhqk,bkhd->bqhd', p, vf)

    o = lax.map(block, (jnp.arange(nb), qb, Fb))
    return jnp.moveaxis(o, 0, 1).reshape(B, T, H, hd)


def fox_sample(q, k, v, logf, k_pool, v_pool, logf_pool, page_table):
    B, T, H, hd = q.shape
    P = page_table.shape[1] * PAGE_SIZE
    kp = k_pool[page_table].reshape(B, P, H, hd).astype(jnp.float32)
    vp = v_pool[page_table].reshape(B, P, H, hd).astype(jnp.float32)
    lfp = logf_pool[page_table].reshape(B, P, H).astype(jnp.float32)
    fn = jnp.cumsum(logf, axis=1).transpose(0, 2, 1)
    r = (lax.cumsum(lfp, axis=1, reverse=True) - lfp).transpose(0, 2, 1)
    qf = q.astype(jnp.float32)
    scale = hd ** -0.5
    s_past = jnp.einsum('bthd,bshd->bhts', qf, kp) * scale + fn[..., None] + r[:, :, None, :]
    s_new = jnp.einsum('bthd,bshd->bhts', qf, k.astype(jnp.float32)) * scale + fn[..., None] - fn[:, :, None, :]
    s_new = jnp.where(jnp.tril(jnp.ones((T, T), dtype=bool)), s_new, -jnp.inf)
    p = jax.nn.softmax(jnp.concatenate([s_past, s_new], axis=-1), axis=-1)
    return (jnp.einsum('bhts,bshd->bthd', p[..., :P], vp)
            + jnp.einsum('bhts,bshd->bthd', p[..., P:], v.astype(jnp.float32)))


def cross_attn(h, mk, mv, wq, wo):
    B, T, _ = h.shape
    q = (h @ wq).reshape(B, T, H_X, HD_X).astype(jnp.float32)
    s = jnp.einsum('bthd,bmhd->bhtm', q, mk.astype(jnp.float32)) * (HD_X ** -0.5)
    p = jax.nn.softmax(s, axis=-1)
    o = jnp.einsum('bhtm,bmhd->bthd', p, mv.astype(jnp.float32)).reshape(B, T, D_X)
    return o.astype(h.dtype) @ wo


def even_mixer(h, pos0, s_delta, s_conv, s_pool, w_in, w_out, conv_w, a_log, dt_bias, out_norm, pool_w, pool_scale):
    B, T, _ = h.shape
    f32 = jnp.float32
    z = h @ w_in
    qkv = z[..., :3 * D_A]
    gate = z[..., 3 * D_A:4 * D_A]
    b_raw = z[..., 4 * D_A:4 * D_A + H_A]
    a_raw = z[..., 4 * D_A + H_A:4 * D_A + 2 * H_A]
    u = z[..., 4 * D_A + 2 * H_A:]
    qkv, new_conv = causal_conv(qkv, s_conv, conv_w)
    qkv = jax.nn.silu(qkv).astype(f32)
    q = l2norm(qkv[..., :D_A].reshape(B, T, H_A, DK_A)) * (DK_A ** -0.5)
    k = l2norm(qkv[..., D_A:2 * D_A].reshape(B, T, H_A, DK_A))
    v = qkv[..., 2 * D_A:].reshape(B, T, H_A, DV_A)
    beta = jax.nn.sigmoid(b_raw.astype(f32))
    g = -jnp.exp(a_log.astype(f32)) * jax.nn.softplus(a_raw.astype(f32) + dt_bias.astype(f32))
    o, new_delta = gated_delta_rule(q, k, v, g, beta, s_delta.astype(f32))
    o = rmsnorm(o, out_norm) * jax.nn.silu(gate.astype(f32)).reshape(B, T, H_A, DV_A)
    y_b, new_pool = pool_mixer(u, s_pool, pos0, pool_w, pool_scale)
    mixed = jnp.concatenate([o.reshape(B, T, D_A).astype(h.dtype), y_b.astype(h.dtype)], axis=-1)
    return mixed @ w_out, new_delta.astype(s_delta.dtype), new_conv, new_pool


def odd_mixer(h, s_lru, s_conv, fox_past, w_in, w_out, conv_w, conv_b, wa, ba, wx, bx, lam, bf):
    B, T, _ = h.shape
    f32 = jnp.float32
    z = h @ w_in
    xc = z[..., :D_C]
    gc = z[..., D_C:2 * D_C]
    o0 = 2 * D_C
    q = z[..., o0:o0 + D_D].reshape(B, T, H_D, HD_D)
    k = z[..., o0 + D_D:o0 + 2 * D_D].reshape(B, T, H_D, HD_D)
    v = z[..., o0 + 2 * D_D:o0 + 3 * D_D].reshape(B, T, H_D, HD_D)
    f_raw = z[..., o0 + 3 * D_D:]
    xc, new_conv = causal_conv(xc, s_conv, conv_w)
    xc = (xc + conv_b).astype(f32)
    y_c, h_last = rglru(xc, s_lru.astype(f32), wa, ba, wx, bx, lam)
    y_c = y_c * jax.nn.gelu(gc.astype(f32))
    logf = jax.nn.log_sigmoid(f_raw.astype(f32) + bf.astype(f32))
    if fox_past is None:
        o = fox_prompt(q, k, v, logf)
    else:
        o = fox_sample(q, k, v, logf, fox_past[0], fox_past[1], fox_past[2], fox_past[3])
    mixed = jnp.concatenate([y_c.astype(h.dtype), o.reshape(B, T, D_D).astype(h.dtype)], axis=-1)
    return mixed @ w_out, h_last.astype(s_lru.dtype), new_conv, k, v, logf.astype(h.dtype)


def trunk(x, pos0, past, fox_pages, mem_k, mem_v, W):
    new = {name: [] for name in ('delta', 'conv_a', 'pool', 'lru', 'conv_c', 'fox_k', 'fox_v', 'fox_logf')}
    for l in range(DEPTH):
        j = l // 2
        h = rmsnorm(x, W['norm_ffn1'][l])
        x = x + 0.5 * swiglu(h, W['ffn1_wg'][l], W['ffn1_wu'][l], W['ffn1_wd'][l])
        h = rmsnorm(x, W['norm_mix'][l])
        if l % 2 == 0:
            m, s_d, s_c, s_p = even_mixer(h, pos0, past['delta'][j], past['conv_a'][j], past['pool'][j],
                                          W['even_w_in'][j], W['even_w_out'][j], W['delta_conv_w'][j],
                                          W['delta_a_log'][j], W['delta_dt_bias'][j], W['delta_out_norm'][j],
                                          W['pool_w'][j], W['pool_scale'][j])
            new['delta'].append(s_d)
            new['conv_a'].append(s_c)
            new['pool'].append(s_p)
        else:
            fp = None if fox_pages is None else (fox_pages[0][j], fox_pages[1][j], fox_pages[2][j], fox_pages[3])
            m, s_h, s_c, k_rows, v_rows, lf_rows = odd_mixer(h, past['lru'][j], past['conv_c'][j], fp,
                                                             W['odd_w_in'][j], W['odd_w_out'][j],
                                                             W['lru_conv_w'][j], W['lru_conv_b'][j],
                                                             W['lru_wa'][j], W['lru_ba'][j], W['lru_wx'][j],
                                                             W['lru_bx'][j], W['lru_lambda'][j], W['fox_bf'][j])
            new['lru'].append(s_h)
            new['conv_c'].append(s_c)
            new['fox_k'].append(k_rows)
            new['fox_v'].append(v_rows)
            new['fox_logf'].append(lf_rows)
        x = x + m
        x = x + cross_attn(rmsnorm(x, W['norm_xattn'][l]), mem_k[l], mem_v[l], W['xattn_wq'][l], W['xattn_wo'][l])
        h = rmsnorm(x, W['norm_ffn2'][l])
        x = x + 0.5 * swiglu(h, W['ffn2_wg'][l], W['ffn2_wu'][l], W['ffn2_wd'][l])
    return rmsnorm(x, W['final_norm']), {name: jnp.stack(vals) for name, vals in new.items()}


def setup_inputs(seed: int = 0) -> dict:
    key = jax.random.key(seed)
    keys = list(jax.random.split(key, 64))
    f32 = jnp.float32

    def nrm(shape, scale=1.0):
        return scale * jax.random.normal(keys.pop(), shape, f32)

    def gain(shape):
        return 1.0 + 0.05 * jax.random.normal(keys.pop(), shape, f32)

    n_pages = PAST_LEN // PAGE_SIZE
    n_used = DEC_BATCH * n_pages
    n_pool = n_used + max(1, n_used // 4)
    page_table = jax.random.permutation(keys.pop(), n_pool)[:n_used].reshape(DEC_BATCH, n_pages).astype(jnp.int32)
    a0 = jax.random.uniform(keys.pop(), (N_EVEN, H_A), f32, 1.0, 16.0)
    dt = jnp.exp(jax.random.uniform(keys.pop(), (N_EVEN, H_A), f32, math.log(1e-3), math.log(1e-1)))
    a_lru = jax.random.uniform(keys.pop(), (N_ODD, D_C), f32, 0.9, 0.999) ** (1.0 / LRU_C)
    D = D_MODEL
    return {
        'x_prompt': nrm((BATCH, SEQ, D)),
        'x_sample': nrm((DEC_BATCH, DEC_SEQ, D)),
        'state_delta': nrm((N_EVEN, DEC_BATCH, H_A, DK_A, DV_A), 0.1),
        'state_conv_a': nrm((N_EVEN, DEC_BATCH, CONV_W - 1, 3 * D_A)),
        'state_pool': nrm((N_EVEN, DEC_BATCH, POOL_MAX - 1, D_B)),
        'state_lru': nrm((N_ODD, DEC_BATCH, D_C), 0.5),
        'state_conv_c': nrm((N_ODD, DEC_BATCH, CONV_W - 1, D_C)),
        'cache_fox_k': nrm((N_ODD, n_pool, PAGE_SIZE, H_D, HD_D)),
        'cache_fox_v': nrm((N_ODD, n_pool, PAGE_SIZE, H_D, HD_D)),
        'cache_fox_logf': jax.nn.log_sigmoid(FOX_CACHE_BIAS + nrm((N_ODD, n_pool, PAGE_SIZE, H_D), 0.5)),
        'cache_mem_k': nrm((DEPTH, DEC_BATCH, N_MEM, H_X, HD_X)),
        'cache_mem_v': nrm((DEPTH, DEC_BATCH, N_MEM, H_X, HD_X)),
        'page_table': page_table,
        'mem_prompt': nrm((BATCH, N_MEM, D)),
        'ffn1_wg': nrm((DEPTH, D, D_FF), D ** -0.5),
        'ffn1_wu': nrm((DEPTH, D, D_FF), D ** -0.5),
        'ffn1_wd': nrm((DEPTH, D_FF, D), D_FF ** -0.5),
        'ffn2_wg': nrm((DEPTH, D, D_FF), D ** -0.5),
        'ffn2_wu': nrm((DEPTH, D, D_FF), D ** -0.5),
        'ffn2_wd': nrm((DEPTH, D_FF, D), D_FF ** -0.5),
        'norm_ffn1': gain((DEPTH, D)),
        'norm_mix': gain((DEPTH, D)),
        'norm_xattn': gain((DEPTH, D)),
        'norm_ffn2': gain((DEPTH, D)),
        'norm_mem': gain((DEPTH, D)),
        'final_norm': gain((D,)),
        'xattn_wq': nrm((DEPTH, D, D_X), D ** -0.5),
        'xattn_wk': nrm((DEPTH, D, D_X), D ** -0.5),
        'xattn_wv': nrm((DEPTH, D, D_X), D ** -0.5),
        'xattn_wo': nrm((DEPTH, D_X, D), D_X ** -0.5),
        'even_w_in': nrm((N_EVEN, D, D_IN_EVEN), D ** -0.5),
        'even_w_out': nrm((N_EVEN, D_A + D_B, D), (D_A + D_B) ** -0.5),
        'delta_conv_w': nrm((N_EVEN, CONV_W, 3 * D_A), CONV_W ** -0.5),
        'delta_a_log': jnp.log(a0),
        'delta_dt_bias': dt + jnp.log(-jnp.expm1(-dt)),
        'delta_out_norm': gain((N_EVEN, DV_A)),
        'pool_w': nrm((N_EVEN, N_POOL_GROUPS, POOL_GW, POOL_GW), POOL_GW ** -0.5),
        'pool_scale': gain((N_EVEN, D_B)),
        'odd_w_in': nrm((N_ODD, D, D_IN_ODD), D ** -0.5),
        'odd_w_out': nrm((N_ODD, D_C + D_D, D), (D_C + D_D) ** -0.5),
        'lru_conv_w': nrm((N_ODD, CONV_W, D_C), CONV_W ** -0.5),
        'lru_conv_b': nrm((N_ODD, D_C), 0.02),
        'lru_wa': nrm((N_ODD, N_LRU_BLOCKS, LRU_BS, LRU_BS), LRU_BS ** -0.5),
        'lru_ba': nrm((N_ODD, D_C), 0.02),
        'lru_wx': nrm((N_ODD, N_LRU_BLOCKS, LRU_BS, LRU_BS), LRU_BS ** -0.5),
        'lru_bx': nrm((N_ODD, D_C), 0.02),
        'lru_lambda': jnp.log(a_lru) - jnp.log1p(-a_lru),
        'fox_bf': FOX_BIAS_INIT + nrm((N_ODD, H_D), 0.1),
    }


def reference(x_prompt, x_sample, state_delta, state_conv_a, state_pool, state_lru, state_conv_c,
              cache_fox_k, cache_fox_v, cache_fox_logf, cache_mem_k, cache_mem_v, page_table, mem_prompt,
              ffn1_wg, ffn1_wu, ffn1_wd, ffn2_wg, ffn2_wu, ffn2_wd,
              norm_ffn1, norm_mix, norm_xattn, norm_ffn2, norm_mem, final_norm,
              xattn_wq, xattn_wk, xattn_wv, xattn_wo,
              even_w_in, even_w_out, delta_conv_w, delta_a_log, delta_dt_bias, delta_out_norm, pool_w, pool_scale,
              odd_w_in, odd_w_out, lru_conv_w, lru_conv_b, lru_wa, lru_ba, lru_wx, lru_bx, lru_lambda, fox_bf):
    W = dict(ffn1_wg=ffn1_wg, ffn1_wu=ffn1_wu, ffn1_wd=ffn1_wd, ffn2_wg=ffn2_wg, ffn2_wu=ffn2_wu, ffn2_wd=ffn2_wd,
             norm_ffn1=norm_ffn1, norm_mix=norm_mix, norm_xattn=norm_xattn, norm_ffn2=norm_ffn2,
             final_norm=final_norm, xattn_wq=xattn_wq, xattn_wo=xattn_wo,
             even_w_in=even_w_in, even_w_out=even_w_out, delta_conv_w=delta_conv_w, delta_a_log=delta_a_log,
             delta_dt_bias=delta_dt_bias, delta_out_norm=delta_out_norm, pool_w=pool_w, pool_scale=pool_scale,
             odd_w_in=odd_w_in, odd_w_out=odd_w_out, lru_conv_w=lru_conv_w, lru_conv_b=lru_conv_b,
             lru_wa=lru_wa, lru_ba=lru_ba, lru_wx=lru_wx, lru_bx=lru_bx, lru_lambda=lru_lambda, fox_bf=fox_bf)
    B = x_prompt.shape[0]
    dt = x_prompt.dtype
    mk_list, mv_list = [], []
    for l in range(DEPTH):
        m = rmsnorm(mem_prompt, norm_mem[l])
        mk_list.append((m @ xattn_wk[l]).reshape(B, N_MEM, H_X, HD_X))
        mv_list.append((m @ xattn_wv[l]).reshape(B, N_MEM, H_X, HD_X))
    mem_k_p = jnp.stack(mk_list)
    mem_v_p = jnp.stack(mv_list)
    past_p = {
        'delta': jnp.zeros((N_EVEN, B, H_A, DK_A, DV_A), dt),
        'conv_a': jnp.zeros((N_EVEN, B, CONV_W - 1, 3 * D_A), dt),
        'pool': jnp.zeros((N_EVEN, B, POOL_MAX - 1, D_B), dt),
        'lru': jnp.zeros((N_ODD, B, D_C), dt),
        'conv_c': jnp.zeros((N_ODD, B, CONV_W - 1, D_C), dt),
    }
    y_prompt, p_new = trunk(x_prompt, 0, past_p, None, mem_k_p, mem_v_p, W)
    past_s = {'delta': state_delta, 'conv_a': state_conv_a, 'pool': state_pool, 'lru': state_lru, 'conv_c': state_conv_c}
    pos0 = page_table.shape[1] * PAGE_SIZE
    y_sample, s_new = trunk(x_sample, pos0, past_s, (cache_fox_k, cache_fox_v, cache_fox_logf, page_table),
                            cache_mem_k, cache_mem_v, W)
    return (y_prompt, y_sample,
            p_new['delta'], p_new['conv_a'], p_new['pool'], p_new['lru'], p_new['conv_c'],
            p_new['fox_k'], p_new['fox_v'], p_new['fox_logf'], mem_k_p, mem_v_p,
            s_new['delta'], s_new['conv_a'], s_new['pool'], s_new['lru'], s_new['conv_c'],
            s_new['fox_k'], s_new['fox_v'], s_new['fox_logf'])
```

```python
import functools
import math

import jax
import jax.numpy as jnp
from jax import lax
from jax.experimental import pallas as pl
from jax.experimental.pallas import tpu as pltpu

F32 = jnp.float32
BF16 = jnp.bfloat16
EPS = 1e-6
LRU_C = 8.0
POOL_WINDOWS = (2, 4, 8, 16)
LANE = 128
SUBLANE = 8
DELTA_CHUNK = 128
INV_BASE = 16
NEG = -1e30


def _tile(n, target, align):
    best = None
    for d in range(align, min(n, target) + 1, align):
        if n % d == 0:
            best = d
    return n if best is None else best


def _dot(a, b, hi=False):
    if hi:
        return jnp.dot(a.astype(F32), b.astype(F32), precision=lax.Precision.HIGHEST, preferred_element_type=F32)
    return jnp.dot(a.astype(BF16), b.astype(BF16), preferred_element_type=F32)


def _dot_nt(a, b):
    return lax.dot_general(a.astype(BF16), b.astype(BF16), (((1,), (1,)), ((), ())), preferred_element_type=F32)


def _sigmoid(x):
    return 1.0 / (1.0 + jnp.exp(-x))


def _silu(x):
    return x * _sigmoid(x)


def _softplus(x):
    return jnp.maximum(x, 0.0) + jnp.log1p(jnp.exp(-jnp.abs(x)))


def _expm1(x):
    u = jnp.exp(x)
    degenerate = (u == 1.0) | (u == 0.0)
    safe = jnp.where(degenerate, 2.0, u)
    return jnp.where(u == 1.0, x, jnp.where(u == 0.0, -1.0, (u - 1.0) * x / jnp.log(safe)))


def _gelu_tanh(x):
    return 0.5 * x * (1.0 + jnp.tanh(math.sqrt(2.0 / math.pi) * (x + 0.044715 * (x * x * x))))


def _cumsum_rows(x):
    n = x.shape[0]
    row = lax.broadcasted_iota(jnp.int32, x.shape, 0)
    s = 1
    while s < n:
        x = x + jnp.where(row >= s, pltpu.roll(x, s, 0), 0.0)
        s *= 2
    return x


def _rms_rows(x, g):
    ms = jnp.mean(x * x, axis=-1, keepdims=True)
    return x * lax.rsqrt(ms + EPS) * g


def _norm_mm_kernel(x_ref, g_ref, w_ref, o_ref, hn_ref):
    @pl.when(pl.program_id(1) == 0)
    def _():
        hn_ref[...] = _rms_rows(x_ref[...], g_ref[...]).astype(BF16)

    o_ref[...] = jnp.dot(hn_ref[...], w_ref[...], preferred_element_type=F32).astype(o_ref.dtype)


def norm_mm(x, g, w, *, tm_target=704, tn=768, out_dtype=F32):
    m, d = x.shape
    n = w.shape[1]
    tm = _tile(m, tm_target, 16)
    tn = min(tn, n)
    return pl.pallas_call(
        _norm_mm_kernel,
        grid=(m // tm, pl.cdiv(n, tn)),
        in_specs=[pl.BlockSpec((tm, d), lambda i, j: (i, 0)),
                  pl.BlockSpec((1, d), lambda i, j: (0, 0)),
                  pl.BlockSpec((d, tn), lambda i, j: (0, j))],
        out_specs=pl.BlockSpec((tm, tn), lambda i, j: (i, j)),
        out_shape=jax.ShapeDtypeStruct((m, n), out_dtype),
        scratch_shapes=[pltpu.VMEM((tm, d), BF16)],
        name="norm_mm",
    )(x, g.reshape(1, d), w)


def _ffn_up_kernel(x_ref, g_ref, wg_ref, wu_ref, o_ref, hn_ref):
    @pl.when(pl.program_id(1) == 0)
    def _():
        hn_ref[...] = _rms_rows(x_ref[...], g_ref[...]).astype(BF16)

    hn = hn_ref[...]
    gate = jnp.dot(hn, wg_ref[...], preferred_element_type=F32)
    up = jnp.dot(hn, wu_ref[...], preferred_element_type=F32)
    o_ref[...] = (0.5 * _silu(gate) * up).astype(o_ref.dtype)


def ffn_up(x, g, wg, wu, *, tm_target=704, tf=512):
    m, d = x.shape
    f = wg.shape[1]
    tm = _tile(m, tm_target, 16)
    return pl.pallas_call(
        _ffn_up_kernel,
        grid=(m // tm, pl.cdiv(f, tf)),
        in_specs=[pl.BlockSpec((tm, d), lambda i, j: (i, 0)),
                  pl.BlockSpec((1, d), lambda i, j: (0, 0)),
                  pl.BlockSpec((d, tf), lambda i, j: (0, j)),
                  pl.BlockSpec((d, tf), lambda i, j: (0, j))],
        out_specs=pl.BlockSpec((tm, tf), lambda i, j: (i, j)),
        out_shape=jax.ShapeDtypeStruct((m, f), BF16),
        scratch_shapes=[pltpu.VMEM((tm, d), BF16)],
        name="ffn_up",
    )(x, g.reshape(1, d), wg, wu)


def _mm_res_kernel(*refs, n_lhs):
    a_refs = refs[:n_lhs]
    w_refs = refs[n_lhs:2 * n_lhs]
    x_ref, o_ref = refs[2 * n_lhs], refs[2 * n_lhs + 1]
    acc = x_ref[...]
    for a_ref, w_ref in zip(a_refs, w_refs):
        acc = acc + jnp.dot(a_ref[...], w_ref[...], preferred_element_type=F32)
    o_ref[...] = acc


def mm_res(lhs, w, x, *, tm_target=704, tn=512):
    m, n = x.shape
    kk = lhs[0].shape[1]
    assert all(a.shape == (m, kk) for a in lhs) and w.shape[0] == kk * len(lhs)
    tm = _tile(m, tm_target, 16)
    tn = min(tn, n)
    nl = len(lhs)
    in_specs = [pl.BlockSpec((tm, kk), lambda i, j: (i, 0)) for _ in lhs]
    in_specs += [pl.BlockSpec((kk, tn), lambda i, j, r=r: (r, j)) for r in range(nl)]
    in_specs += [pl.BlockSpec((tm, tn), lambda i, j: (i, j))]
    return pl.pallas_call(
        functools.partial(_mm_res_kernel, n_lhs=nl),
        grid=(m // tm, n // tn),
        in_specs=in_specs,
        out_specs=pl.BlockSpec((tm, tn), lambda i, j: (i, j)),
        out_shape=jax.ShapeDtypeStruct((m, n), F32),
        name="mm_res",
    )(*lhs, *([w] * nl), x)


def _rms_kernel(x_ref, g_ref, o_ref):
    o_ref[...] = _rms_rows(x_ref[...], g_ref[...])


def rms_rows(x, g, row0, rows, *, tm_target=512):
    d = x.shape[1]
    tm = _tile(rows, tm_target, 8)
    assert row0 % tm == 0
    off = row0 // tm
    return pl.pallas_call(
        _rms_kernel,
        grid=(rows // tm,),
        in_specs=[pl.BlockSpec((tm, d), lambda i: (off + i, 0)),
                  pl.BlockSpec((1, d), lambda i: (0, 0))],
        out_specs=pl.BlockSpec((tm, d), lambda i: (i, 0)),
        out_shape=jax.ShapeDtypeStruct((rows, d), F32),
        name="final_norm",
    )(x, g.reshape(1, d))


def _conv_rows(xp_ref, x, w, first, halo8, taps):
    t = x.shape[0]

    @pl.when(first)
    def _():
        xp_ref[0:SUBLANE, :] = halo8

    xp_ref[SUBLANE:SUBLANE + t, :] = x
    y = x * w[taps - 1:taps, :]
    for i in range(taps - 1):
        sh = taps - 1 - i
        y = y + xp_ref[SUBLANE - sh:SUBLANE - sh + t, :] * w[i:i + 1, :]
    xp_ref[0:SUBLANE, :] = xp_ref[t:t + SUBLANE, :]
    return y


def _unit_lower_inverse(a):
    c = a.shape[0]
    ri = lax.broadcasted_iota(jnp.int32, (c, c), 0)
    ci = lax.broadcasted_iota(jnp.int32, (c, c), 1)
    eye = (ri == ci).astype(F32)
    s = min(INV_BASE, c)
    nb = jnp.where((ri // s) == (ci // s), -a, 0.0)
    t = eye + nb
    p = nb
    k = 2
    while k < s:
        p = _dot(p, p, hi=True)
        t = t + _dot(t, p, hi=True)
        k *= 2
    while s < c:
        off = jnp.where(((ri // (2 * s)) == (ci // (2 * s))) & ((ri // s) != (ci // s)), a, 0.0)
        t = t - _dot(_dot(t, off, hi=True), t, hi=True)
        s *= 2
    return t


def _delta_kernel(q_ref, k_ref, v_ref, gt_ref, ba_ref, cq_ref, ck_ref, cv_ref, wq_ref, wk_ref, wv_ref,
                  s0_ref, alog_ref, dt_ref, on_ref, o_ref, sn_ref, xq_ref, xk_ref, xv_ref, s_ref,
                  *, tb, chunk, taps, n_heads):
    h = pl.program_id(1)
    c = pl.program_id(2)
    first = c == 0

    @pl.when(first)
    def _():
        s_ref[...] = s0_ref[0, 0]

    dk = q_ref.shape[1]
    yq = _silu(_conv_rows(xq_ref, q_ref[...], wq_ref[...], first, cq_ref[0], taps))
    yk = _silu(_conv_rows(xk_ref, k_ref[...], wk_ref[...], first, ck_ref[0], taps))
    yv = _silu(_conv_rows(xv_ref, v_ref[...], wv_ref[...], first, cv_ref[0], taps))
    q = yq * lax.rsqrt(jnp.sum(yq * yq, axis=-1, keepdims=True) + EPS) * (dk ** -0.5)
    k = yk * lax.rsqrt(jnp.sum(yk * yk, axis=-1, keepdims=True) + EPS)
    v = yv

    ba = ba_ref[...]
    lane = lax.broadcasted_iota(jnp.int32, ba.shape, 1)
    beta = jnp.sum(jnp.where(lane == h, _sigmoid(ba), 0.0), axis=1, keepdims=True)
    gfull = -jnp.exp(alog_ref[...]) * _softplus(ba + dt_ref[...])
    g = jnp.sum(jnp.where(lane == n_heads + h, gfull, 0.0), axis=1, keepdims=True)

    if tb < chunk:
        pad = chunk - tb
        zrow = jnp.zeros((pad, dk), F32)
        q = jnp.concatenate([q, zrow], axis=0)
        k = jnp.concatenate([k, zrow], axis=0)
        v = jnp.concatenate([v, zrow], axis=0)
        beta = jnp.concatenate([beta, jnp.zeros((pad, 1), F32)], axis=0)
        g = jnp.concatenate([g, jnp.zeros((pad, 1), F32)], axis=0)

    cc = chunk
    gcb = _cumsum_rows(jnp.broadcast_to(g, (cc, cc)))
    ri = lax.broadcasted_iota(jnp.int32, (cc, cc), 0)
    ci = lax.broadcasted_iota(jnp.int32, (cc, cc), 1)
    grow = jnp.sum(jnp.where(ri == ci, gcb, 0.0), axis=0, keepdims=True)
    incl = ci <= ri
    dec = jnp.where(incl, jnp.exp(jnp.where(incl, gcb - grow, 0.0)), 0.0)
    g_col = gcb[:, 0:1]
    g_last = gcb[cc - 1:cc, 0:1]
    gamma = jnp.exp(g_col)

    kk = _dot_nt(k, k)
    a_mat = jnp.where(ci < ri, beta * kk * dec, 0.0)
    t_inv = _unit_lower_inverse(a_mat)
    rhs = jnp.concatenate([(beta * gamma) * k, beta * v], axis=1)
    sol = _dot(t_inv, rhs, hi=True)
    w_c, u_c = sol[:, :dk], sol[:, dk:]
    attn = _dot_nt(q, k) * dec
    q_dec = q * gamma
    k_dec = k * jnp.exp(g_last - g_col)

    s = s_ref[...]
    u = u_c - _dot(w_c, s)
    o = _dot(q_dec, s) + _dot(attn, u)
    s_new = jnp.exp(g_last) * s + _dot(k_dec.T, u)
    s_ref[...] = s_new

    o = o[:tb]
    o = o * lax.rsqrt(jnp.mean(o * o, axis=-1, keepdims=True) + EPS) * on_ref[...]
    o_ref[...] = (o * _silu(gt_ref[...])).astype(o_ref.dtype)

    @pl.when(c == pl.num_programs(2) - 1)
    def _():
        sn_ref[0, 0] = s_new


def delta_heads(z, row0, nb, t, conv8, s0, conv_w, alog_row, dt_row, out_norm, *, n_heads, dk, cols):
    cq, ck, cv, cg, cba = cols
    taps = conv_w.shape[0]
    chunk = DELTA_CHUNK
    tb = min(t, chunk)
    assert t % tb == 0 and row0 % tb == 0 and tb % SUBLANE == 0 and dk == LANE
    nc = t // tb
    r0 = row0 // tb

    def zspec(cb):
        return pl.BlockSpec((tb, dk), lambda b, h, c: (r0 + b * nc + c, cb + h))

    def cspec(cb):
        return pl.BlockSpec((1, SUBLANE, dk), lambda b, h, c: (b, 0, cb + h))

    def wspec(cb):
        return pl.BlockSpec((taps, dk), lambda b, h, c: (0, cb + h))

    row = pl.BlockSpec((1, LANE), lambda b, h, c: (0, 0))
    kern = functools.partial(_delta_kernel, tb=tb, chunk=chunk, taps=taps, n_heads=n_heads)
    return pl.pallas_call(
        kern,
        grid=(nb, n_heads, nc),
        in_specs=[zspec(cq), zspec(ck), zspec(cv), zspec(cg),
                  pl.BlockSpec((tb, LANE), lambda b, h, c: (r0 + b * nc + c, cba)),
                  cspec(cq), cspec(ck), cspec(cv), wspec(cq), wspec(ck), wspec(cv),
                  pl.BlockSpec((1, 1, dk, dk), lambda b, h, c: (b, h, 0, 0)),
                  row, row, row],
        out_specs=[pl.BlockSpec((tb, dk), lambda b, h, c: (b * nc + c, h)),
                   pl.BlockSpec((1, 1, dk, dk), lambda b, h, c: (b, h, 0, 0))],
        out_shape=[jax.ShapeDtypeStruct((nb * t, n_heads * dk), BF16),
                   jax.ShapeDtypeStruct((nb, n_heads, dk, dk), F32)],
        scratch_shapes=[pltpu.VMEM((SUBLANE + tb, dk), F32)] * 3 + [pltpu.VMEM((dk, dk), F32)],
        name="delta_heads",
    )(z, z, z, z, z, conv8, conv8, conv8, conv_w, conv_w, conv_w, s0, alog_row, dt_row, out_norm)


def _pool_kernel(u_ref, st_ref, w_ref, sc_ref, o_ref, xp_ref, *, tt, pos0, halo, gw):
    tb = pl.program_id(1)

    @pl.when(tb == 0)
    def _():
        xp_ref[0:halo, :] = st_ref[0]

    u = u_ref[...]
    xp_ref[halo:halo + tt, :] = u
    pos = pos0 + tb * tt + lax.broadcasted_iota(jnp.int32, (tt, 1), 0)
    for gi, wsz in enumerate(POOL_WINDOWS):
        lo, hi = gi * gw, (gi + 1) * gw
        acc = u[:, lo:hi]
        for i in range(1, wsz):
            acc = acc + xp_ref[halo - i:halo - i + tt, lo:hi]
        cnt = jnp.minimum(wsz, pos + 1).astype(F32)
        d = acc * (1.0 / cnt) - u[:, lo:hi]
        y = _dot(d, w_ref[gi]) * sc_ref[:, lo:hi]
        o_ref[:, lo:hi] = y.astype(o_ref.dtype)
    xp_ref[0:halo, :] = xp_ref[tt:tt + halo, :]


def pool_groups(z, row0, nb, t, state16, pool_w, pool_scale, *, col_block, pos0, tt_target=256):
    db = pool_scale.shape[-1]
    halo = state16.shape[1]
    tt = _tile(t, tt_target, SUBLANE)
    assert row0 % tt == 0
    nt, r0 = t // tt, row0 // tt
    gw = db // len(POOL_WINDOWS)
    kern = functools.partial(_pool_kernel, tt=tt, pos0=pos0, halo=halo, gw=gw)
    return pl.pallas_call(
        kern,
        grid=(nb, nt),
        in_specs=[pl.BlockSpec((tt, db), lambda b, i: (r0 + b * nt + i, col_block)),
                  pl.BlockSpec((1, halo, db), lambda b, i: (b, 0, 0)),
                  pl.BlockSpec(pool_w.shape, lambda b, i: (0, 0, 0)),
                  pl.BlockSpec((1, db), lambda b, i: (0, 0))],
        out_specs=pl.BlockSpec((tt, db), lambda b, i: (b * nt + i, 0)),
        out_shape=jax.ShapeDtypeStruct((nb * t, db), BF16),
        scratch_shapes=[pltpu.VMEM((halo + tt, db), F32)],
        name="pool_groups",
    )(z, state16, pool_w, pool_scale.reshape(1, db))


def _rglru_kernel(xc_ref, gc_ref, c8_ref, cw_ref, cb_ref, wa_ref, wx_ref, ba_ref, bx_ref, lam_ref, h0_ref,
                  y_ref, hl_ref, xp_ref, h_ref, *, taps, n_blocks, bs):
    tb = pl.program_id(1)
    first = tb == 0

    @pl.when(first)
    def _():
        h_ref[...] = h0_ref[0]

    x = _conv_rows(xp_ref, xc_ref[...], cw_ref[...], first, c8_ref[0], taps) + cb_ref[...]
    tt = x.shape[0]
    rs, is_ = [], []
    for n in range(n_blocks):
        xb = x[:, n * bs:(n + 1) * bs]
        rs.append(_dot(xb, wa_ref[n]))
        is_.append(_dot(xb, wx_ref[n]))
    r = _sigmoid(jnp.concatenate(rs, axis=1) + ba_ref[...])
    i = _sigmoid(jnp.concatenate(is_, axis=1) + bx_ref[...])
    log_a = -LRU_C * r * _softplus(-lam_ref[...])
    a = jnp.exp(log_a)
    b = jnp.sqrt(-_expm1(2.0 * log_a)) * (i * x)
    row = lax.broadcasted_iota(jnp.int32, a.shape, 0)
    s = 1
    while s < tt:
        keep = row >= s
        b = jnp.where(keep, a * pltpu.roll(b, s, 0) + b, b)
        a = jnp.where(keep, a * pltpu.roll(a, s, 0), a)
        s *= 2
    hs = b + a * h_ref[...]
    h_last = hs[tt - 1:tt, :]
    h_ref[...] = h_last
    y_ref[...] = (hs * _gelu_tanh(gc_ref[...])).astype(y_ref.dtype)

    @pl.when(tb == pl.num_programs(1) - 1)
    def _():
        hl_ref[0] = h_last


def rglru_blocks(z, row0, nb, t, conv8, h0, conv_w, conv_b, wa, ba, wx, bx, lam, *, tt_target=256):
    dc = lam.shape[-1]
    taps = conv_w.shape[0]
    n_blocks, bs = wa.shape[0], wa.shape[1]
    tt = _tile(t, tt_target, SUBLANE)
    assert row0 % tt == 0
    nt, r0 = t // tt, row0 // tt
    vec = pl.BlockSpec((1, dc), lambda b, i: (0, 0))
    kern = functools.partial(_rglru_kernel, taps=taps, n_blocks=n_blocks, bs=bs)
    return pl.pallas_call(
        kern,
        grid=(nb, nt),
        in_specs=[pl.BlockSpec((tt, dc), lambda b, i: (r0 + b * nt + i, 0)),
                  pl.BlockSpec((tt, dc), lambda b, i: (r0 + b * nt + i, 1)),
                  pl.BlockSpec((1, SUBLANE, dc), lambda b, i: (b, 0, 0)),
                  pl.BlockSpec((taps, dc), lambda b, i: (0, 0)),
                  vec,
                  pl.BlockSpec(wa.shape, lambda b, i: (0, 0, 0)),
                  pl.BlockSpec(wx.shape, lambda b, i: (0, 0, 0)),
                  vec, vec, vec,
                  pl.BlockSpec((1, 1, dc), lambda b, i: (b, 0, 0))],
        out_specs=[pl.BlockSpec((tt, dc), lambda b, i: (b * nt + i, 0)),
                   pl.BlockSpec((1, 1, dc), lambda b, i: (b, 0, 0))],
        out_shape=[jax.ShapeDtypeStruct((nb * t, dc), BF16),
                   jax.ShapeDtypeStruct((nb, 1, dc), F32)],
        scratch_shapes=[pltpu.VMEM((SUBLANE + tt, dc), F32), pltpu.VMEM((1, dc), F32)],
        name="rglru_blocks",
    )(z, z, conv8, conv_w, conv_b.reshape(1, dc), wa, wx, ba.reshape(1, dc), bx.reshape(1, dc),
      lam.reshape(1, dc), h0.reshape(nb, 1, dc))


def _fox_prep_kernel(f_ref, bf_ref, lf_ref, cum_ref, cumt_ref, *, n_heads):
    lf = -_softplus(-(f_ref[...] + bf_ref[...]))
    lf_ref[...] = lf
    cum = _cumsum_rows(lf)
    cum_ref[...] = cum
    cumt_ref[0] = cum.T[0:n_heads, :]


def fox_prep(z, nb, t, bf_row, *, col_block, n_heads):
    return pl.pallas_call(
        functools.partial(_fox_prep_kernel, n_heads=n_heads),
        grid=(nb,),
        in_specs=[pl.BlockSpec((t, LANE), lambda b: (b, col_block)),
                  pl.BlockSpec((1, LANE), lambda b: (0, 0))],
        out_specs=[pl.BlockSpec((t, LANE), lambda b: (b, 0)),
                   pl.BlockSpec((t, LANE), lambda b: (b, 0)),
                   pl.BlockSpec((1, n_heads, t), lambda b: (b, 0, 0))],
        out_shape=[jax.ShapeDtypeStruct((nb * t, LANE), F32),
                   jax.ShapeDtypeStruct((nb * t, LANE), F32),
                   jax.ShapeDtypeStruct((nb, n_heads, t), F32)],
        name="fox_prep",
    )(z, bf_row)


def _fox_flash_kernel(q_ref, k_ref, v_ref, fq_ref, fk_ref, o_ref, m_ref, l_ref, acc_ref, fqc_ref, *, scale):
    h = pl.program_id(1)
    qi = pl.program_id(2)
    ki = pl.program_id(3)
    tq, tk = q_ref.shape[0], k_ref.shape[0]

    @pl.when(ki == 0)
    def _():
        m_ref[...] = jnp.full(m_ref.shape, NEG, F32)
        l_ref[...] = jnp.zeros(l_ref.shape, F32)
        acc_ref[...] = jnp.zeros(acc_ref.shape, F32)
        fq = fq_ref[...]
        lane = lax.broadcasted_iota(jnp.int32, fq.shape, 1)
        fqc_ref[...] = jnp.sum(jnp.where(lane == h, fq, 0.0), axis=1, keepdims=True)

    @pl.when(ki <= qi)
    def _():
        fk = fk_ref[0, pl.ds(h, 1), :]
        s = _dot_nt(q_ref[...], k_ref[...]) * scale + fqc_ref[...] - fk
        ri = lax.broadcasted_iota(jnp.int32, (tq, tk), 0) + qi * tq
        ci = lax.broadcasted_iota(jnp.int32, (tq, tk), 1) + ki * tk
        s = jnp.where(ci <= ri, s, NEG)
        m_old = m_ref[...]
        m_new = jnp.maximum(m_old, jnp.max(s, axis=1, keepdims=True))
        alpha = jnp.exp(m_old - m_new)
        p = jnp.exp(s - m_new)
        l_ref[...] = alpha * l_ref[...] + jnp.sum(p, axis=1, keepdims=True)
        acc_ref[...] = alpha * acc_ref[...] + _dot(p, v_ref[...])
        m_ref[...] = m_new

    @pl.when(ki == qi)
    def _():
        o_ref[...] = (acc_ref[...] / l_ref[...]).astype(o_ref.dtype)


def fox_prompt(z, cum, cumt, nb, t, *, n_heads, hd, cols, tq_target=512):
    cq, ck, cv = cols
    tq = _tile(t, tq_target, LANE)
    nq = t // tq
    kern = functools.partial(_fox_flash_kernel, scale=hd ** -0.5)

    def kv(cb):
        return pl.BlockSpec((tq, hd), lambda b, h, i, j: (b * nq + jnp.minimum(i, j), cb + h))

    return pl.pallas_call(
        kern,
        grid=(nb, n_heads, nq, nq),
        in_specs=[pl.BlockSpec((tq, hd), lambda b, h, i, j: (b * nq + i, cq + h)),
                  kv(ck), kv(cv),
                  pl.BlockSpec((tq, LANE), lambda b, h, i, j: (b * nq + i, 0)),
                  pl.BlockSpec((1, n_heads, tq), lambda b, h, i, j: (b, 0, jnp.minimum(i, j)))],
        out_specs=pl.BlockSpec((tq, hd), lambda b, h, i, j: (b * nq + i, h)),
        out_shape=jax.ShapeDtypeStruct((nb * t, n_heads * hd), BF16),
        scratch_shapes=[pltpu.VMEM((tq, 1), F32), pltpu.VMEM((tq, 1), F32), pltpu.VMEM((tq, hd), F32),
                        pltpu.VMEM((tq, 1), F32)],
        name="fox_prompt",
    )(z, z, z, cum, cumt)


def _fox_sample_kernel(pt_ref, q_ref, kn_ref, vn_ref, f_ref, bf_ref, *rest, n_heads, hd, pages_per_step, scale):
    g_pages = pages_per_step
    k_refs = rest[0:g_pages]
    v_refs = rest[g_pages:2 * g_pages]
    lf_refs = rest[2 * g_pages:3 * g_pages]
    o_ref, lfo_ref, m_ref, l_ref, acc_ref, carry_ref, fn_ref = rest[3 * g_pages:]
    step = pl.program_id(1)
    t = q_ref.shape[0]
    page = k_refs[0].shape[2]
    q = q_ref[...].astype(BF16)

    def update(h, s, vh):
        m_old = m_ref[h]
        m_new = jnp.maximum(m_old, jnp.max(s, axis=1, keepdims=True))
        alpha = jnp.exp(m_old - m_new)
        p = jnp.exp(s - m_new)
        l_ref[h] = alpha * l_ref[h] + jnp.sum(p, axis=1, keepdims=True)
        acc_ref[:, h * hd:(h + 1) * hd] = alpha * acc_ref[:, h * hd:(h + 1) * hd] + _dot(p, vh)
        m_ref[h] = m_new

    @pl.when(step == 0)
    def _():
        lf = -_softplus(-(f_ref[...] + bf_ref[...]))
        lfo_ref[...] = lf
        fn = _cumsum_rows(lf)
        fn_ref[...] = fn
        carry_ref[...] = jnp.zeros(carry_ref.shape, F32)
        m_ref[...] = jnp.full(m_ref.shape, NEG, F32)
        l_ref[...] = jnp.zeros(l_ref.shape, F32)
        acc_ref[...] = jnp.zeros(acc_ref.shape, F32)
        kn = jnp.concatenate([kn_ref[...], jnp.zeros((page - t, n_heads * hd), F32)], axis=0)
        vn = jnp.concatenate([vn_ref[...], jnp.zeros((page - t, n_heads * hd), F32)], axis=0)
        ri = lax.broadcasted_iota(jnp.int32, (t, page), 0)
        ci = lax.broadcasted_iota(jnp.int32, (t, page), 1)
        for h in range(n_heads):
            col = lf[:, h:h + 1]
            fnrow = jnp.sum(jnp.where(ri <= ci, col, 0.0), axis=0, keepdims=True)
            sl = slice(h * hd, (h + 1) * hd)
            s = _dot_nt(q[:, sl], kn[:, sl]) * scale + fn[:, h:h + 1] - fnrow
            s = jnp.where(ci <= ri, s, NEG)
            update(h, s, vn[:, sl])

    fn = fn_ref[...]
    ri = lax.broadcasted_iota(jnp.int32, (page, page), 0)
    ci = lax.broadcasted_iota(jnp.int32, (page, page), 1)
    later = ri > ci
    for g in reversed(range(g_pages)):
        kp = k_refs[g][0, 0]
        vp = v_refs[g][0, 0]
        lfp = lf_refs[g][0, 0]
        for h in range(n_heads):
            col = lfp[:, h:h + 1]
            rrow = jnp.sum(jnp.where(later, col, 0.0), axis=0, keepdims=True)
            tot = jnp.sum(col, axis=0, keepdims=True)
            carry = carry_ref[h:h + 1, 0:1]
            sl = slice(h * hd, (h + 1) * hd)
            s = _dot_nt(q[:, sl], kp[:, sl]) * scale + fn[:, h:h + 1] + rrow + carry
            update(h, s, vp[:, sl])
            carry_ref[h:h + 1, :] = jnp.broadcast_to(carry + tot, (1, carry_ref.shape[1]))

    @pl.when(step == pl.num_programs(1) - 1)
    def _():
        for h in range(n_heads):
            sl = slice(h * hd, (h + 1) * hd)
            o_ref[:, sl] = (acc_ref[:, sl] / l_ref[h]).astype(o_ref.dtype)


def fox_sample(z, row0, nb, t, page_table, kc, vc, lfc, layer, bf_row, *, n_heads, hd, cols, pages_per_step=8):
    cq, ck, cv, cf = cols
    n_pages = page_table.shape[1]
    page = kc.shape[2]
    g_pages = pages_per_step
    while n_pages % g_pages:
        g_pages //= 2
    ns = n_pages // g_pages
    dd = n_heads * hd
    assert row0 % t == 0 and t % SUBLANE == 0
    r0 = row0 // t

    def zspec(cb, w):
        return pl.BlockSpec((t, w), lambda b, s, pt: (r0 + b, cb))

    def pspec(g, w):
        return pl.BlockSpec((1, 1, page, w), lambda b, s, pt, g=g: (layer, pt[b, (ns - 1 - s) * g_pages + g], 0, 0))

    kern = functools.partial(_fox_sample_kernel, n_heads=n_heads, hd=hd, pages_per_step=g_pages, scale=hd ** -0.5)
    grid_spec = pltpu.PrefetchScalarGridSpec(
        num_scalar_prefetch=1,
        grid=(nb, ns),
        in_specs=[zspec(cq, dd), zspec(ck, dd), zspec(cv, dd), zspec(cf, LANE),
                  pl.BlockSpec((1, LANE), lambda b, s, pt: (0, 0))]
        + [pspec(g, dd) for g in range(g_pages)] + [pspec(g, dd) for g in range(g_pages)]
        + [pspec(g, n_heads) for g in range(g_pages)],
        out_specs=[pl.BlockSpec((t, dd), lambda b, s, pt: (b, 0)),
                   pl.BlockSpec((t, LANE), lambda b, s, pt: (b, 0))],
        scratch_shapes=[pltpu.VMEM((n_heads, t, 1), F32), pltpu.VMEM((n_heads, t, 1), F32),
                        pltpu.VMEM((t, dd), F32), pltpu.VMEM((n_heads, LANE), F32), pltpu.VMEM((t, LANE), F32)],
    )
    return pl.pallas_call(
        kern,
        grid_spec=grid_spec,
        out_shape=[jax.ShapeDtypeStruct((nb * t, dd), BF16), jax.ShapeDtypeStruct((nb * t, LANE), F32)],
        name="fox_sample",
    )(page_table, z, z, z, z, bf_row, *([kc] * g_pages), *([vc] * g_pages), *([lfc] * g_pages))


def _xattn_kernel(q_ref, k_ref, v_ref, o_ref, *, n_heads, hd, lead, scale):
    idx = (0,) * lead
    kk = k_ref[idx] if lead else k_ref[...]
    vv = v_ref[idx] if lead else v_ref[...]
    q = q_ref[...]
    for h in range(n_heads):
        sl = slice(h * hd, (h + 1) * hd)
        s = _dot_nt(q[:, sl], kk[:, sl]) * scale
        p = jnp.exp(s - jnp.max(s, axis=1, keepdims=True))
        o = _dot(p, vv[:, sl]) / jnp.sum(p, axis=1, keepdims=True)
        o_ref[:, sl] = o.astype(o_ref.dtype)


def xattn(qx, row0, nb, t, mk, mv, kspec, vspec, *, n_heads, hd, lead, tt_target=512):
    dx = n_heads * hd
    tt = _tile(t, tt_target, SUBLANE)
    assert row0 % tt == 0
    nt, r0 = t // tt, row0 // tt
    kern = functools.partial(_xattn_kernel, n_heads=n_heads, hd=hd, lead=lead, scale=hd ** -0.5)
    return pl.pallas_call(
        kern,
        grid=(nb, nt),
        in_specs=[pl.BlockSpec((tt, dx), lambda b, i: (r0 + b * nt + i, 0)), kspec, vspec],
        out_specs=pl.BlockSpec((tt, dx), lambda b, i: (b * nt + i, 0)),
        out_shape=jax.ShapeDtypeStruct((nb * t, dx), BF16),
        name="xattn",
    )(qx, mk, mv)


def _pad_cols(w, n):
    return jnp.pad(w, ((0, 0), (0, n - w.shape[1])))


def _halo_rows(buf, rows):
    return jnp.pad(buf, ((0, 0), (rows - buf.shape[1], 0), (0, 0)))


def _tail_state(buf, new, t):
    r = buf.shape[1]
    if t >= r:
        return new[:, t - r:]
    return jnp.concatenate([buf[:, t:], new], axis=1)


def kernel(x_prompt, x_sample, state_delta, state_conv_a, state_pool, state_lru, state_conv_c, cache_fox_k, cache_fox_v, cache_fox_logf, cache_mem_k, cache_mem_v, page_table, mem_prompt, ffn1_wg, ffn1_wu, ffn1_wd, ffn2_wg, ffn2_wu, ffn2_wd, norm_ffn1, norm_mix, norm_xattn, norm_ffn2, norm_mem, final_norm, xattn_wq, xattn_wk, xattn_wv, xattn_wo, even_w_in, even_w_out, delta_conv_w, delta_a_log, delta_dt_bias, delta_out_norm, pool_w, pool_scale, odd_w_in, odd_w_out, lru_conv_w, lru_conv_b, lru_wa, lru_ba, lru_wx, lru_bx, lru_lambda, fox_bf):
    bp, tp, d = x_prompt.shape
    bs, ts, _ = x_sample.shape
    mp, ms = bp * tp, bs * ts
    depth = ffn1_wg.shape[0]
    h_a, dk_a = state_delta.shape[2], state_delta.shape[3]
    d_a = h_a * dk_a
    d_b = state_pool.shape[-1]
    d_c = state_lru.shape[-1]
    h_d, hd_d = cache_fox_k.shape[3], cache_fox_k.shape[4]
    d_d = h_d * hd_d
    n_mem, h_x, hd_x = cache_mem_k.shape[2], cache_mem_k.shape[3], cache_mem_k.shape[4]
    d_x = h_x * hd_x
    page = cache_fox_k.shape[2]
    pos0_s = page_table.shape[1] * page
    assert d_a == d_b and d_c == d_d and 2 * h_a <= LANE and h_d <= LANE
    tn_z = 768
    nz = -(-(max(4 * d_a + d_b, 2 * d_c + 3 * d_d) + LANE) // tn_z) * tn_z
    bc =(4 * d_a + d_b) // LANE
    assert bc == (2 * d_c + 3 * d_d) // LANE

    x = jnp.concatenate([x_prompt.reshape(mp, d), x_sample.reshape(ms, d)], axis=0)

    mem2 = mem_prompt.reshape(bp * n_mem, d)
    mem_kv = []
    for l in range(depth):
        wkv = jnp.concatenate([xattn_wk[l], xattn_wv[l]], axis=1).astype(BF16)
        mem_kv.append(norm_mm(mem2, norm_mem[l], wkv, tn=d_x))
    kc4 = cache_fox_k.reshape(cache_fox_k.shape[0], cache_fox_k.shape[1], page, d_d)
    vc4 = cache_fox_v.reshape(kc4.shape)
    mk4 = cache_mem_k.reshape(depth, bs, n_mem, d_x)
    mv4 = cache_mem_v.reshape(depth, bs, n_mem, d_x)

    new_p = {n: [] for n in ('delta', 'conv_a', 'pool', 'lru', 'conv_c', 'fox_k', 'fox_v', 'fox_logf')}
    new_s = {n: [] for n in new_p}

    def seq_rows(zz, grp):
        return zz[:mp].reshape(bp, tp, -1) if grp == 0 else zz[mp:].reshape(bs, ts, -1)

    for l in range(depth):
        j = l // 2
        a = ffn_up(x, norm_ffn1[l], ffn1_wg[l].astype(BF16), ffn1_wu[l].astype(BF16))
        x = mm_res([a], ffn1_wd[l].astype(BF16), x)

        if l % 2 == 0:
            w = even_w_in[j]
            w_in = _pad_cols(jnp.concatenate([w[:, :4 * d_a], w[:, 4 * d_a + 2 * h_a:], w[:, 4 * d_a:4 * d_a + 2 * h_a]],
                                             axis=1), nz).astype(BF16)
            z = norm_mm(x, norm_mix[l], w_in)
            hb = dk_a // LANE
            alog_row = jnp.zeros((1, LANE), F32).at[0, h_a:2 * h_a].set(delta_a_log[j])
            dt_row = jnp.zeros((1, LANE), F32).at[0, h_a:2 * h_a].set(delta_dt_bias[j])
            groups = ((0, bp, tp, jnp.zeros((bp,) + state_conv_a.shape[2:], F32),
                       jnp.zeros((bp,) + state_delta.shape[2:], F32), jnp.zeros((bp,) + state_pool.shape[2:], F32), 0, new_p),
                      (mp, bs, ts, state_conv_a[j], state_delta[j], state_pool[j], pos0_s, new_s))
            mix_a, mix_b = [], []
            for gi, (row0, nb, t, cbuf, s0, pbuf, pos0, new) in enumerate(groups):
                o, s_new = delta_heads(z, row0, nb, t, _halo_rows(cbuf, SUBLANE), s0, delta_conv_w[j], alog_row, dt_row,
                                       delta_out_norm[j].reshape(1, dk_a), n_heads=h_a, dk=dk_a,
                                       cols=(0, h_a * hb, 2 * h_a * hb, 3 * h_a * hb, bc))
                yb = pool_groups(z, row0, nb, t, _halo_rows(pbuf, 2 * SUBLANE), pool_w[j].astype(BF16), pool_scale[j],
                                 col_block=4 * d_a // d_b, pos0=pos0)
                mix_a.append(o)
                mix_b.append(yb)
                zs = seq_rows(z, gi)
                new['delta'].append(s_new)
                new['conv_a'].append(_tail_state(cbuf, zs[:, :, :3 * d_a], t))
                new['pool'].append(_tail_state(pbuf, zs[:, :, 4 * d_a:4 * d_a + d_b], t))
            w_out = even_w_out[j].astype(BF16)
        else:
            w_in = _pad_cols(odd_w_in[j], nz).astype(BF16)
            z = norm_mm(x, norm_mix[l], w_in)
            bf_row = jnp.zeros((1, LANE), F32).at[0, :h_d].set(fox_bf[j])
            qb, kb, vb = 2 * d_c // hd_d, (2 * d_c + d_d) // hd_d, (2 * d_c + 2 * d_d) // hd_d
            groups = ((0, bp, tp, jnp.zeros((bp,) + state_conv_c.shape[2:], F32), jnp.zeros((bp, d_c), F32), new_p),
                      (mp, bs, ts, state_conv_c[j], state_lru[j], new_s))
            mix_a, mix_b = [], []
            for gi, (row0, nb, t, cbuf, h0, new) in enumerate(groups):
                yc, h_last = rglru_blocks(z, row0, nb, t, _halo_rows(cbuf, SUBLANE), h0, lru_conv_w[j], lru_conv_b[j],
                                          lru_wa[j].astype(BF16), lru_ba[j], lru_wx[j].astype(BF16), lru_bx[j],
                                          lru_lambda[j])
                if gi == 0:
                    lf, cum, cumt = fox_prep(z, nb, t, bf_row, col_block=bc, n_heads=h_d)
                    o = fox_prompt(z, cum, cumt, nb, t, n_heads=h_d, hd=hd_d, cols=(qb, kb, vb))
                else:
                    o, lf = fox_sample(z, row0, nb, t, page_table, kc4, vc4, cache_fox_logf, j, bf_row,
                                       n_heads=h_d, hd=hd_d,
                                       cols=(2 * d_c // d_d, (2 * d_c + d_d) // d_d, (2 * d_c + 2 * d_d) // d_d, bc))
                mix_a.append(yc)
                mix_b.append(o)
                zs = seq_rows(z, gi)
                new['lru'].append(h_last.reshape(nb, d_c))
                new['conv_c'].append(_tail_state(cbuf, zs[:, :, :d_c], t))
                new['fox_k'].append(zs[:, :, 2 * d_c + d_d:2 * d_c + 2 * d_d].reshape(nb, t, h_d, hd_d))
                new['fox_v'].append(zs[:, :, 2 * d_c + 2 * d_d:2 * d_c + 3 * d_d].reshape(nb, t, h_d, hd_d))
                new['fox_logf'].append(lf[:, :h_d].reshape(nb, t, h_d))
            w_out = odd_w_out[j].astype(BF16)

        x = mm_res([jnp.concatenate(mix_a, axis=0), jnp.concatenate(mix_b, axis=0)], w_out, x)

        qx = norm_mm(x, norm_xattn[l], xattn_wq[l].astype(BF16), tn=d_x)
        o_p = xattn(qx, 0, bp, tp, mem_kv[l], mem_kv[l],
                    pl.BlockSpec((n_mem, d_x), lambda b, i: (b, 0)), pl.BlockSpec((n_mem, d_x), lambda b, i: (b, 1)),
                    n_heads=h_x, hd=hd_x, lead=0)
        o_s = xattn(qx, mp, bs, ts, mk4, mv4,
                    pl.BlockSpec((1, 1, n_mem, d_x), lambda b, i, l=l: (l, b, 0, 0)),
                    pl.BlockSpec((1, 1, n_mem, d_x), lambda b, i, l=l: (l, b, 0, 0)),
                    n_heads=h_x, hd=hd_x, lead=2)
        x = mm_res([jnp.concatenate([o_p, o_s], axis=0)], xattn_wo[l].astype(BF16), x)

        a = ffn_up(x, norm_ffn2[l], ffn2_wg[l].astype(BF16), ffn2_wu[l].astype(BF16))
        x = mm_res([a], ffn2_wd[l].astype(BF16), x)

    y_p = rms_rows(x, final_norm, 0, mp).reshape(bp, tp, d)
    y_s = rms_rows(x, final_norm, mp, ms).reshape(bs, ts, d)
    mem_k_p = jnp.stack([m[:, :d_x].reshape(bp, n_mem, h_x, hd_x) for m in mem_kv])
    mem_v_p = jnp.stack([m[:, d_x:].reshape(bp, n_mem, h_x, hd_x) for m in mem_kv])
    names = ('delta', 'conv_a', 'pool', 'lru', 'conv_c', 'fox_k', 'fox_v', 'fox_logf')
    return ((y_p, y_s) + tuple(jnp.stack(new_p[n]) for n in names) + (mem_k_p, mem_v_p)
            + tuple(jnp.stack(new_s[n]) for n in names))
```

```python
import functools
import math

import jax
import jax.numpy as jnp
from jax import lax
from jax.experimental import pallas as pl
from jax.experimental.pallas import tpu as pltpu

F32 = jnp.float32
BF16 = jnp.bfloat16
EPS = 1e-6
LRU_C = 8.0
POOL_WINDOWS = (2, 4, 8, 16)
LANE = 128
SUBLANE = 8
DELTA_CHUNK = 128
DELTA_HEADS_PER_STEP = 4
INV_BASE = 16
FLASH_ROW_BAND = 128
FLASH_CHAINS = 2
NEG = -1e30


def _tile(n, target, align):
    best = None
    for d in range(align, min(n, target) + 1, align):
        if n % d == 0:
            best = d
    return n if best is None else best


def _dot(a, b, hi=False):
    if hi:
        return jnp.dot(a.astype(F32), b.astype(F32), precision=lax.Precision.HIGHEST, preferred_element_type=F32)
    return jnp.dot(a.astype(BF16), b.astype(BF16), preferred_element_type=F32)


def _dot_nt(a, b):
    return lax.dot_general(a.astype(BF16), b.astype(BF16), (((1,), (1,)), ((), ())), preferred_element_type=F32)


def _sigmoid(x):
    return 1.0 / (1.0 + jnp.exp(-x))


def _silu(x):
    return x * _sigmoid(x)


def _softplus(x):
    return jnp.maximum(x, 0.0) + jnp.log1p(jnp.exp(-jnp.abs(x)))


def _expm1(x):
    u = jnp.exp(x)
    degenerate = (u == 1.0) | (u == 0.0)
    safe = jnp.where(degenerate, 2.0, u)
    return jnp.where(u == 1.0, x, jnp.where(u == 0.0, -1.0, (u - 1.0) * x / jnp.log(safe)))


def _gelu_tanh(x):
    return 0.5 * x * (1.0 + jnp.tanh(math.sqrt(2.0 / math.pi) * (x + 0.044715 * (x * x * x))))


def _cumsum_rows(x):
    n = x.shape[0]
    row = lax.broadcasted_iota(jnp.int32, x.shape, 0)
    s = 1
    while s < n:
        x = x + jnp.where(row >= s, pltpu.roll(x, s, 0), 0.0)
        s *= 2
    return x


def _rms_rows(x, g):
    ms = jnp.mean(x * x, axis=-1, keepdims=True)
    return x * lax.rsqrt(ms + EPS) * g


def _norm_mm_kernel(x_ref, g_ref, w_ref, o_ref, hn_ref):
    @pl.when(pl.program_id(1) == 0)
    def _():
        hn_ref[...] = _rms_rows(x_ref[...], g_ref[...]).astype(BF16)

    o_ref[...] = jnp.dot(hn_ref[...], w_ref[...], preferred_element_type=F32).astype(o_ref.dtype)


def norm_mm(x, g, w, *, tm_target=704, tn=768, out_dtype=F32):
    m, d = x.shape
    n = w.shape[1]
    tm = _tile(m, tm_target, 16)
    tn = min(tn, n)
    return pl.pallas_call(
        _norm_mm_kernel,
        grid=(m // tm, pl.cdiv(n, tn)),
        in_specs=[pl.BlockSpec((tm, d), lambda i, j: (i, 0)),
                  pl.BlockSpec((1, d), lambda i, j: (0, 0)),
                  pl.BlockSpec((d, tn), lambda i, j: (0, j))],
        out_specs=pl.BlockSpec((tm, tn), lambda i, j: (i, j)),
        out_shape=jax.ShapeDtypeStruct((m, n), out_dtype),
        scratch_shapes=[pltpu.VMEM((tm, d), BF16)],
        name="norm_mm",
    )(x, g.reshape(1, d), w)


def _ffn_up_kernel(x_ref, g_ref, wg_ref, wu_ref, o_ref, hn_ref):
    @pl.when(pl.program_id(1) == 0)
    def _():
        hn_ref[...] = _rms_rows(x_ref[...], g_ref[...]).astype(BF16)

    hn = hn_ref[...]
    gate = jnp.dot(hn, wg_ref[...], preferred_element_type=F32)
    up = jnp.dot(hn, wu_ref[...], preferred_element_type=F32)
    o_ref[...] = (0.5 * _silu(gate) * up).astype(o_ref.dtype)


def ffn_up(x, g, wg, wu, *, tm_target=704, tf=512):
    m, d = x.shape
    f = wg.shape[1]
    tm = _tile(m, tm_target, 16)
    return pl.pallas_call(
        _ffn_up_kernel,
        grid=(m // tm, pl.cdiv(f, tf)),
        in_specs=[pl.BlockSpec((tm, d), lambda i, j: (i, 0)),
                  pl.BlockSpec((1, d), lambda i, j: (0, 0)),
                  pl.BlockSpec((d, tf), lambda i, j: (0, j)),
                  pl.BlockSpec((d, tf), lambda i, j: (0, j))],
        out_specs=pl.BlockSpec((tm, tf), lambda i, j: (i, j)),
        out_shape=jax.ShapeDtypeStruct((m, f), BF16),
        scratch_shapes=[pltpu.VMEM((tm, d), BF16)],
        name="ffn_up",
    )(x, g.reshape(1, d), wg, wu)


def _mm_res_kernel(*refs, n_lhs):
    a_refs = refs[:n_lhs]
    w_refs = refs[n_lhs:2 * n_lhs]
    x_ref, o_ref = refs[2 * n_lhs], refs[2 * n_lhs + 1]
    acc = x_ref[...]
    for a_ref, w_ref in zip(a_refs, w_refs):
        acc = acc + jnp.dot(a_ref[...], w_ref[...], preferred_element_type=F32)
    o_ref[...] = acc


def mm_res(lhs, w, x, *, tm_target=704, tn=512):
    m, n = x.shape
    kk = lhs[0].shape[1]
    assert all(a.shape == (m, kk) for a in lhs) and w.shape[0] == kk * len(lhs)
    tm = _tile(m, tm_target, 16)
    tn = min(tn, n)
    nl = len(lhs)
    in_specs = [pl.BlockSpec((tm, kk), lambda i, j: (i, 0)) for _ in lhs]
    in_specs += [pl.BlockSpec((kk, tn), lambda i, j, r=r: (r, j)) for r in range(nl)]
    in_specs += [pl.BlockSpec((tm, tn), lambda i, j: (i, j))]
    return pl.pallas_call(
        functools.partial(_mm_res_kernel, n_lhs=nl),
        grid=(m // tm, n // tn),
        in_specs=in_specs,
        out_specs=pl.BlockSpec((tm, tn), lambda i, j: (i, j)),
        out_shape=jax.ShapeDtypeStruct((m, n), F32),
        name="mm_res",
    )(*lhs, *([w] * nl), x)


def _rms_kernel(x_ref, g_ref, o_ref):
    o_ref[...] = _rms_rows(x_ref[...], g_ref[...])


def rms_rows(x, g, row0, rows, *, tm_target=512):
    d = x.shape[1]
    tm = _tile(rows, tm_target, 8)
    assert row0 % tm == 0
    off = row0 // tm
    return pl.pallas_call(
        _rms_kernel,
        grid=(rows // tm,),
        in_specs=[pl.BlockSpec((tm, d), lambda i: (off + i, 0)),
                  pl.BlockSpec((1, d), lambda i: (0, 0))],
        out_specs=pl.BlockSpec((tm, d), lambda i: (i, 0)),
        out_shape=jax.ShapeDtypeStruct((rows, d), F32),
        name="final_norm",
    )(x, g.reshape(1, d))


def _conv_rows(xp_ref, x, w, first, halo8, taps):
    t = x.shape[0]

    @pl.when(first)
    def _():
        xp_ref[0:SUBLANE, :] = halo8

    xp_ref[SUBLANE:SUBLANE + t, :] = x
    y = x * w[taps - 1:taps, :]
    for i in range(taps - 1):
        sh = taps - 1 - i
        y = y + xp_ref[SUBLANE - sh:SUBLANE - sh + t, :] * w[i:i + 1, :]
    xp_ref[0:SUBLANE, :] = xp_ref[t:t + SUBLANE, :]
    return y


def _split_bf16(a):
    hi = a.astype(BF16)
    return hi, (a - hi.astype(F32)).astype(BF16)


def _dot3(a, b):
    ah, al = _split_bf16(a)
    bh, bl = _split_bf16(b)
    return (jnp.dot(ah, bh, preferred_element_type=F32)
            + (jnp.dot(ah, bl, preferred_element_type=F32) + jnp.dot(al, bh, preferred_element_type=F32)))


def _dot_mask(a, mask):
    a1 = a.astype(BF16)
    r1 = a - a1.astype(F32)
    a2 = r1.astype(BF16)
    a3 = (r1 - a2.astype(F32)).astype(BF16)
    return (jnp.dot(a1, mask, preferred_element_type=F32)
            + (jnp.dot(a2, mask, preferred_element_type=F32) + jnp.dot(a3, mask, preferred_element_type=F32)))


def _unit_lower_inverse(mats, top):
    c = mats[0].shape[0]
    ri = lax.broadcasted_iota(jnp.int32, (c, c), 0)
    ci = lax.broadcasted_iota(jnp.int32, (c, c), 1)
    eye = (ri == ci).astype(F32)
    s = min(INV_BASE, top)
    ps = [jnp.where((ri // s) == (ci // s), -a, 0.0) for a in mats]
    ts = [eye + p for p in ps]
    k = 2
    while k < s:
        ps = [_dot(p, p) for p in ps]
        ts = [t + _dot(t, p) for t, p in zip(ts, ps)]
        k *= 2
    while s < top:
        blk = ((ri // (2 * s)) == (ci // (2 * s))) & ((ri // s) != (ci // s))
        tos = [_dot(t, jnp.where(blk, a, 0.0)) for t, a in zip(ts, mats)]
        ts = [t - _dot(to, t) for t, to in zip(ts, tos)]
        s *= 2
    return ts


def _delta_kernel(q_ref, k_ref, v_ref, gt_ref, ba_ref, cq_ref, ck_ref, cv_ref, wq_ref, wk_ref, wv_ref,
                  s0_ref, alog_ref, dt_ref, on_ref, o_ref, sn_ref, xq_ref, xk_ref, xv_ref, s_ref,
                  *, seg, nseg, hb, dk, chunk, taps, n_heads):
    hblk = pl.program_id(1)
    c = pl.program_id(2)
    rows = seg * nseg
    cc = chunk
    wid = hb * dk

    @pl.when(c == 0)
    def _():
        s_ref[...] = s0_ref[...]
        xq_ref[:, 0:SUBLANE, :] = cq_ref[...]
        xk_ref[:, 0:SUBLANE, :] = ck_ref[...]
        xv_ref[:, 0:SUBLANE, :] = cv_ref[...]

    def conv(x_ref, xp_ref, w_ref):
        w = w_ref[...]
        x3 = x_ref[...].reshape(nseg, seg, wid)
        xp_ref[:, SUBLANE:SUBLANE + seg, :] = x3
        y = x3 * w[taps - 1:taps, :]
        for i in range(taps - 1):
            sh = taps - 1 - i
            y = y + xp_ref[:, SUBLANE - sh:SUBLANE - sh + seg, :] * w[i:i + 1, :]
        xp_ref[:, 0:SUBLANE, :] = xp_ref[:, seg:seg + SUBLANE, :]
        return _silu(y).reshape(rows, wid)

    yq_all = conv(q_ref, xq_ref, wq_ref)
    yk_all = conv(k_ref, xk_ref, wk_ref)
    yv_all = conv(v_ref, xv_ref, wv_ref)

    ba = ba_ref[...]
    lane = lax.broadcasted_iota(jnp.int32, ba.shape, 1)
    sig = _sigmoid(ba)
    gfull = -jnp.exp(alog_ref[...]) * _softplus(ba + dt_ref[...])
    gate = gt_ref[...]

    ri = lax.broadcasted_iota(jnp.int32, (cc, cc), 0)
    ci = lax.broadcasted_iota(jnp.int32, (cc, cc), 1)
    same = (ri // seg) == (ci // seg)
    incl = same & (ci <= ri)
    strict = same & (ci < ri)
    seg_last = (ri // seg) * seg + (seg - 1)
    row_in_seg = ri % seg
    pad = cc - rows

    heads = range(hb)

    def prep(hh):
        h = hblk * hb + hh
        sl = slice(hh * dk, (hh + 1) * dk)
        yq, yk, v = yq_all[:, sl], yk_all[:, sl], yv_all[:, sl]
        q = yq * lax.rsqrt(jnp.sum(yq * yq, axis=-1, keepdims=True) + EPS) * (dk ** -0.5)
        k = yk * lax.rsqrt(jnp.sum(yk * yk, axis=-1, keepdims=True) + EPS)
        beta = jnp.sum(jnp.where(lane == h, sig, 0.0), axis=1, keepdims=True)
        g = jnp.sum(jnp.where(lane == n_heads + h, gfull, 0.0), axis=1, keepdims=True)
        if pad:
            zrow = jnp.zeros((pad, dk), F32)
            q = jnp.concatenate([q, zrow], axis=0)
            k = jnp.concatenate([k, zrow], axis=0)
            v = jnp.concatenate([v, zrow], axis=0)
            beta = jnp.concatenate([beta, jnp.zeros((pad, 1), F32)], axis=0)
            g = jnp.concatenate([g, jnp.zeros((pad, 1), F32)], axis=0)
        gcb = jnp.broadcast_to(g, (cc, cc))
        s = 1
        while s < seg:
            gcb = gcb + jnp.where(row_in_seg >= s, pltpu.roll(gcb, s, 0), 0.0)
            s *= 2
        grow = jnp.sum(jnp.where(ri == ci, gcb, 0.0), axis=0, keepdims=True)
        dec = jnp.where(incl, jnp.exp(jnp.where(incl, gcb - grow, 0.0)), 0.0)
        g_col = gcb[:, 0:1]
        g_end = jnp.sum(jnp.where(ci == seg_last, grow, 0.0), axis=1, keepdims=True)
        return dict(q=q, k=k, v=v, beta=beta, dec=dec, g_col=g_col, g_end=g_end, gamma=jnp.exp(g_col))

    hd_ = [prep(hh) for hh in heads]
    kks = [_dot_nt(p['k'], p['k']) for p in hd_]
    t_invs = _unit_lower_inverse([jnp.where(strict, p['beta'] * kk * p['dec'], 0.0) for p, kk in zip(hd_, kks)], seg)
    sols = [_dot3(t, jnp.concatenate([(p['beta'] * p['gamma']) * p['k'], p['beta'] * p['v']], axis=1))
            for p, t in zip(hd_, t_invs)]
    attns = [_dot_nt(p['q'], p['k']) * p['dec'] for p in hd_]
    kdts = [(p['k'] * jnp.exp(p['g_end'] - p['g_col'])).T for p in hd_]

    us, oqs = [], []
    for hh in heads:
        w_c, u_c = sols[hh][:, :dk], sols[hh][:, dk:]
        q_dec = hd_[hh]['q'] * hd_[hh]['gamma']
        ul, ol = [], []
        for i in range(nseg):
            r = slice(i * seg, (i + 1) * seg)
            st = s_ref[i, hh]
            ul.append(u_c[r] - _dot(w_c[r], st))
            ol.append(_dot(q_dec[r], st))
        if pad:
            ul.append(jnp.zeros((pad, dk), F32))
            ol.append(jnp.zeros((pad, dk), F32))
        us.append(ul[0] if len(ul) == 1 else jnp.concatenate(ul, axis=0))
        oqs.append(ol[0] if len(ol) == 1 else jnp.concatenate(ol, axis=0))
    os_ = [oq + _dot(attn, u) for oq, attn, u in zip(oqs, attns, us)]
    for hh in heads:
        g_end = hd_[hh]['g_end']
        for i in range(nseg):
            kd_i = kdts[hh] if cc == seg else jnp.where((ci // seg) == i, kdts[hh], 0.0)
            s_ref[i, hh] = jnp.exp(g_end[i * seg:i * seg + 1, :]) * s_ref[i, hh] + _dot(kd_i, us[hh])
    for hh in heads:
        sl = slice(hh * dk, (hh + 1) * dk)
        o = os_[hh][:rows]
        o = o * lax.rsqrt(jnp.mean(o * o, axis=-1, keepdims=True) + EPS) * on_ref[...]
        o_ref[:, sl] = (o * _silu(gate[:, sl])).astype(o_ref.dtype)

    @pl.when(c == pl.num_programs(2) - 1)
    def _():
        sn_ref[...] = s_ref[...]


def delta_heads(z, row0, nb, t, conv8, s0, conv_w, alog_row, dt_row, out_norm, *, n_heads, dk, cols, hb):
    cq, ck, cv, cg, cba = cols
    taps = conv_w.shape[0]
    chunk = DELTA_CHUNK
    assert dk == chunk == LANE and t % SUBLANE == 0 and n_heads % hb == 0
    if t >= chunk:
        assert t % chunk == 0
        seg, nseg = chunk, 1
    else:
        assert chunk % t == 0
        seg, nseg = t, _tile(nb, chunk // t, 1)
    rows = seg * nseg
    nc = t // seg
    assert row0 % rows == 0 and all(cb % hb == 0 for cb in (cq, ck, cv, cg))
    r0 = row0 // rows
    wid = hb * dk

    def zspec(cb):
        return pl.BlockSpec((rows, wid), lambda b, h, c: (r0 + b * nc + c, cb // hb + h))

    def cspec(cb):
        return pl.BlockSpec((nseg, SUBLANE, wid), lambda b, h, c: (b, 0, cb // hb + h))

    def wspec(cb):
        return pl.BlockSpec((taps, wid), lambda b, h, c: (0, cb // hb + h))

    row = pl.BlockSpec((1, LANE), lambda b, h, c: (0, 0))
    sspec = pl.BlockSpec((nseg, hb, dk, dk), lambda b, h, c: (b, h, 0, 0))
    kern = functools.partial(_delta_kernel, seg=seg, nseg=nseg, hb=hb, dk=dk, chunk=chunk, taps=taps,
                             n_heads=n_heads)
    return pl.pallas_call(
        kern,
        grid=(nb // nseg, n_heads // hb, nc),
        in_specs=[zspec(cq), zspec(ck), zspec(cv), zspec(cg),
                  pl.BlockSpec((rows, LANE), lambda b, h, c: (r0 + b * nc + c, cba)),
                  cspec(cq), cspec(ck), cspec(cv), wspec(cq), wspec(ck), wspec(cv),
                  sspec, row, row, row],
        out_specs=[pl.BlockSpec((rows, wid), lambda b, h, c: (b * nc + c, h)), sspec],
        out_shape=[jax.ShapeDtypeStruct((nb * t, n_heads * dk), BF16),
                   jax.ShapeDtypeStruct((nb, n_heads, dk, dk), F32)],
        scratch_shapes=[pltpu.VMEM((nseg, SUBLANE + seg, wid), F32)] * 3 + [pltpu.VMEM((nseg, hb, dk, dk), F32)],
        name="delta_heads",
    )(z, z, z, z, z, conv8, conv8, conv8, conv_w, conv_w, conv_w, s0, alog_row, dt_row, out_norm)


def _pool_kernel(u_ref, st_ref, w_ref, sc_ref, o_ref, xp_ref, *, tt, pos0, halo, gw):
    tb = pl.program_id(1)

    @pl.when(tb == 0)
    def _():
        xp_ref[0:halo, :] = st_ref[0]

    u = u_ref[...]
    xp_ref[halo:halo + tt, :] = u
    pos = pos0 + tb * tt + lax.broadcasted_iota(jnp.int32, (tt, 1), 0)
    for gi, wsz in enumerate(POOL_WINDOWS):
        lo, hi = gi * gw, (gi + 1) * gw
        acc = u[:, lo:hi]
        for i in range(1, wsz):
            acc = acc + xp_ref[halo - i:halo - i + tt, lo:hi]
        cnt = jnp.minimum(wsz, pos + 1).astype(F32)
        d = acc * (1.0 / cnt) - u[:, lo:hi]
        y = _dot(d, w_ref[gi]) * sc_ref[:, lo:hi]
        o_ref[:, lo:hi] = y.astype(o_ref.dtype)
    xp_ref[0:halo, :] = xp_ref[tt:tt + halo, :]


def pool_groups(z, row0, nb, t, state16, pool_w, pool_scale, *, col_block, pos0, tt_target=256):
    db = pool_scale.shape[-1]
    halo = state16.shape[1]
    tt = _tile(t, tt_target, SUBLANE)
    assert row0 % tt == 0
    nt, r0 = t // tt, row0 // tt
    gw = db // len(POOL_WINDOWS)
    kern = functools.partial(_pool_kernel, tt=tt, pos0=pos0, halo=halo, gw=gw)
    return pl.pallas_call(
        kern,
        grid=(nb, nt),
        in_specs=[pl.BlockSpec((tt, db), lambda b, i: (r0 + b * nt + i, col_block)),
                  pl.BlockSpec((1, halo, db), lambda b, i: (b, 0, 0)),
                  pl.BlockSpec(pool_w.shape, lambda b, i: (0, 0, 0)),
                  pl.BlockSpec((1, db), lambda b, i: (0, 0))],
        out_specs=pl.BlockSpec((tt, db), lambda b, i: (b * nt + i, 0)),
        out_shape=jax.ShapeDtypeStruct((nb * t, db), BF16),
        scratch_shapes=[pltpu.VMEM((halo + tt, db), F32)],
        name="pool_groups",
    )(z, state16, pool_w, pool_scale.reshape(1, db))


def _rglru_kernel(xc_ref, gc_ref, c8_ref, cw_ref, cb_ref, wa_ref, wx_ref, ba_ref, bx_ref, lam_ref, h0_ref,
                  y_ref, hl_ref, xp_ref, h_ref, *, taps, n_blocks, bs):
    tb = pl.program_id(1)
    first = tb == 0

    @pl.when(first)
    def _():
        h_ref[...] = h0_ref[0]

    x = _conv_rows(xp_ref, xc_ref[...], cw_ref[...], first, c8_ref[0], taps) + cb_ref[...]
    tt = x.shape[0]
    rs, is_ = [], []
    for n in range(n_blocks):
        xb = x[:, n * bs:(n + 1) * bs]
        rs.append(_dot(xb, wa_ref[n]))
        is_.append(_dot(xb, wx_ref[n]))
    r = _sigmoid(jnp.concatenate(rs, axis=1) + ba_ref[...])
    i = _sigmoid(jnp.concatenate(is_, axis=1) + bx_ref[...])
    log_a = -LRU_C * r * _softplus(-lam_ref[...])
    a = jnp.exp(log_a)
    b = jnp.sqrt(-_expm1(2.0 * log_a)) * (i * x)
    row = lax.broadcasted_iota(jnp.int32, a.shape, 0)
    s = 1
    while s < tt:
        keep = row >= s
        b = jnp.where(keep, a * pltpu.roll(b, s, 0) + b, b)
        a = jnp.where(keep, a * pltpu.roll(a, s, 0), a)
        s *= 2
    hs = b + a * h_ref[...]
    h_last = hs[tt - 1:tt, :]
    h_ref[...] = h_last
    y_ref[...] = (hs * _gelu_tanh(gc_ref[...])).astype(y_ref.dtype)

    @pl.when(tb == pl.num_programs(1) - 1)
    def _():
        hl_ref[0] = h_last


def rglru_blocks(z, row0, nb, t, conv8, h0, conv_w, conv_b, wa, ba, wx, bx, lam, *, tt_target=256):
    dc = lam.shape[-1]
    taps = conv_w.shape[0]
    n_blocks, bs = wa.shape[0], wa.shape[1]
    tt = _tile(t, tt_target, SUBLANE)
    assert row0 % tt == 0
    nt, r0 = t // tt, row0 // tt
    vec = pl.BlockSpec((1, dc), lambda b, i: (0, 0))
    kern = functools.partial(_rglru_kernel, taps=taps, n_blocks=n_blocks, bs=bs)
    return pl.pallas_call(
        kern,
        grid=(nb, nt),
        in_specs=[pl.BlockSpec((tt, dc), lambda b, i: (r0 + b * nt + i, 0)),
                  pl.BlockSpec((tt, dc), lambda b, i: (r0 + b * nt + i, 1)),
                  pl.BlockSpec((1, SUBLANE, dc), lambda b, i: (b, 0, 0)),
                  pl.BlockSpec((taps, dc), lambda b, i: (0, 0)),
                  vec,
                  pl.BlockSpec(wa.shape, lambda b, i: (0, 0, 0)),
                  pl.BlockSpec(wx.shape, lambda b, i: (0, 0, 0)),
                  vec, vec, vec,
                  pl.BlockSpec((1, 1, dc), lambda b, i: (b, 0, 0))],
        out_specs=[pl.BlockSpec((tt, dc), lambda b, i: (b * nt + i, 0)),
                   pl.BlockSpec((1, 1, dc), lambda b, i: (b, 0, 0))],
        out_shape=[jax.ShapeDtypeStruct((nb * t, dc), BF16),
                   jax.ShapeDtypeStruct((nb, 1, dc), F32)],
        scratch_shapes=[pltpu.VMEM((SUBLANE + tt, dc), F32), pltpu.VMEM((1, dc), F32)],
        name="rglru_blocks",
    )(z, z, conv8, conv_w, conv_b.reshape(1, dc), wa, wx, ba.reshape(1, dc), bx.reshape(1, dc),
      lam.reshape(1, dc), h0.reshape(nb, 1, dc))


def _fox_prep_kernel(f_ref, bf_ref, lf_ref, cum_ref, cumt_ref, *, n_heads):
    lf = -_softplus(-(f_ref[...] + bf_ref[...]))
    lf_ref[...] = lf
    cum = _cumsum_rows(lf)
    cum_ref[...] = cum
    cumt_ref[0] = cum.T[0:n_heads, :]


def fox_prep(z, nb, t, bf_row, *, col_block, n_heads):
    return pl.pallas_call(
        functools.partial(_fox_prep_kernel, n_heads=n_heads),
        grid=(nb,),
        in_specs=[pl.BlockSpec((t, LANE), lambda b: (b, col_block)),
                  pl.BlockSpec((1, LANE), lambda b: (0, 0))],
        out_specs=[pl.BlockSpec((t, LANE), lambda b: (b, 0)),
                   pl.BlockSpec((t, LANE), lambda b: (b, 0)),
                   pl.BlockSpec((1, n_heads, t), lambda b: (b, 0, 0))],
        out_shape=[jax.ShapeDtypeStruct((nb * t, LANE), F32),
                   jax.ShapeDtypeStruct((nb * t, LANE), F32),
                   jax.ShapeDtypeStruct((nb, n_heads, t), F32)],
        name="fox_prep",
    )(z, bf_row)


def _fox_flash_kernel(qi_ref, ki_ref, q_ref, k_ref, v_ref, fq_ref, fk_ref, o_ref, m_ref, l_ref, acc_ref, fqc_ref,
                      *, scale, hb, hd):
    hblk = pl.program_id(1)
    pair = pl.program_id(2)
    qi, ki = qi_ref[pair], ki_ref[pair]
    tq, tk = q_ref.shape[0], k_ref.shape[0]
    log2e = math.log2(math.e)

    @pl.when(ki == 0)
    def _():
        m_ref[...] = jnp.full(m_ref.shape, NEG, F32)
        l_ref[...] = jnp.zeros(l_ref.shape, F32)
        acc_ref[...] = jnp.zeros(acc_ref.shape, F32)
        fq = fq_ref[...] * log2e
        lane = lax.broadcasted_iota(jnp.int32, fq.shape, 1)
        for hh in range(hb):
            fqc_ref[hh] = jnp.sum(jnp.where(lane == hblk * hb + hh, fq, 0.0), axis=1, keepdims=True)

    def attend(diagonal):
        heads = range(hb)
        kb = k_ref[...].astype(BF16)
        vb = v_ref[...].astype(BF16)
        fks = [fk_ref[0, pl.ds(hblk * hb + hh, 1), :] * log2e for hh in heads]
        band = FLASH_ROW_BAND
        chains = [(r0, hh) for r0 in range(0, tq, band) for hh in heads]
        for c0 in range(0, len(chains), FLASH_CHAINS):
            group = chains[c0:c0 + FLASH_CHAINS]
            ts, nks = [], []
            for r0, hh in group:
                sl = slice(hh * hd, (hh + 1) * hd)
                nk = min(tk, -(-(r0 + band) // LANE) * LANE) if diagonal else tk
                qb = q_ref[r0:r0 + band, sl].astype(BF16)
                t2 = _dot_nt(qb, kb[:nk, sl]) * (scale * log2e) - fks[hh][:, :nk]
                if diagonal:
                    ri = lax.broadcasted_iota(jnp.int32, (band, nk), 0) + r0
                    ci = lax.broadcasted_iota(jnp.int32, (band, nk), 1)
                    t2 = jnp.where(ci <= ri, t2, NEG)
                ts.append(t2)
                nks.append(nk)
            alphas, ps, m_news = [], [], []
            for (r0, hh), t2 in zip(group, ts):
                rs = slice(r0, r0 + band)
                fq = fqc_ref[hh, rs, :]
                m_old = m_ref[hh, rs, :]
                m_new = jnp.maximum(m_old, jnp.max(t2, axis=1, keepdims=True) + fq)
                alphas.append(jnp.exp2(m_old - m_new))
                ps.append(jnp.exp2(t2 + (fq - m_new)))
                m_news.append(m_new)
            for (r0, hh), p, alpha, m_new, nk in zip(group, ps, alphas, m_news, nks):
                rs = slice(r0, r0 + band)
                sl = slice(hh * hd, (hh + 1) * hd)
                l_ref[hh, rs, :] = alpha * l_ref[hh, rs, :] + jnp.sum(p, axis=1, keepdims=True)
                acc_ref[rs, sl] = alpha * acc_ref[rs, sl] + _dot(p, vb[:nk, sl])
                m_ref[hh, rs, :] = m_new

    @pl.when(ki < qi)
    def _():
        attend(False)

    @pl.when(ki == qi)
    def _():
        attend(True)
        for hh in range(hb):
            sl = slice(hh * hd, (hh + 1) * hd)
            o_ref[:, sl] = (acc_ref[:, sl] / l_ref[hh]).astype(o_ref.dtype)


def fox_prompt(z, cum, cumt, nb, t, *, n_heads, hd, cols, tq_target=512, hb=2):
    cq, ck, cv = cols
    tq = _tile(t, tq_target, LANE)
    nq = t // tq
    assert n_heads % hb == 0 and all(cb % hb == 0 for cb in cols)
    pairs = [(i, j) for i in range(nq) for j in range(i + 1)]
    qi_tab = jnp.asarray([p[0] for p in pairs], jnp.int32)
    ki_tab = jnp.asarray([p[1] for p in pairs], jnp.int32)
    wid = hb * hd
    kern = functools.partial(_fox_flash_kernel, scale=hd ** -0.5, hb=hb, hd=hd)

    def kv(cb):
        return pl.BlockSpec((tq, wid), lambda b, h, p, qt, kt: (b * nq + kt[p], cb // hb + h))

    grid_spec = pltpu.PrefetchScalarGridSpec(
        num_scalar_prefetch=2,
        grid=(nb, n_heads // hb, len(pairs)),
        in_specs=[pl.BlockSpec((tq, wid), lambda b, h, p, qt, kt: (b * nq + qt[p], cq // hb + h)),
                  kv(ck), kv(cv),
                  pl.BlockSpec((tq, LANE), lambda b, h, p, qt, kt: (b * nq + qt[p], 0)),
                  pl.BlockSpec((1, n_heads, tq), lambda b, h, p, qt, kt: (b, 0, kt[p]))],
        out_specs=pl.BlockSpec((tq, wid), lambda b, h, p, qt, kt: (b * nq + qt[p], h)),
        scratch_shapes=[pltpu.VMEM((hb, tq, 1), F32), pltpu.VMEM((hb, tq, 1), F32), pltpu.VMEM((tq, wid), F32),
                        pltpu.VMEM((hb, tq, 1), F32)],
    )
    return pl.pallas_call(
        kern,
        grid_spec=grid_spec,
        out_shape=jax.ShapeDtypeStruct((nb * t, n_heads * hd), BF16),
        name="fox_prompt",
    )(qi_tab, ki_tab, z, z, z, cum, cumt)


def _fox_sample_kernel(pt_ref, q_ref, kn_ref, vn_ref, f_ref, bf_ref, *rest, n_heads, hd, pages_per_step, scale):
    g_pages = pages_per_step
    k_refs = rest[0:g_pages]
    v_refs = rest[g_pages:2 * g_pages]
    lf_refs = rest[2 * g_pages:3 * g_pages]
    o_ref, lfo_ref, m_ref, l_ref, acc_ref, carry_ref, fnc_ref = rest[3 * g_pages:]
    step = pl.program_id(1)
    t = q_ref.shape[0]
    page = lf_refs[0].shape[3]
    q = q_ref[...].astype(BF16)

    def update(h, s_list, v_list):
        m_old = m_ref[h]
        smax = s_list[0]
        for s in s_list[1:]:
            smax = jnp.maximum(smax, s)
        m_new = jnp.maximum(m_old, jnp.max(smax, axis=1, keepdims=True))
        alpha = jnp.exp(m_old - m_new)
        psum, pv = None, None
        for s, vh in zip(s_list, v_list):
            p = jnp.exp(s - m_new)
            d = _dot(p, vh)
            psum = p if psum is None else psum + p
            pv = d if pv is None else pv + d
        sl = slice(h * hd, (h + 1) * hd)
        l_ref[h] = alpha * l_ref[h] + jnp.sum(psum, axis=1, keepdims=True)
        acc_ref[:, sl] = alpha * acc_ref[:, sl] + pv
        m_ref[h] = m_new

    @pl.when(step == 0)
    def _():
        lf = -_softplus(-(f_ref[...] + bf_ref[...]))
        lfo_ref[...] = lf
        fn = _cumsum_rows(lf)
        carry_ref[...] = jnp.zeros(carry_ref.shape, F32)
        m_ref[...] = jnp.full(m_ref.shape, NEG, F32)
        l_ref[...] = jnp.zeros(l_ref.shape, F32)
        acc_ref[...] = jnp.zeros(acc_ref.shape, F32)
        kn = jnp.concatenate([kn_ref[...], jnp.zeros((page - t, n_heads * hd), F32)], axis=0)
        vn = jnp.concatenate([vn_ref[...], jnp.zeros((page - t, n_heads * hd), F32)], axis=0)
        ri = lax.broadcasted_iota(jnp.int32, (t, page), 0)
        ci = lax.broadcasted_iota(jnp.int32, (t, page), 1)
        for h in range(n_heads):
            col = lf[:, h:h + 1]
            fnrow = jnp.sum(jnp.where(ri <= ci, col, 0.0), axis=0, keepdims=True)
            fnc = jnp.broadcast_to(fn[:, h:h + 1], (t, page))
            fnc_ref[h] = fnc
            sl = slice(h * hd, (h + 1) * hd)
            s = _dot_nt(q[:, sl], kn[:, sl]) * scale + fnc - fnrow
            update(h, [jnp.where(ci <= ri, s, NEG)], [vn[:, sl]])

    lft = jnp.concatenate([lf_refs[g][0, 0] for g in range(g_pages)], axis=0)
    ri = lax.broadcasted_iota(jnp.int32, (page, page), 0)
    ci = lax.broadcasted_iota(jnp.int32, (page, page), 1)
    rr = _dot_mask(lft, (ri > ci).astype(BF16))
    tot = jnp.sum(lft, axis=1, keepdims=True)
    run = carry_ref[:, 0:1]
    rc = [None] * g_pages
    for g in reversed(range(g_pages)):
        rc[g] = rr[g * n_heads:(g + 1) * n_heads] + run
        run = run + tot[g * n_heads:(g + 1) * n_heads]
    carry_ref[...] = jnp.broadcast_to(run, carry_ref.shape)

    def head_tile(ref, h):
        return ref[0, 0, pl.ds(h, page, stride=n_heads), :]

    scores = []
    for h in range(n_heads):
        qh = q[:, h * hd:(h + 1) * hd]
        fnc = fnc_ref[h]
        scores.append([_dot_nt(qh, head_tile(k_refs[g], h)) * scale + fnc + rc[g][h:h + 1, :]
                       for g in range(g_pages)])
    m_news, alphas, probs = [], [], []
    for h in range(n_heads):
        smax = scores[h][0]
        for s in scores[h][1:]:
            smax = jnp.maximum(smax, s)
        m_old = m_ref[h]
        m_new = jnp.maximum(m_old, jnp.max(smax, axis=1, keepdims=True))
        m_news.append(m_new)
        alphas.append(jnp.exp(m_old - m_new))
        probs.append([jnp.exp(s - m_new) for s in scores[h]])
    for h in range(n_heads):
        sl = slice(h * hd, (h + 1) * hd)
        psum, pv = None, None
        for g in range(g_pages):
            p = probs[h][g]
            d = _dot(p, head_tile(v_refs[g], h))
            psum = p if psum is None else psum + p
            pv = d if pv is None else pv + d
        l_ref[h] = alphas[h] * l_ref[h] + jnp.sum(psum, axis=1, keepdims=True)
        acc_ref[:, sl] = alphas[h] * acc_ref[:, sl] + pv
        m_ref[h] = m_news[h]

    @pl.when(step == pl.num_programs(1) - 1)
    def _():
        for h in range(n_heads):
            sl = slice(h * hd, (h + 1) * hd)
            o_ref[:, sl] = (acc_ref[:, sl] / l_ref[h]).astype(o_ref.dtype)


def fox_sample(z, row0, nb, t, page_table, kc, vc, lfc, layer, bf_row, *, n_heads, hd, cols, pages_per_step=8):
    cq, ck, cv, cf = cols
    n_pages = page_table.shape[1]
    page = lfc.shape[3]
    g_pages = pages_per_step
    while n_pages % g_pages:
        g_pages //= 2
    ns = n_pages // g_pages
    dd = n_heads * hd
    assert row0 % t == 0 and t % SUBLANE == 0
    r0 = row0 // t

    def zspec(cb, w):
        return pl.BlockSpec((t, w), lambda b, s, pt: (r0 + b, cb))

    def pspec(g, r, w):
        return pl.BlockSpec((1, 1, r, w), lambda b, s, pt, g=g: (layer, pt[b, (ns - 1 - s) * g_pages + g], 0, 0))

    kern = functools.partial(_fox_sample_kernel, n_heads=n_heads, hd=hd, pages_per_step=g_pages, scale=hd ** -0.5)
    grid_spec = pltpu.PrefetchScalarGridSpec(
        num_scalar_prefetch=1,
        grid=(nb, ns),
        in_specs=[zspec(cq, dd), zspec(ck, dd), zspec(cv, dd), zspec(cf, LANE),
                  pl.BlockSpec((1, LANE), lambda b, s, pt: (0, 0))]
        + [pspec(g, page * n_heads, hd) for g in range(g_pages)] * 2
        + [pspec(g, n_heads, page) for g in range(g_pages)],
        out_specs=[pl.BlockSpec((t, dd), lambda b, s, pt: (b, 0)),
                   pl.BlockSpec((t, LANE), lambda b, s, pt: (b, 0))],
        scratch_shapes=[pltpu.VMEM((n_heads, t, 1), F32), pltpu.VMEM((n_heads, t, 1), F32),
                        pltpu.VMEM((t, dd), F32), pltpu.VMEM((n_heads, LANE), F32),
                        pltpu.VMEM((n_heads, t, page), F32)],
    )
    return pl.pallas_call(
        kern,
        grid_spec=grid_spec,
        out_shape=[jax.ShapeDtypeStruct((nb * t, dd), BF16), jax.ShapeDtypeStruct((nb * t, LANE), F32)],
        name="fox_sample",
    )(page_table, z, z, z, z, bf_row, *([kc] * g_pages), *([vc] * g_pages), *([lfc] * g_pages))


def _xattn_kernel(q_ref, k_ref, v_ref, o_ref, *, n_heads, hd, interleaved, scale):
    q = q_ref[...]
    for h in range(n_heads):
        sl = slice(h * hd, (h + 1) * hd)
        if interleaved:
            n_mem = k_ref.shape[2] // n_heads
            kh = k_ref[0, 0, pl.ds(h, n_mem, stride=n_heads), :]
            vh = v_ref[0, 0, pl.ds(h, n_mem, stride=n_heads), :]
        else:
            kh, vh = k_ref[:, sl], v_ref[:, sl]
        s = _dot_nt(q[:, sl], kh) * scale
        p = jnp.exp(s - jnp.max(s, axis=1, keepdims=True))
        o = _dot(p, vh) / jnp.sum(p, axis=1, keepdims=True)
        o_ref[:, sl] = o.astype(o_ref.dtype)


def xattn(qx, row0, nb, t, mk, mv, kspec, vspec, *, n_heads, hd, interleaved, tt_target=512):
    dx = n_heads * hd
    tt = _tile(t, tt_target, SUBLANE)
    assert row0 % tt == 0
    nt, r0 = t // tt, row0 // tt
    kern = functools.partial(_xattn_kernel, n_heads=n_heads, hd=hd, interleaved=interleaved, scale=hd ** -0.5)
    return pl.pallas_call(
        kern,
        grid=(nb, nt),
        in_specs=[pl.BlockSpec((tt, dx), lambda b, i: (r0 + b * nt + i, 0)), kspec, vspec],
        out_specs=pl.BlockSpec((tt, dx), lambda b, i: (b * nt + i, 0)),
        out_shape=jax.ShapeDtypeStruct((nb * t, dx), BF16),
        name="xattn",
    )(qx, mk, mv)


def _pad_cols(w, n):
    return jnp.pad(w, ((0, 0), (0, n - w.shape[1])))


def _halo_rows(buf, rows):
    return jnp.pad(buf, ((0, 0), (rows - buf.shape[1], 0), (0, 0)))


def kernel(x_prompt, x_sample, state_delta, state_conv_a, state_pool, state_lru, state_conv_c, cache_fox_k, cache_fox_v, cache_fox_logf, cache_mem_k, cache_mem_v, page_table, mem_prompt, ffn1_wg, ffn1_wu, ffn1_wd, ffn2_wg, ffn2_wu, ffn2_wd, norm_ffn1, norm_mix, norm_xattn, norm_ffn2, norm_mem, final_norm, xattn_wq, xattn_wk, xattn_wv, xattn_wo, even_w_in, even_w_out, delta_conv_w, delta_a_log, delta_dt_bias, delta_out_norm, pool_w, pool_scale, odd_w_in, odd_w_out, lru_conv_w, lru_conv_b, lru_wa, lru_ba, lru_wx, lru_bx, lru_lambda, fox_bf):
    bp, tp, d = x_prompt.shape
    bs, ts, _ = x_sample.shape
    mp, ms = bp * tp, bs * ts
    depth = ffn1_wg.shape[0]
    h_a, dk_a = state_delta.shape[2], state_delta.shape[3]
    d_a = h_a * dk_a
    d_b = state_pool.shape[-1]
    d_c = state_lru.shape[-1]
    h_d, hd_d = cache_fox_k.shape[3], cache_fox_k.shape[4]
    d_d = h_d * hd_d
    n_mem, h_x, hd_x = cache_mem_k.shape[2], cache_mem_k.shape[3], cache_mem_k.shape[4]
    d_x = h_x * hd_x
    page = cache_fox_k.shape[2]
    pos0_s = page_table.shape[1] * page
    assert d_a == d_b and d_c == d_d and 2 * h_a <= LANE and h_d <= LANE
    tn_z = 768
    nz = -(-(max(4 * d_a + d_b, 2 * d_c + 3 * d_d) + LANE) // tn_z) * tn_z
    bc =(4 * d_a + d_b) // LANE
    assert bc == (2 * d_c + 3 * d_d) // LANE

    x = jnp.concatenate([x_prompt.reshape(mp, d), x_sample.reshape(ms, d)], axis=0)

    mem2 = mem_prompt.reshape(bp * n_mem, d)
    mem_kv = []
    for l in range(depth):
        wkv = jnp.concatenate([xattn_wk[l], xattn_wv[l]], axis=1).astype(BF16)
        mem_kv.append(norm_mm(mem2, norm_mem[l], wkv, tn=d_x))
    kc4 = cache_fox_k.reshape(cache_fox_k.shape[0], cache_fox_k.shape[1], page * h_d, hd_d)
    vc4 = cache_fox_v.reshape(kc4.shape)
    lfc4 = jnp.swapaxes(cache_fox_logf, 2, 3)
    mk4 = cache_mem_k.reshape(depth, bs, n_mem * h_x, hd_x)
    mv4 = cache_mem_v.reshape(depth, bs, n_mem * h_x, hd_x)

    new_p = {n: [] for n in ('delta', 'conv_a', 'pool', 'lru', 'conv_c', 'fox_k', 'fox_v', 'fox_logf')}
    new_s = {n: [] for n in new_p}

    def seq_cols(zz, row0, nb, t, c0, c1, last=None):
        if last is None or last >= t:
            return lax.slice(zz, (row0, c0), (row0 + nb * t, c1)).reshape(nb, t, c1 - c0)
        idx = (row0 + jnp.arange(nb)[:, None] * t + (t - last) + jnp.arange(last)[None, :]).reshape(-1)
        return lax.slice_in_dim(zz[idx], c0, c1, axis=1).reshape(nb, last, c1 - c0)

    def new_buf(buf, zz, row0, nb, t, c0, c1):
        r = buf.shape[1]
        if t >= r:
            return seq_cols(zz, row0, nb, t, c0, c1, last=r)
        return jnp.concatenate([buf[:, t:], seq_cols(zz, row0, nb, t, c0, c1)], axis=1)

    for l in range(depth):
        j = l // 2
        a = ffn_up(x, norm_ffn1[l], ffn1_wg[l].astype(BF16), ffn1_wu[l].astype(BF16))
        x = mm_res([a], ffn1_wd[l].astype(BF16), x)

        if l % 2 == 0:
            w = even_w_in[j]
            w_in = _pad_cols(jnp.concatenate([w[:, :4 * d_a], w[:, 4 * d_a + 2 * h_a:], w[:, 4 * d_a:4 * d_a + 2 * h_a]],
                                             axis=1), nz).astype(BF16)
            z = norm_mm(x, norm_mix[l], w_in)
            hb = dk_a // LANE
            alog_row = jnp.zeros((1, LANE), F32).at[0, h_a:2 * h_a].set(delta_a_log[j])
            dt_row = jnp.zeros((1, LANE), F32).at[0, h_a:2 * h_a].set(delta_dt_bias[j])
            groups = ((0, bp, tp, jnp.zeros((bp,) + state_conv_a.shape[2:], F32),
                       jnp.zeros((bp,) + state_delta.shape[2:], F32), jnp.zeros((bp,) + state_pool.shape[2:], F32), 0, new_p),
                      (mp, bs, ts, state_conv_a[j], state_delta[j], state_pool[j], pos0_s, new_s))
            mix_a, mix_b = [], []
            for gi, (row0, nb, t, cbuf, s0, pbuf, pos0, new) in enumerate(groups):
                o, s_new = delta_heads(z, row0, nb, t, _halo_rows(cbuf, SUBLANE), s0, delta_conv_w[j], alog_row, dt_row,
                                       delta_out_norm[j].reshape(1, dk_a), n_heads=h_a, dk=dk_a,
                                       cols=(0, h_a * hb, 2 * h_a * hb, 3 * h_a * hb, bc),
                                       hb=DELTA_HEADS_PER_STEP if t >= DELTA_CHUNK else 1)
                yb = pool_groups(z, row0, nb, t, _halo_rows(pbuf, 2 * SUBLANE), pool_w[j].astype(BF16), pool_scale[j],
                                 col_block=4 * d_a // d_b, pos0=pos0)
                mix_a.append(o)
                mix_b.append(yb)
                new['delta'].append(s_new)
                new['conv_a'].append(new_buf(cbuf, z, row0, nb, t, 0, 3 * d_a))
                new['pool'].append(new_buf(pbuf, z, row0, nb, t, 4 * d_a, 4 * d_a + d_b))
            w_out = even_w_out[j].astype(BF16)
        else:
            w_in = _pad_cols(odd_w_in[j], nz).astype(BF16)
            z = norm_mm(x, norm_mix[l], w_in)
            bf_row = jnp.zeros((1, LANE), F32).at[0, :h_d].set(fox_bf[j])
            qb, kb, vb = 2 * d_c // hd_d, (2 * d_c + d_d) // hd_d, (2 * d_c + 2 * d_d) // hd_d
            groups = ((0, bp, tp, jnp.zeros((bp,) + state_conv_c.shape[2:], F32), jnp.zeros((bp, d_c), F32), new_p),
                      (mp, bs, ts, state_conv_c[j], state_lru[j], new_s))
            mix_a, mix_b = [], []
            for gi, (row0, nb, t, cbuf, h0, new) in enumerate(groups):
                yc, h_last = rglru_blocks(z, row0, nb, t, _halo_rows(cbuf, SUBLANE), h0, lru_conv_w[j], lru_conv_b[j],
                                          lru_wa[j].astype(BF16), lru_ba[j], lru_wx[j].astype(BF16), lru_bx[j],
                                          lru_lambda[j])
                if gi == 0:
                    lf, cum, cumt = fox_prep(z, nb, t, bf_row, col_block=bc, n_heads=h_d)
                    o = fox_prompt(z, cum, cumt, nb, t, n_heads=h_d, hd=hd_d, cols=(qb, kb, vb))
                else:
                    o, lf = fox_sample(z, row0, nb, t, page_table, kc4, vc4, lfc4, j, bf_row,
                                       n_heads=h_d, hd=hd_d,
                                       cols=(2 * d_c // d_d, (2 * d_c + d_d) // d_d, (2 * d_c + 2 * d_d) // d_d, bc))
                mix_a.append(yc)
                mix_b.append(o)
                new['lru'].append(h_last.reshape(nb, d_c))
                new['conv_c'].append(new_buf(cbuf, z, row0, nb, t, 0, d_c))
                new['fox_k'].append(seq_cols(z, row0, nb, t, 2 * d_c + d_d, 2 * d_c + 2 * d_d).reshape(nb, t, h_d, hd_d))
                new['fox_v'].append(seq_cols(z, row0, nb, t, 2 * d_c + 2 * d_d, 2 * d_c + 3 * d_d).reshape(nb, t, h_d, hd_d))
                new['fox_logf'].append(lf[:, :h_d].reshape(nb, t, h_d))
            w_out = odd_w_out[j].astype(BF16)

        x = mm_res([jnp.concatenate(mix_a, axis=0), jnp.concatenate(mix_b, axis=0)], w_out, x)

        qx = norm_mm(x, norm_xattn[l], xattn_wq[l].astype(BF16), tn=d_x)
        o_p = xattn(qx, 0, bp, tp, mem_kv[l], mem_kv[l],
                    pl.BlockSpec((n_mem, d_x), lambda b, i: (b, 0)), pl.BlockSpec((n_mem, d_x), lambda b, i: (b, 1)),
                    n_heads=h_x, hd=hd_x, interleaved=False)
        o_s = xattn(qx, mp, bs, ts, mk4, mv4,
                    pl.BlockSpec((1, 1, n_mem * h_x, hd_x), lambda b, i, l=l: (l, b, 0, 0)),
                    pl.BlockSpec((1, 1, n_mem * h_x, hd_x), lambda b, i, l=l: (l, b, 0, 0)),
                    n_heads=h_x, hd=hd_x, interleaved=True)
        x = mm_res([jnp.concatenate([o_p, o_s], axis=0)], xattn_wo[l].astype(BF16), x)

        a = ffn_up(x, norm_ffn2[l], ffn2_wg[l].astype(BF16), ffn2_wu[l].astype(BF16))
        x = mm_res([a], ffn2_wd[l].astype(BF16), x)

    y_p = rms_rows(x, final_norm, 0, mp).reshape(bp, tp, d)
    y_s = rms_rows(x, final_norm, mp, ms).reshape(bs, ts, d)
    mem_k_p = jnp.stack([m[:, :d_x].reshape(bp, n_mem, h_x, hd_x) for m in mem_kv])
    mem_v_p = jnp.stack([m[:, d_x:].reshape(bp, n_mem, h_x, hd_x) for m in mem_kv])
    names = ('delta', 'conv_a', 'pool', 'lru', 'conv_c', 'fox_k', 'fox_v', 'fox_logf')
    return ((y_p, y_s) + tuple(jnp.stack(new_p[n]) for n in names) + (mem_k_p, mem_v_p)
            + tuple(jnp.stack(new_s[n]) for n in names))
```

```python
import functools
import math

import jax
import jax.numpy as jnp
from jax import lax
from jax.experimental import pallas as pl
from jax.experimental.pallas import tpu as pltpu

F32 = jnp.float32
BF16 = jnp.bfloat16
EPS = 1e-6
LRU_C = 8.0
POOL_WINDOWS = (2, 4, 8, 16)
LANE = 128
SUBLANE = 8
DELTA_CHUNK = 128
FFN_DOWN_TN = 256
DELTA_HEADS_PER_STEP = 4
INV_BASE = 16
FLASH_ROW_BAND = 128
FLASH_CHAINS = 2
NEG = -1e30


def _tile(n, target, align):
    best = None
    for d in range(align, min(n, target) + 1, align):
        if n % d == 0:
            best = d
    return n if best is None else best


def _dot(a, b, hi=False):
    if hi:
        return jnp.dot(a.astype(F32), b.astype(F32), precision=lax.Precision.HIGHEST, preferred_element_type=F32)
    return jnp.dot(a.astype(BF16), b.astype(BF16), preferred_element_type=F32)


def _dot_nt(a, b):
    return lax.dot_general(a.astype(BF16), b.astype(BF16), (((1,), (1,)), ((), ())), preferred_element_type=F32)


def _sigmoid(x):
    return 1.0 / (1.0 + jnp.exp(-x))


def _silu(x):
    return x * _sigmoid(x)


def _softplus(x):
    return jnp.maximum(x, 0.0) + jnp.log1p(jnp.exp(-jnp.abs(x)))


def _expm1(x):
    u = jnp.exp(x)
    degenerate = (u == 1.0) | (u == 0.0)
    safe = jnp.where(degenerate, 2.0, u)
    return jnp.where(u == 1.0, x, jnp.where(u == 0.0, -1.0, (u - 1.0) * x / jnp.log(safe)))


def _gelu_tanh(x):
    return 0.5 * x * (1.0 + jnp.tanh(math.sqrt(2.0 / math.pi) * (x + 0.044715 * (x * x * x))))


def _cumsum_rows(x):
    n = x.shape[0]
    row = lax.broadcasted_iota(jnp.int32, x.shape, 0)
    s = 1
    while s < n:
        x = x + jnp.where(row >= s, pltpu.roll(x, s, 0), 0.0)
        s *= 2
    return x


def _rms_rows(x, g):
    ms = jnp.mean(x * x, axis=-1, keepdims=True)
    return x * lax.rsqrt(ms + EPS) * g


def _norm_mm_kernel(x_ref, g_ref, w_ref, o_ref, hn_ref):
    @pl.when(pl.program_id(1) == 0)
    def _():
        hn_ref[...] = _rms_rows(x_ref[...], g_ref[...]).astype(BF16)

    o_ref[...] = jnp.dot(hn_ref[...], w_ref[0].astype(BF16), preferred_element_type=F32).astype(o_ref.dtype)


def norm_mm(x, g, w, layer=0, *, tm_target=704, tn=768, out_dtype=F32):
    m, d = x.shape
    w = w.reshape((1,) + w.shape) if w.ndim == 2 else w
    n = w.shape[2]
    tm = _tile(m, tm_target, 16)
    tn = min(tn, n)
    return pl.pallas_call(
        _norm_mm_kernel,
        grid=(m // tm, pl.cdiv(n, tn)),
        in_specs=[pl.BlockSpec((tm, d), lambda i, j: (i, 0)),
                  pl.BlockSpec((1, d), lambda i, j: (0, 0)),
                  pl.BlockSpec((1, d, tn), lambda i, j: (layer, 0, j))],
        out_specs=pl.BlockSpec((tm, tn), lambda i, j: (i, j)),
        out_shape=jax.ShapeDtypeStruct((m, n), out_dtype),
        scratch_shapes=[pltpu.VMEM((tm, d), BF16)],
        name="norm_mm",
    )(x, g.reshape(1, d), w)


def _ffn_up_kernel(x_ref, g_ref, wg_ref, wu_ref, o_ref, hn_ref):
    @pl.when(pl.program_id(1) == 0)
    def _():
        hn_ref[...] = _rms_rows(x_ref[...], g_ref[...]).astype(BF16)

    hn = hn_ref[...]
    gate = jnp.dot(hn, wg_ref[0].astype(BF16), preferred_element_type=F32)
    up = jnp.dot(hn, wu_ref[0].astype(BF16), preferred_element_type=F32)
    o_ref[...] = (0.5 * _silu(gate) * up).astype(o_ref.dtype)


def ffn_up(x, g, wg, wu, layer, *, tm_target=1056, tf=512):
    m, d = x.shape
    f = wg.shape[2]
    tm = _tile(m, tm_target, 16)
    return pl.pallas_call(
        _ffn_up_kernel,
        grid=(m // tm, pl.cdiv(f, tf)),
        in_specs=[pl.BlockSpec((tm, d), lambda i, j: (i, 0)),
                  pl.BlockSpec((1, d), lambda i, j: (0, 0)),
                  pl.BlockSpec((1, d, tf), lambda i, j: (layer, 0, j)),
                  pl.BlockSpec((1, d, tf), lambda i, j: (layer, 0, j))],
        out_specs=pl.BlockSpec((tm, tf), lambda i, j: (i, j)),
        out_shape=jax.ShapeDtypeStruct((m, f), BF16),
        scratch_shapes=[pltpu.VMEM((tm, d), BF16)],
        name="ffn_up",
    )(x, g.reshape(1, d), wg, wu)


def _mm_res_kernel(*refs, n_lhs):
    a_refs = refs[:n_lhs]
    w_refs = refs[n_lhs:2 * n_lhs]
    x_ref, o_ref = refs[2 * n_lhs], refs[2 * n_lhs + 1]
    acc = x_ref[...]
    for a_ref, w_ref in zip(a_refs, w_refs):
        acc = acc + jnp.dot(a_ref[...], w_ref[0].astype(BF16), preferred_element_type=F32)
    o_ref[...] = acc


def mm_res(lhs, w, layer, x, *, tm_target=704, tn=512):
    m, n = x.shape
    kk = lhs[0].shape[1]
    assert all(a.shape == (m, kk) for a in lhs) and w.shape[1] == kk * len(lhs)
    tm = _tile(m, tm_target, 16)
    tn = min(tn, n)
    nl = len(lhs)
    in_specs = [pl.BlockSpec((tm, kk), lambda i, j: (i, 0)) for _ in lhs]
    in_specs += [pl.BlockSpec((1, kk, tn), lambda i, j, r=r: (layer, r, j)) for r in range(nl)]
    in_specs += [pl.BlockSpec((tm, tn), lambda i, j: (i, j))]
    return pl.pallas_call(
        functools.partial(_mm_res_kernel, n_lhs=nl),
        grid=(m // tm, n // tn),
        in_specs=in_specs,
        out_specs=pl.BlockSpec((tm, tn), lambda i, j: (i, j)),
        out_shape=jax.ShapeDtypeStruct((m, n), F32),
        name="mm_res",
    )(*lhs, *([w] * nl), x)


def _rms_kernel(x_ref, g_ref, o_ref):
    o_ref[...] = _rms_rows(x_ref[...], g_ref[...])


def rms_rows(x, g, row0, rows, *, tm_target=512):
    d = x.shape[1]
    tm = _tile(rows, tm_target, 8)
    assert row0 % tm == 0
    off = row0 // tm
    return pl.pallas_call(
        _rms_kernel,
        grid=(rows // tm,),
        in_specs=[pl.BlockSpec((tm, d), lambda i: (off + i, 0)),
                  pl.BlockSpec((1, d), lambda i: (0, 0))],
        out_specs=pl.BlockSpec((tm, d), lambda i: (i, 0)),
        out_shape=jax.ShapeDtypeStruct((rows, d), F32),
        name="final_norm",
    )(x, g.reshape(1, d))


def _conv_rows(xp_ref, x, w, first, halo8, taps):
    t = x.shape[0]

    @pl.when(first)
    def _():
        xp_ref[0:SUBLANE, :] = halo8

    xp_ref[SUBLANE:SUBLANE + t, :] = x
    y = x * w[taps - 1:taps, :]
    for i in range(taps - 1):
        sh = taps - 1 - i
        y = y + xp_ref[SUBLANE - sh:SUBLANE - sh + t, :] * w[i:i + 1, :]
    xp_ref[0:SUBLANE, :] = xp_ref[t:t + SUBLANE, :]
    return y


def _split_bf16(a):
    hi = a.astype(BF16)
    return hi, (a - hi.astype(F32)).astype(BF16)


def _dot3(a, b):
    ah, al = _split_bf16(a)
    bh, bl = _split_bf16(b)
    return (jnp.dot(ah, bh, preferred_element_type=F32)
            + (jnp.dot(ah, bl, preferred_element_type=F32) + jnp.dot(al, bh, preferred_element_type=F32)))


def _dot_mask(a, mask):
    a1 = a.astype(BF16)
    r1 = a - a1.astype(F32)
    a2 = r1.astype(BF16)
    a3 = (r1 - a2.astype(F32)).astype(BF16)
    return (jnp.dot(a1, mask, preferred_element_type=F32)
            + (jnp.dot(a2, mask, preferred_element_type=F32) + jnp.dot(a3, mask, preferred_element_type=F32)))


def _unit_lower_inverse(mats, top):
    c = mats[0].shape[0]
    ri = lax.broadcasted_iota(jnp.int32, (c, c), 0)
    ci = lax.broadcasted_iota(jnp.int32, (c, c), 1)
    eye = (ri == ci).astype(F32)
    s = min(INV_BASE, top)
    ps = [jnp.where((ri // s) == (ci // s), -a, 0.0) for a in mats]
    ts = [eye + p for p in ps]
    k = 2
    while k < s:
        ps = [_dot(p, p) for p in ps]
        ts = [t + _dot(t, p) for t, p in zip(ts, ps)]
        k *= 2
    while s < top:
        blk = ((ri // (2 * s)) == (ci // (2 * s))) & ((ri // s) != (ci // s))
        tos = [_dot(t, jnp.where(blk, a, 0.0)) for t, a in zip(ts, mats)]
        ts = [t - _dot(to, t) for t, to in zip(ts, tos)]
        s *= 2
    return ts


def _delta_kernel(q_ref, k_ref, v_ref, gt_ref, ba_ref, cq_ref, ck_ref, cv_ref, wq_ref, wk_ref, wv_ref,
                  s0_ref, alog_ref, dt_ref, on_ref, o_ref, sn_ref, xq_ref, xk_ref, xv_ref, s_ref,
                  *, seg, nseg, hb, dk, chunk, taps, n_heads):
    hblk = pl.program_id(1)
    c = pl.program_id(2)
    rows = seg * nseg
    cc = chunk
    wid = hb * dk

    @pl.when(c == 0)
    def _():
        s_ref[...] = s0_ref[...]
        xq_ref[:, 0:SUBLANE, :] = cq_ref[...]
        xk_ref[:, 0:SUBLANE, :] = ck_ref[...]
        xv_ref[:, 0:SUBLANE, :] = cv_ref[...]

    def conv(x_ref, xp_ref, w_ref):
        w = w_ref[...]
        x3 = x_ref[...].reshape(nseg, seg, wid)
        xp_ref[:, SUBLANE:SUBLANE + seg, :] = x3
        y = x3 * w[taps - 1:taps, :]
        for i in range(taps - 1):
            sh = taps - 1 - i
            y = y + xp_ref[:, SUBLANE - sh:SUBLANE - sh + seg, :] * w[i:i + 1, :]
        xp_ref[:, 0:SUBLANE, :] = xp_ref[:, seg:seg + SUBLANE, :]
        return _silu(y).reshape(rows, wid)

    yq_all = conv(q_ref, xq_ref, wq_ref)
    yk_all = conv(k_ref, xk_ref, wk_ref)
    yv_all = conv(v_ref, xv_ref, wv_ref)

    ba = ba_ref[...]
    lane = lax.broadcasted_iota(jnp.int32, ba.shape, 1)
    sig = _sigmoid(ba)
    gfull = -jnp.exp(alog_ref[...]) * _softplus(ba + dt_ref[...])
    gate = gt_ref[...]

    ri = lax.broadcasted_iota(jnp.int32, (cc, cc), 0)
    ci = lax.broadcasted_iota(jnp.int32, (cc, cc), 1)
    same = (ri // seg) == (ci // seg)
    incl = same & (ci <= ri)
    strict = same & (ci < ri)
    seg_last = (ri // seg) * seg + (seg - 1)
    row_in_seg = ri % seg
    pad = cc - rows

    heads = range(hb)

    def prep(hh):
        h = hblk * hb + hh
        sl = slice(hh * dk, (hh + 1) * dk)
        yq, yk, v = yq_all[:, sl], yk_all[:, sl], yv_all[:, sl]
        q = yq * lax.rsqrt(jnp.sum(yq * yq, axis=-1, keepdims=True) + EPS) * (dk ** -0.5)
        k = yk * lax.rsqrt(jnp.sum(yk * yk, axis=-1, keepdims=True) + EPS)
        beta = jnp.sum(jnp.where(lane == h, sig, 0.0), axis=1, keepdims=True)
        g = jnp.sum(jnp.where(lane == n_heads + h, gfull, 0.0), axis=1, keepdims=True)
        if pad:
            zrow = jnp.zeros((pad, dk), F32)
            q = jnp.concatenate([q, zrow], axis=0)
            k = jnp.concatenate([k, zrow], axis=0)
            v = jnp.concatenate([v, zrow], axis=0)
            beta = jnp.concatenate([beta, jnp.zeros((pad, 1), F32)], axis=0)
            g = jnp.concatenate([g, jnp.zeros((pad, 1), F32)], axis=0)
        gcb = jnp.broadcast_to(g, (cc, cc))
        s = 1
        while s < seg:
            gcb = gcb + jnp.where(row_in_seg >= s, pltpu.roll(gcb, s, 0), 0.0)
            s *= 2
        grow = jnp.sum(jnp.where(ri == ci, gcb, 0.0), axis=0, keepdims=True)
        dec = jnp.where(incl, jnp.exp(jnp.where(incl, gcb - grow, 0.0)), 0.0)
        g_col = gcb[:, 0:1]
        g_end = jnp.sum(jnp.where(ci == seg_last, grow, 0.0), axis=1, keepdims=True)
        return dict(q=q, k=k, v=v, beta=beta, dec=dec, g_col=g_col, g_end=g_end, gamma=jnp.exp(g_col))

    hd_ = [prep(hh) for hh in heads]
    kks = [_dot_nt(p['k'], p['k']) for p in hd_]
    t_invs = _unit_lower_inverse([jnp.where(strict, p['beta'] * kk * p['dec'], 0.0) for p, kk in zip(hd_, kks)], seg)
    sols = [_dot3(t, jnp.concatenate([(p['beta'] * p['gamma']) * p['k'], p['beta'] * p['v']], axis=1))
            for p, t in zip(hd_, t_invs)]
    attns = [_dot_nt(p['q'], p['k']) * p['dec'] for p in hd_]
    kdts = [(p['k'] * jnp.exp(p['g_end'] - p['g_col'])).T for p in hd_]

    us, oqs = [], []
    for hh in heads:
        w_c, u_c = sols[hh][:, :dk], sols[hh][:, dk:]
        q_dec = hd_[hh]['q'] * hd_[hh]['gamma']
        ul, ol = [], []
        for i in range(nseg):
            r = slice(i * seg, (i + 1) * seg)
            st = s_ref[i, hh]
            ul.append(u_c[r] - _dot(w_c[r], st))
            ol.append(_dot(q_dec[r], st))
        if pad:
            ul.append(jnp.zeros((pad, dk), F32))
            ol.append(jnp.zeros((pad, dk), F32))
        us.append(ul[0] if len(ul) == 1 else jnp.concatenate(ul, axis=0))
        oqs.append(ol[0] if len(ol) == 1 else jnp.concatenate(ol, axis=0))
    os_ = [oq + _dot(attn, u) for oq, attn, u in zip(oqs, attns, us)]
    for hh in heads:
        g_end = hd_[hh]['g_end']
        for i in range(nseg):
            kd_i = kdts[hh] if cc == seg else jnp.where((ci // seg) == i, kdts[hh], 0.0)
            s_ref[i, hh] = jnp.exp(g_end[i * seg:i * seg + 1, :]) * s_ref[i, hh] + _dot(kd_i, us[hh])
    for hh in heads:
        sl = slice(hh * dk, (hh + 1) * dk)
        o = os_[hh][:rows]
        o = o * lax.rsqrt(jnp.mean(o * o, axis=-1, keepdims=True) + EPS) * on_ref[...]
        o_ref[:, sl] = (o * _silu(gate[:, sl])).astype(o_ref.dtype)

    @pl.when(c == pl.num_programs(2) - 1)
    def _():
        sn_ref[...] = s_ref[...]


def delta_heads(z, row0, nb, t, conv8, s0, conv_w, alog_row, dt_row, out_norm, *, n_heads, dk, cols, hb):
    cq, ck, cv, cg, cba = cols
    taps = conv_w.shape[0]
    chunk = DELTA_CHUNK
    assert dk == chunk == LANE and t % SUBLANE == 0 and n_heads % hb == 0
    if t >= chunk:
        assert t % chunk == 0
        seg, nseg = chunk, 1
    else:
        assert chunk % t == 0
        seg, nseg = t, _tile(nb, chunk // t, 1)
    rows = seg * nseg
    nc = t // seg
    assert row0 % rows == 0 and all(cb % hb == 0 for cb in (cq, ck, cv, cg))
    r0 = row0 // rows
    wid = hb * dk

    def zspec(cb):
        return pl.BlockSpec((rows, wid), lambda b, h, c: (r0 + b * nc + c, cb // hb + h))

    def cspec(cb):
        return pl.BlockSpec((nseg, SUBLANE, wid), lambda b, h, c: (b, 0, cb // hb + h))

    def wspec(cb):
        return pl.BlockSpec((taps, wid), lambda b, h, c: (0, cb // hb + h))

    row = pl.BlockSpec((1, LANE), lambda b, h, c: (0, 0))
    sspec = pl.BlockSpec((nseg, hb, dk, dk), lambda b, h, c: (b, h, 0, 0))
    kern = functools.partial(_delta_kernel, seg=seg, nseg=nseg, hb=hb, dk=dk, chunk=chunk, taps=taps,
                             n_heads=n_heads)
    return pl.pallas_call(
        kern,
        grid=(nb // nseg, n_heads // hb, nc),
        in_specs=[zspec(cq), zspec(ck), zspec(cv), zspec(cg),
                  pl.BlockSpec((rows, LANE), lambda b, h, c: (r0 + b * nc + c, cba)),
                  cspec(cq), cspec(ck), cspec(cv), wspec(cq), wspec(ck), wspec(cv),
                  sspec, row, row, row],
        out_specs=[pl.BlockSpec((rows, wid), lambda b, h, c: (b * nc + c, h)), sspec],
        out_shape=[jax.ShapeDtypeStruct((nb * t, n_heads * dk), BF16),
                   jax.ShapeDtypeStruct((nb, n_heads, dk, dk), F32)],
        scratch_shapes=[pltpu.VMEM((nseg, SUBLANE + seg, wid), F32)] * 3 + [pltpu.VMEM((nseg, hb, dk, dk), F32)],
        name="delta_heads",
    )(z, z, z, z, z, conv8, conv8, conv8, conv_w, conv_w, conv_w, s0, alog_row, dt_row, out_norm)


def _pool_kernel(u_ref, st_ref, w_ref, sc_ref, o_ref, xp_ref, *, tt, pos0, halo, gw):
    tb = pl.program_id(1)

    @pl.when(tb == 0)
    def _():
        xp_ref[0:halo, :] = st_ref[0]

    u = u_ref[...]
    xp_ref[halo:halo + tt, :] = u
    pos = pos0 + tb * tt + lax.broadcasted_iota(jnp.int32, (tt, 1), 0)
    for gi, wsz in enumerate(POOL_WINDOWS):
        lo, hi = gi * gw, (gi + 1) * gw
        acc = u[:, lo:hi]
        for i in range(1, wsz):
            acc = acc + xp_ref[halo - i:halo - i + tt, lo:hi]
        cnt = jnp.minimum(wsz, pos + 1).astype(F32)
        d = acc * (1.0 / cnt) - u[:, lo:hi]
        y = _dot(d, w_ref[gi]) * sc_ref[:, lo:hi]
        o_ref[:, lo:hi] = y.astype(o_ref.dtype)
    xp_ref[0:halo, :] = xp_ref[tt:tt + halo, :]


def pool_groups(z, row0, nb, t, state16, pool_w, pool_scale, *, col_block, pos0, tt_target=256):
    db = pool_scale.shape[-1]
    halo = state16.shape[1]
    tt = _tile(t, tt_target, SUBLANE)
    assert row0 % tt == 0
    nt, r0 = t // tt, row0 // tt
    gw = db // len(POOL_WINDOWS)
    kern = functools.partial(_pool_kernel, tt=tt, pos0=pos0, halo=halo, gw=gw)
    return pl.pallas_call(
        kern,
        grid=(nb, nt),
        in_specs=[pl.BlockSpec((tt, db), lambda b, i: (r0 + b * nt + i, col_block)),
                  pl.BlockSpec((1, halo, db), lambda b, i: (b, 0, 0)),
                  pl.BlockSpec(pool_w.shape, lambda b, i: (0, 0, 0)),
                  pl.BlockSpec((1, db), lambda b, i: (0, 0))],
        out_specs=pl.BlockSpec((tt, db), lambda b, i: (b * nt + i, 0)),
        out_shape=jax.ShapeDtypeStruct((nb * t, db), BF16),
        scratch_shapes=[pltpu.VMEM((halo + tt, db), F32)],
        name="pool_groups",
    )(z, state16, pool_w, pool_scale.reshape(1, db))


def _rglru_kernel(xc_ref, gc_ref, c8_ref, cw_ref, cb_ref, wa_ref, wx_ref, ba_ref, bx_ref, lam_ref, h0_ref,
                  y_ref, hl_ref, xp_ref, h_ref, *, taps, n_blocks, bs):
    tb = pl.program_id(1)
    first = tb == 0

    @pl.when(first)
    def _():
        h_ref[...] = h0_ref[0]

    x = _conv_rows(xp_ref, xc_ref[...], cw_ref[...], first, c8_ref[0], taps) + cb_ref[...]
    tt = x.shape[0]
    rs, is_ = [], []
    for n in range(n_blocks):
        xb = x[:, n * bs:(n + 1) * bs]
        rs.append(_dot(xb, wa_ref[n]))
        is_.append(_dot(xb, wx_ref[n]))
    r = _sigmoid(jnp.concatenate(rs, axis=1) + ba_ref[...])
    i = _sigmoid(jnp.concatenate(is_, axis=1) + bx_ref[...])
    log_a = -LRU_C * r * _softplus(-lam_ref[...])
    a = jnp.exp(log_a)
    b = jnp.sqrt(-_expm1(2.0 * log_a)) * (i * x)
    row = lax.broadcasted_iota(jnp.int32, a.shape, 0)
    s = 1
    while s < tt:
        keep = row >= s
        b = jnp.where(keep, a * pltpu.roll(b, s, 0) + b, b)
        a = jnp.where(keep, a * pltpu.roll(a, s, 0), a)
        s *= 2
    hs = b + a * h_ref[...]
    h_last = hs[tt - 1:tt, :]
    h_ref[...] = h_last
    y_ref[...] = (hs * _gelu_tanh(gc_ref[...])).astype(y_ref.dtype)

    @pl.when(tb == pl.num_programs(1) - 1)
    def _():
        hl_ref[0] = h_last


def rglru_blocks(z, row0, nb, t, conv8, h0, conv_w, conv_b, wa, ba, wx, bx, lam, *, tt_target=256):
    dc = lam.shape[-1]
    taps = conv_w.shape[0]
    n_blocks, bs = wa.shape[0], wa.shape[1]
    tt = _tile(t, tt_target, SUBLANE)
    assert row0 % tt == 0
    nt, r0 = t // tt, row0 // tt
    vec = pl.BlockSpec((1, dc), lambda b, i: (0, 0))
    kern = functools.partial(_rglru_kernel, taps=taps, n_blocks=n_blocks, bs=bs)
    return pl.pallas_call(
        kern,
        grid=(nb, nt),
        in_specs=[pl.BlockSpec((tt, dc), lambda b, i: (r0 + b * nt + i, 0)),
                  pl.BlockSpec((tt, dc), lambda b, i: (r0 + b * nt + i, 1)),
                  pl.BlockSpec((1, SUBLANE, dc), lambda b, i: (b, 0, 0)),
                  pl.BlockSpec((taps, dc), lambda b, i: (0, 0)),
                  vec,
                  pl.BlockSpec(wa.shape, lambda b, i: (0, 0, 0)),
                  pl.BlockSpec(wx.shape, lambda b, i: (0, 0, 0)),
                  vec, vec, vec,
                  pl.BlockSpec((1, 1, dc), lambda b, i: (b, 0, 0))],
        out_specs=[pl.BlockSpec((tt, dc), lambda b, i: (b * nt + i, 0)),
                   pl.BlockSpec((1, 1, dc), lambda b, i: (b, 0, 0))],
        out_shape=[jax.ShapeDtypeStruct((nb * t, dc), BF16),
                   jax.ShapeDtypeStruct((nb, 1, dc), F32)],
        scratch_shapes=[pltpu.VMEM((SUBLANE + tt, dc), F32), pltpu.VMEM((1, dc), F32)],
        name="rglru_blocks",
    )(z, z, conv8, conv_w, conv_b.reshape(1, dc), wa, wx, ba.reshape(1, dc), bx.reshape(1, dc),
      lam.reshape(1, dc), h0.reshape(nb, 1, dc))


def _fox_prep_kernel(f_ref, bf_ref, lf_ref, cum_ref, cumt_ref, *, n_heads):
    lf = -_softplus(-(f_ref[...] + bf_ref[...]))
    lf_ref[...] = lf
    cum = _cumsum_rows(lf)
    cum_ref[...] = cum
    cumt_ref[0] = cum.T[0:n_heads, :]


def fox_prep(z, nb, t, bf_row, *, col_block, n_heads):
    return pl.pallas_call(
        functools.partial(_fox_prep_kernel, n_heads=n_heads),
        grid=(nb,),
        in_specs=[pl.BlockSpec((t, LANE), lambda b: (b, col_block)),
                  pl.BlockSpec((1, LANE), lambda b: (0, 0))],
        out_specs=[pl.BlockSpec((t, LANE), lambda b: (b, 0)),
                   pl.BlockSpec((t, LANE), lambda b: (b, 0)),
                   pl.BlockSpec((1, n_heads, t), lambda b: (b, 0, 0))],
        out_shape=[jax.ShapeDtypeStruct((nb * t, LANE), F32),
                   jax.ShapeDtypeStruct((nb * t, LANE), F32),
                   jax.ShapeDtypeStruct((nb, n_heads, t), F32)],
        name="fox_prep",
    )(z, bf_row)


def _fox_flash_kernel(qi_ref, ki_ref, q_ref, k_ref, v_ref, fq_ref, fk_ref, o_ref, m_ref, l_ref, acc_ref, fqc_ref,
                      *, scale, hb, hd):
    hblk = pl.program_id(1)
    pair = pl.program_id(2)
    qi, ki = qi_ref[pair], ki_ref[pair]
    tq, tk = q_ref.shape[0], k_ref.shape[0]
    log2e = math.log2(math.e)

    @pl.when(ki == 0)
    def _():
        m_ref[...] = jnp.full(m_ref.shape, NEG, F32)
        l_ref[...] = jnp.zeros(l_ref.shape, F32)
        acc_ref[...] = jnp.zeros(acc_ref.shape, F32)
        fq = fq_ref[...] * log2e
        lane = lax.broadcasted_iota(jnp.int32, fq.shape, 1)
        for hh in range(hb):
            col = jnp.sum(jnp.where(lane == hblk * hb + hh, fq, 0.0), axis=1, keepdims=True)
            fqc_ref[hh] = jnp.broadcast_to(col, fqc_ref.shape[1:])

    def attend(diagonal):
        heads = range(hb)
        kb = k_ref[...].astype(BF16)
        vb = v_ref[...].astype(BF16)
        fks = [fk_ref[0, pl.ds(hblk * hb + hh, 1), :] * log2e for hh in heads]
        band = FLASH_ROW_BAND
        chains = [(r0, hh) for r0 in range(0, tq, band) for hh in heads]
        for c0 in range(0, len(chains), FLASH_CHAINS):
            group = chains[c0:c0 + FLASH_CHAINS]
            ts, nks = [], []
            for r0, hh in group:
                sl = slice(hh * hd, (hh + 1) * hd)
                nk = min(tk, -(-(r0 + band) // LANE) * LANE) if diagonal else tk
                qb = q_ref[r0:r0 + band, sl].astype(BF16)
                t2 = _dot_nt(qb, kb[:nk, sl]) * (scale * log2e) - fks[hh][:, :nk]
                if diagonal:
                    ri = lax.broadcasted_iota(jnp.int32, (band, nk), 0) + r0
                    ci = lax.broadcasted_iota(jnp.int32, (band, nk), 1)
                    t2 = jnp.where(ci <= ri, t2, NEG)
                ts.append(t2)
                nks.append(nk)
            alphas, ps, m_news = [], [], []
            for (r0, hh), t2, nk in zip(group, ts, nks):
                rs = slice(r0, r0 + band)
                fq = fqc_ref[hh, rs, :]
                m_old = m_ref[hh, rs, :]
                m_new = jnp.maximum(m_old, jnp.max(t2, axis=1, keepdims=True) + fq)
                shift = fq - m_new
                alphas.append(jnp.exp2(m_old - m_new))
                ps.append([jnp.exp2(t2[:, j:j + LANE] + shift) for j in range(0, nk, LANE)])
                m_news.append(m_new)
            for (r0, hh), p, alpha, m_new, nk in zip(group, ps, alphas, m_news, nks):
                rs = slice(r0, r0 + band)
                sl = slice(hh * hd, (hh + 1) * hd)
                psum = p[0]
                for pj in p[1:]:
                    psum = psum + pj
                l_ref[hh, rs, :] = alpha * l_ref[hh, rs, :] + jnp.sum(psum, axis=1, keepdims=True)
                pb = jnp.concatenate([pj.astype(BF16) for pj in p], axis=1) if len(p) > 1 else p[0].astype(BF16)
                acc_ref[rs, sl] = alpha * acc_ref[rs, sl] + jnp.dot(pb, vb[:nk, sl], preferred_element_type=F32)
                m_ref[hh, rs, :] = m_new

    @pl.when(ki < qi)
    def _():
        attend(False)

    @pl.when(ki == qi)
    def _():
        attend(True)
        for hh in range(hb):
            sl = slice(hh * hd, (hh + 1) * hd)
            o_ref[:, sl] = (acc_ref[:, sl] / l_ref[hh]).astype(o_ref.dtype)


def fox_prompt(z, cum, cumt, nb, t, *, n_heads, hd, cols, tq_target=512, hb=2):
    cq, ck, cv = cols
    tq = _tile(t, tq_target, LANE)
    nq = t // tq
    assert n_heads % hb == 0 and all(cb % hb == 0 for cb in cols)
    pairs = [(i, j) for i in range(nq) for j in range(i + 1)]
    qi_tab = jnp.asarray([p[0] for p in pairs], jnp.int32)
    ki_tab = jnp.asarray([p[1] for p in pairs], jnp.int32)
    wid = hb * hd
    kern = functools.partial(_fox_flash_kernel, scale=hd ** -0.5, hb=hb, hd=hd)

    def kv(cb):
        return pl.BlockSpec((tq, wid), lambda b, h, p, qt, kt: (b * nq + kt[p], cb // hb + h))

    grid_spec = pltpu.PrefetchScalarGridSpec(
        num_scalar_prefetch=2,
        grid=(nb, n_heads // hb, len(pairs)),
        in_specs=[pl.BlockSpec((tq, wid), lambda b, h, p, qt, kt: (b * nq + qt[p], cq // hb + h)),
                  kv(ck), kv(cv),
                  pl.BlockSpec((tq, LANE), lambda b, h, p, qt, kt: (b * nq + qt[p], 0)),
                  pl.BlockSpec((1, n_heads, tq), lambda b, h, p, qt, kt: (b, 0, kt[p]))],
        out_specs=pl.BlockSpec((tq, wid), lambda b, h, p, qt, kt: (b * nq + qt[p], h)),
        scratch_shapes=[pltpu.VMEM((hb, tq, LANE), F32), pltpu.VMEM((hb, tq, LANE), F32), pltpu.VMEM((tq, wid), F32),
                        pltpu.VMEM((hb, tq, LANE), F32)],
    )
    assert hd == LANE
    return pl.pallas_call(
        kern,
        grid_spec=grid_spec,
        out_shape=jax.ShapeDtypeStruct((nb * t, n_heads * hd), BF16),
        name="fox_prompt",
    )(qi_tab, ki_tab, z, z, z, cum, cumt)


def _fox_sample_kernel(pt_ref, q_ref, kn_ref, vn_ref, f_ref, bf_ref, *rest, n_heads, hd, pages_per_step, scale):
    g_pages = pages_per_step
    k_refs = rest[0:g_pages]
    v_refs = rest[g_pages:2 * g_pages]
    lf_refs = rest[2 * g_pages:3 * g_pages]
    o_ref, lfo_ref, m_ref, l_ref, acc_ref, carry_ref, fnc_ref = rest[3 * g_pages:]
    step = pl.program_id(1)
    t = q_ref.shape[0]
    page = lf_refs[0].shape[3]
    q = q_ref[...].astype(BF16)

    def update(h, s_list, v_list):
        m_old = m_ref[h]
        smax = s_list[0]
        for s in s_list[1:]:
            smax = jnp.maximum(smax, s)
        m_new = jnp.maximum(m_old, jnp.max(smax, axis=1, keepdims=True))
        alpha = jnp.exp(m_old - m_new)
        psum, pv = None, None
        for s, vh in zip(s_list, v_list):
            p = jnp.exp(s - m_new)
            d = _dot(p, vh)
            psum = p if psum is None else psum + p
            pv = d if pv is None else pv + d
        sl = slice(h * hd, (h + 1) * hd)
        l_ref[h] = alpha * l_ref[h] + jnp.sum(psum, axis=1, keepdims=True)
        acc_ref[:, sl] = alpha * acc_ref[:, sl] + pv
        m_ref[h] = m_new

    @pl.when(step == 0)
    def _():
        lf = -_softplus(-(f_ref[...] + bf_ref[...]))
        lfo_ref[...] = lf
        fn = _cumsum_rows(lf)
        carry_ref[...] = jnp.zeros(carry_ref.shape, F32)
        m_ref[...] = jnp.full(m_ref.shape, NEG, F32)
        l_ref[...] = jnp.zeros(l_ref.shape, F32)
        acc_ref[...] = jnp.zeros(acc_ref.shape, F32)
        kn = jnp.concatenate([kn_ref[...], jnp.zeros((page - t, n_heads * hd), F32)], axis=0)
        vn = jnp.concatenate([vn_ref[...], jnp.zeros((page - t, n_heads * hd), F32)], axis=0)
        ri = lax.broadcasted_iota(jnp.int32, (t, page), 0)
        ci = lax.broadcasted_iota(jnp.int32, (t, page), 1)
        for h in range(n_heads):
            col = lf[:, h:h + 1]
            fnrow = jnp.sum(jnp.where(ri <= ci, col, 0.0), axis=0, keepdims=True)
            fnc = jnp.broadcast_to(fn[:, h:h + 1], (t, page))
            fnc_ref[h] = fnc
            sl = slice(h * hd, (h + 1) * hd)
            s = _dot_nt(q[:, sl], kn[:, sl]) * scale + fnc - fnrow
            update(h, [jnp.where(ci <= ri, s, NEG)], [vn[:, sl]])

    lft = jnp.concatenate([lf_refs[g][0, 0] for g in range(g_pages)], axis=0)
    ri = lax.broadcasted_iota(jnp.int32, (page, page), 0)
    ci = lax.broadcasted_iota(jnp.int32, (page, page), 1)
    rr = _dot_mask(lft, (ri > ci).astype(BF16))
    tot = jnp.sum(lft, axis=1, keepdims=True)
    run = carry_ref[:, 0:1]
    rc = [None] * g_pages
    for g in reversed(range(g_pages)):
        rc[g] = rr[g * n_heads:(g + 1) * n_heads] + run
        run = run + tot[g * n_heads:(g + 1) * n_heads]
    carry_ref[...] = jnp.broadcast_to(run, carry_ref.shape)

    def head_tile(ref, h):
        return ref[0, 0, pl.ds(h, page, stride=n_heads), :]

    scores = []
    for h in range(n_heads):
        qh = q[:, h * hd:(h + 1) * hd]
        fnc = fnc_ref[h]
        scores.append([_dot_nt(qh, head_tile(k_refs[g], h)) * scale + fnc + rc[g][h:h + 1, :]
                       for g in range(g_pages)])
    m_news, alphas, probs = [], [], []
    for h in range(n_heads):
        smax = scores[h][0]
        for s in scores[h][1:]:
            smax = jnp.maximum(smax, s)
        m_old = m_ref[h]
        m_new = jnp.maximum(m_old, jnp.max(smax, axis=1, keepdims=True))
        m_news.append(m_new)
        alphas.append(jnp.exp(m_old - m_new))
        probs.append([jnp.exp(s - m_new) for s in scores[h]])
    for h in range(n_heads):
        sl = slice(h * hd, (h + 1) * hd)
        psum, pv = None, None
        for g in range(g_pages):
            p = probs[h][g]
            d = _dot(p, head_tile(v_refs[g], h))
            psum = p if psum is None else psum + p
            pv = d if pv is None else pv + d
        l_ref[h] = alphas[h] * l_ref[h] + jnp.sum(psum, axis=1, keepdims=True)
        acc_ref[:, sl] = alphas[h] * acc_ref[:, sl] + pv
        m_ref[h] = m_news[h]

    @pl.when(step == pl.num_programs(1) - 1)
    def _():
        for h in range(n_heads):
            sl = slice(h * hd, (h + 1) * hd)
            o_ref[:, sl] = (acc_ref[:, sl] / l_ref[h]).astype(o_ref.dtype)


def fox_sample(z, row0, nb, t, page_table, kc, vc, lfc, layer, bf_row, *, n_heads, hd, cols, pages_per_step=16):
    cq, ck, cv, cf = cols
    n_pages = page_table.shape[1]
    page = lfc.shape[3]
    g_pages = pages_per_step
    while n_pages % g_pages:
        g_pages //= 2
    ns = n_pages // g_pages
    dd = n_heads * hd
    assert row0 % t == 0 and t % SUBLANE == 0
    r0 = row0 // t

    def zspec(cb, w):
        return pl.BlockSpec((t, w), lambda b, s, pt: (r0 + b, cb))

    def pspec(g, r, w):
        return pl.BlockSpec((1, 1, r, w), lambda b, s, pt, g=g: (layer, pt[b, (ns - 1 - s) * g_pages + g], 0, 0))

    kern = functools.partial(_fox_sample_kernel, n_heads=n_heads, hd=hd, pages_per_step=g_pages, scale=hd ** -0.5)
    grid_spec = pltpu.PrefetchScalarGridSpec(
        num_scalar_prefetch=1,
        grid=(nb, ns),
        in_specs=[zspec(cq, dd), zspec(ck, dd), zspec(cv, dd), zspec(cf, LANE),
                  pl.BlockSpec((1, LANE), lambda b, s, pt: (0, 0))]
        + [pspec(g, page * n_heads, hd) for g in range(g_pages)] * 2
        + [pspec(g, n_heads, page) for g in range(g_pages)],
        out_specs=[pl.BlockSpec((t, dd), lambda b, s, pt: (b, 0)),
                   pl.BlockSpec((t, LANE), lambda b, s, pt: (b, 0))],
        scratch_shapes=[pltpu.VMEM((n_heads, t, 1), F32), pltpu.VMEM((n_heads, t, 1), F32),
                        pltpu.VMEM((t, dd), F32), pltpu.VMEM((n_heads, LANE), F32),
                        pltpu.VMEM((n_heads, t, page), F32)],
    )
    return pl.pallas_call(
        kern,
        grid_spec=grid_spec,
        out_shape=[jax.ShapeDtypeStruct((nb * t, dd), BF16), jax.ShapeDtypeStruct((nb * t, LANE), F32)],
        name="fox_sample",
    )(page_table, z, z, z, z, bf_row, *([kc] * g_pages), *([vc] * g_pages), *([lfc] * g_pages))


def _xattn_kernel(q_ref, k_ref, v_ref, o_ref, *, n_heads, hd, interleaved, scale):
    q = q_ref[...]
    for h in range(n_heads):
        sl = slice(h * hd, (h + 1) * hd)
        if interleaved:
            n_mem = k_ref.shape[2] // n_heads
            kh = k_ref[0, 0, pl.ds(h, n_mem, stride=n_heads), :]
            vh = v_ref[0, 0, pl.ds(h, n_mem, stride=n_heads), :]
        else:
            kh, vh = k_ref[:, sl], v_ref[:, sl]
        s = _dot_nt(q[:, sl], kh) * scale
        p = jnp.exp(s - jnp.max(s, axis=1, keepdims=True))
        o = _dot(p, vh) / jnp.sum(p, axis=1, keepdims=True)
        o_ref[:, sl] = o.astype(o_ref.dtype)


def xattn(qx, row0, nb, t, mk, mv, kspec, vspec, *, n_heads, hd, interleaved, tt_target=512):
    dx = n_heads * hd
    tt = _tile(t, tt_target, SUBLANE)
    assert row0 % tt == 0
    nt, r0 = t // tt, row0 // tt
    kern = functools.partial(_xattn_kernel, n_heads=n_heads, hd=hd, interleaved=interleaved, scale=hd ** -0.5)
    return pl.pallas_call(
        kern,
        grid=(nb, nt),
        in_specs=[pl.BlockSpec((tt, dx), lambda b, i: (r0 + b * nt + i, 0)), kspec, vspec],
        out_specs=pl.BlockSpec((tt, dx), lambda b, i: (b * nt + i, 0)),
        out_shape=jax.ShapeDtypeStruct((nb * t, dx), BF16),
        name="xattn",
    )(qx, mk, mv)


def _pad_cols(w, n):
    return jnp.pad(w, ((0, 0), (0, n - w.shape[1])))


def _halo_rows(buf, rows):
    return jnp.pad(buf, ((0, 0), (rows - buf.shape[1], 0), (0, 0)))


def kernel(x_prompt, x_sample, state_delta, state_conv_a, state_pool, state_lru, state_conv_c, cache_fox_k, cache_fox_v, cache_fox_logf, cache_mem_k, cache_mem_v, page_table, mem_prompt, ffn1_wg, ffn1_wu, ffn1_wd, ffn2_wg, ffn2_wu, ffn2_wd, norm_ffn1, norm_mix, norm_xattn, norm_ffn2, norm_mem, final_norm, xattn_wq, xattn_wk, xattn_wv, xattn_wo, even_w_in, even_w_out, delta_conv_w, delta_a_log, delta_dt_bias, delta_out_norm, pool_w, pool_scale, odd_w_in, odd_w_out, lru_conv_w, lru_conv_b, lru_wa, lru_ba, lru_wx, lru_bx, lru_lambda, fox_bf):
    bp, tp, d = x_prompt.shape
    bs, ts, _ = x_sample.shape
    mp, ms = bp * tp, bs * ts
    depth = ffn1_wg.shape[0]
    h_a, dk_a = state_delta.shape[2], state_delta.shape[3]
    d_a = h_a * dk_a
    d_b = state_pool.shape[-1]
    d_c = state_lru.shape[-1]
    h_d, hd_d = cache_fox_k.shape[3], cache_fox_k.shape[4]
    d_d = h_d * hd_d
    n_mem, h_x, hd_x = cache_mem_k.shape[2], cache_mem_k.shape[3], cache_mem_k.shape[4]
    d_x = h_x * hd_x
    page = cache_fox_k.shape[2]
    pos0_s = page_table.shape[1] * page
    assert d_a == d_b and d_c == d_d and 2 * h_a <= LANE and h_d <= LANE
    tn_z = 768
    nz = -(-(max(4 * d_a + d_b, 2 * d_c + 3 * d_d) + LANE) // tn_z) * tn_z
    bc =(4 * d_a + d_b) // LANE
    assert bc == (2 * d_c + 3 * d_d) // LANE

    x = jnp.concatenate([x_prompt.reshape(mp, d), x_sample.reshape(ms, d)], axis=0)

    mem2 = mem_prompt.reshape(bp * n_mem, d)
    mem_k = [norm_mm(mem2, norm_mem[l], xattn_wk, l, tn=d_x) for l in range(depth)]
    mem_v = [norm_mm(mem2, norm_mem[l], xattn_wv, l, tn=d_x) for l in range(depth)]
    kc4 = cache_fox_k.reshape(cache_fox_k.shape[0], cache_fox_k.shape[1], page * h_d, hd_d)
    vc4 = cache_fox_v.reshape(kc4.shape)
    lfc4 = jnp.swapaxes(cache_fox_logf, 2, 3)
    mk4 = cache_mem_k.reshape(depth, bs, n_mem * h_x, hd_x)
    mv4 = cache_mem_v.reshape(depth, bs, n_mem * h_x, hd_x)

    new_p = {n: [] for n in ('delta', 'conv_a', 'pool', 'lru', 'conv_c', 'fox_k', 'fox_v', 'fox_logf')}
    new_s = {n: [] for n in new_p}

    def seq_cols(zz, row0, nb, t, c0, c1, last=None):
        if last is None or last >= t:
            return lax.slice(zz, (row0, c0), (row0 + nb * t, c1)).reshape(nb, t, c1 - c0)
        idx = (row0 + jnp.arange(nb)[:, None] * t + (t - last) + jnp.arange(last)[None, :]).reshape(-1)
        return lax.slice_in_dim(zz[idx], c0, c1, axis=1).reshape(nb, last, c1 - c0)

    def new_buf(buf, zz, row0, nb, t, c0, c1):
        r = buf.shape[1]
        if t >= r:
            return seq_cols(zz, row0, nb, t, c0, c1, last=r)
        return jnp.concatenate([buf[:, t:], seq_cols(zz, row0, nb, t, c0, c1)], axis=1)

    for l in range(depth):
        j = l // 2
        a = ffn_up(x, norm_ffn1[l], ffn1_wg, ffn1_wu, l)
        x = mm_res([a], ffn1_wd, l, x, tm_target=1056, tn=FFN_DOWN_TN)

        if l % 2 == 0:
            w = even_w_in[j]
            w_in = _pad_cols(jnp.concatenate([w[:, :4 * d_a], w[:, 4 * d_a + 2 * h_a:], w[:, 4 * d_a:4 * d_a + 2 * h_a]],
                                             axis=1), nz).astype(BF16)
            z = norm_mm(x, norm_mix[l], w_in)
            hb = dk_a // LANE
            alog_row = jnp.zeros((1, LANE), F32).at[0, h_a:2 * h_a].set(delta_a_log[j])
            dt_row = jnp.zeros((1, LANE), F32).at[0, h_a:2 * h_a].set(delta_dt_bias[j])
            groups = ((0, bp, tp, jnp.zeros((bp,) + state_conv_a.shape[2:], F32),
                       jnp.zeros((bp,) + state_delta.shape[2:], F32), jnp.zeros((bp,) + state_pool.shape[2:], F32), 0, new_p),
                      (mp, bs, ts, state_conv_a[j], state_delta[j], state_pool[j], pos0_s, new_s))
            mix_a, mix_b = [], []
            for gi, (row0, nb, t, cbuf, s0, pbuf, pos0, new) in enumerate(groups):
                o, s_new = delta_heads(z, row0, nb, t, _halo_rows(cbuf, SUBLANE), s0, delta_conv_w[j], alog_row, dt_row,
                                       delta_out_norm[j].reshape(1, dk_a), n_heads=h_a, dk=dk_a,
                                       cols=(0, h_a * hb, 2 * h_a * hb, 3 * h_a * hb, bc),
                                       hb=DELTA_HEADS_PER_STEP if t >= DELTA_CHUNK else 1)
                yb = pool_groups(z, row0, nb, t, _halo_rows(pbuf, 2 * SUBLANE), pool_w[j].astype(BF16), pool_scale[j],
                                 col_block=4 * d_a // d_b, pos0=pos0)
                mix_a.append(o)
                mix_b.append(yb)
                new['delta'].append(s_new)
                new['conv_a'].append(new_buf(cbuf, z, row0, nb, t, 0, 3 * d_a))
                new['pool'].append(new_buf(pbuf, z, row0, nb, t, 4 * d_a, 4 * d_a + d_b))
            w_out = even_w_out
        else:
            w_in = _pad_cols(odd_w_in[j], nz).astype(BF16)
            z = norm_mm(x, norm_mix[l], w_in)
            bf_row = jnp.zeros((1, LANE), F32).at[0, :h_d].set(fox_bf[j])
            qb, kb, vb = 2 * d_c // hd_d, (2 * d_c + d_d) // hd_d, (2 * d_c + 2 * d_d) // hd_d
            groups = ((0, bp, tp, jnp.zeros((bp,) + state_conv_c.shape[2:], F32), jnp.zeros((bp, d_c), F32), new_p),
                      (mp, bs, ts, state_conv_c[j], state_lru[j], new_s))
            mix_a, mix_b = [], []
            for gi, (row0, nb, t, cbuf, h0, new) in enumerate(groups):
                yc, h_last = rglru_blocks(z, row0, nb, t, _halo_rows(cbuf, SUBLANE), h0, lru_conv_w[j], lru_conv_b[j],
                                          lru_wa[j].astype(BF16), lru_ba[j], lru_wx[j].astype(BF16), lru_bx[j],
                                          lru_lambda[j])
                if gi == 0:
                    lf, cum, cumt = fox_prep(z, nb, t, bf_row, col_block=bc, n_heads=h_d)
                    o = fox_prompt(z, cum, cumt, nb, t, n_heads=h_d, hd=hd_d, cols=(qb, kb, vb))
                else:
                    o, lf = fox_sample(z, row0, nb, t, page_table, kc4, vc4, lfc4, j, bf_row,
                                       n_heads=h_d, hd=hd_d,
                                       cols=(2 * d_c // d_d, (2 * d_c + d_d) // d_d, (2 * d_c + 2 * d_d) // d_d, bc))
                mix_a.append(yc)
                mix_b.append(o)
                new['lru'].append(h_last.reshape(nb, d_c))
                new['conv_c'].append(new_buf(cbuf, z, row0, nb, t, 0, d_c))
                new['fox_k'].append(seq_cols(z, row0, nb, t, 2 * d_c + d_d, 2 * d_c + 2 * d_d).reshape(nb, t, h_d, hd_d))
                new['fox_v'].append(seq_cols(z, row0, nb, t, 2 * d_c + 2 * d_d, 2 * d_c + 3 * d_d).reshape(nb, t, h_d, hd_d))
                new['fox_logf'].append(lf[:, :h_d].reshape(nb, t, h_d))
            w_out = odd_w_out

        x = mm_res([jnp.concatenate(mix_a, axis=0), jnp.concatenate(mix_b, axis=0)], w_out, j, x)

        qx = norm_mm(x, norm_xattn[l], xattn_wq, l, tn=d_x)
        o_p = xattn(qx, 0, bp, tp, mem_k[l], mem_v[l],
                    pl.BlockSpec((n_mem, d_x), lambda b, i: (b, 0)), pl.BlockSpec((n_mem, d_x), lambda b, i: (b, 0)),
                    n_heads=h_x, hd=hd_x, interleaved=False)
        o_s = xattn(qx, mp, bs, ts, mk4, mv4,
                    pl.BlockSpec((1, 1, n_mem * h_x, hd_x), lambda b, i, l=l: (l, b, 0, 0)),
                    pl.BlockSpec((1, 1, n_mem * h_x, hd_x), lambda b, i, l=l: (l, b, 0, 0)),
                    n_heads=h_x, hd=hd_x, interleaved=True)
        x = mm_res([jnp.concatenate([o_p, o_s], axis=0)], xattn_wo, l, x)

        a = ffn_up(x, norm_ffn2[l], ffn2_wg, ffn2_wu, l)
        x = mm_res([a], ffn2_wd, l, x, tm_target=1056, tn=FFN_DOWN_TN)

    y_p = rms_rows(x, final_norm, 0, mp).reshape(bp, tp, d)
    y_s = rms_rows(x, final_norm, mp, ms).reshape(bs, ts, d)
    mem_k_p = jnp.stack([m.reshape(bp, n_mem, h_x, hd_x) for m in mem_k])
    mem_v_p = jnp.stack([m.reshape(bp, n_mem, h_x, hd_x) for m in mem_v])
    names = ('delta', 'conv_a', 'pool', 'lru', 'conv_c', 'fox_k', 'fox_v', 'fox_logf')
    return ((y_p, y_s) + tuple(jnp.stack(new_p[n]) for n in names) + (mem_k_p, mem_v_p)
            + tuple(jnp.stack(new_s[n]) for n in names))
```

```python
import functools
import math

import jax
import jax.numpy as jnp
from jax import lax
from jax.experimental import pallas as pl
from jax.experimental.pallas import tpu as pltpu

F32 = jnp.float32
BF16 = jnp.bfloat16
EPS = 1e-6
LRU_C = 8.0
POOL_WINDOWS = (2, 4, 8, 16)
LANE = 128
SUBLANE = 8
DELTA_CHUNK = 128
FFN_DOWN_TN = 256
DELTA_HEADS_PER_STEP = 8
INV_BASE = 16
FLASH_ROW_BAND = 128
SAMPLE_PAGE_GROUP = 16
FLASH_CHAINS = 2
NEG = -1e30


def _tile(n, target, align):
    best = None
    for d in range(align, min(n, target) + 1, align):
        if n % d == 0:
            best = d
    return n if best is None else best


def _dot(a, b, hi=False):
    if hi:
        return jnp.dot(a.astype(F32), b.astype(F32), precision=lax.Precision.HIGHEST, preferred_element_type=F32)
    return jnp.dot(a.astype(BF16), b.astype(BF16), preferred_element_type=F32)


def _dot_nt(a, b):
    return lax.dot_general(a.astype(BF16), b.astype(BF16), (((1,), (1,)), ((), ())), preferred_element_type=F32)


def _sigmoid(x):
    return 1.0 / (1.0 + jnp.exp(-x))


def _silu(x):
    return x * _sigmoid(x)


def _softplus(x):
    return jnp.maximum(x, 0.0) + jnp.log1p(jnp.exp(-jnp.abs(x)))


def _expm1(x):
    u = jnp.exp(x)
    degenerate = (u == 1.0) | (u == 0.0)
    safe = jnp.where(degenerate, 2.0, u)
    return jnp.where(u == 1.0, x, jnp.where(u == 0.0, -1.0, (u - 1.0) * x / jnp.log(safe)))


def _gelu_tanh(x):
    return 0.5 * x * (1.0 + jnp.tanh(math.sqrt(2.0 / math.pi) * (x + 0.044715 * (x * x * x))))


def _cumsum_rows(x):
    n = x.shape[0]
    row = lax.broadcasted_iota(jnp.int32, x.shape, 0)
    s = 1
    while s < n:
        x = x + jnp.where(row >= s, pltpu.roll(x, s, 0), 0.0)
        s *= 2
    return x


def _rms_rows(x, g):
    ms = jnp.mean(x * x, axis=-1, keepdims=True)
    return x * lax.rsqrt(ms + EPS) * g


def _norm_mm_kernel(x_ref, g_ref, w_ref, o_ref, hn_ref):
    @pl.when(pl.program_id(1) == 0)
    def _():
        hn_ref[...] = _rms_rows(x_ref[...], g_ref[...]).astype(BF16)

    o_ref[...] = jnp.dot(hn_ref[...], w_ref[0].astype(BF16), preferred_element_type=F32).astype(o_ref.dtype)


def norm_mm(x, g, w, layer=0, *, tm_target=1056, tn=768, out_dtype=F32):
    m, d = x.shape
    w = w.reshape((1,) + w.shape) if w.ndim == 2 else w
    n = w.shape[2]
    tm = _tile(m, tm_target, 16)
    tn = min(tn, n)
    return pl.pallas_call(
        _norm_mm_kernel,
        grid=(m // tm, pl.cdiv(n, tn)),
        in_specs=[pl.BlockSpec((tm, d), lambda i, j: (i, 0)),
                  pl.BlockSpec((1, d), lambda i, j: (0, 0)),
                  pl.BlockSpec((1, d, tn), lambda i, j: (layer, 0, j))],
        out_specs=pl.BlockSpec((tm, tn), lambda i, j: (i, j)),
        out_shape=jax.ShapeDtypeStruct((m, n), out_dtype),
        scratch_shapes=[pltpu.VMEM((tm, d), BF16)],
        name="norm_mm",
    )(x, g.reshape(1, d), w)


def _ffn_up_kernel(x_ref, g_ref, wg_ref, wu_ref, o_ref, hn_ref):
    @pl.when(pl.program_id(1) == 0)
    def _():
        hn_ref[...] = _rms_rows(x_ref[...], g_ref[...]).astype(BF16)

    hn = hn_ref[...]
    gate = jnp.dot(hn, wg_ref[0].astype(BF16), preferred_element_type=F32)
    up = jnp.dot(hn, wu_ref[0].astype(BF16), preferred_element_type=F32)
    o_ref[...] = (0.5 * _silu(gate) * up).astype(o_ref.dtype)


def ffn_up(x, g, wg, wu, layer, *, tm_target=1056, tf=512):
    m, d = x.shape
    f = wg.shape[2]
    tm = _tile(m, tm_target, 16)
    return pl.pallas_call(
        _ffn_up_kernel,
        grid=(m // tm, pl.cdiv(f, tf)),
        in_specs=[pl.BlockSpec((tm, d), lambda i, j: (i, 0)),
                  pl.BlockSpec((1, d), lambda i, j: (0, 0)),
                  pl.BlockSpec((1, d, tf), lambda i, j: (layer, 0, j)),
                  pl.BlockSpec((1, d, tf), lambda i, j: (layer, 0, j))],
        out_specs=pl.BlockSpec((tm, tf), lambda i, j: (i, j)),
        out_shape=jax.ShapeDtypeStruct((m, f), BF16),
        scratch_shapes=[pltpu.VMEM((tm, d), BF16)],
        name="ffn_up",
    )(x, g.reshape(1, d), wg, wu)


def _mm_res_kernel(*refs, n_lhs):
    a_refs = refs[:n_lhs]
    w_refs = refs[n_lhs:2 * n_lhs]
    x_ref, o_ref = refs[2 * n_lhs], refs[2 * n_lhs + 1]
    acc = x_ref[...]
    for a_ref, w_ref in zip(a_refs, w_refs):
        acc = acc + jnp.dot(a_ref[...], w_ref[0].astype(BF16), preferred_element_type=F32)
    o_ref[...] = acc


def mm_res(lhs, w, layer, x, *, tm_target=1056, tn=512):
    m, n = x.shape
    kk = lhs[0].shape[1]
    assert all(a.shape == (m, kk) for a in lhs) and w.shape[1] == kk * len(lhs)
    tm = _tile(m, tm_target, 16)
    tn = min(tn, n)
    nl = len(lhs)
    in_specs = [pl.BlockSpec((tm, kk), lambda i, j: (i, 0)) for _ in lhs]
    in_specs += [pl.BlockSpec((1, kk, tn), lambda i, j, r=r: (layer, r, j)) for r in range(nl)]
    in_specs += [pl.BlockSpec((tm, tn), lambda i, j: (i, j))]
    return pl.pallas_call(
        functools.partial(_mm_res_kernel, n_lhs=nl),
        grid=(m // tm, n // tn),
        in_specs=in_specs,
        out_specs=pl.BlockSpec((tm, tn), lambda i, j: (i, j)),
        out_shape=jax.ShapeDtypeStruct((m, n), F32),
        name="mm_res",
    )(*lhs, *([w] * nl), x)


def _rms_kernel(x_ref, g_ref, o_ref):
    o_ref[...] = _rms_rows(x_ref[...], g_ref[...])


def rms_rows(x, g, row0, rows, *, tm_target=512):
    d = x.shape[1]
    tm = _tile(rows, tm_target, 8)
    assert row0 % tm == 0
    off = row0 // tm
    return pl.pallas_call(
        _rms_kernel,
        grid=(rows // tm,),
        in_specs=[pl.BlockSpec((tm, d), lambda i: (off + i, 0)),
                  pl.BlockSpec((1, d), lambda i: (0, 0))],
        out_specs=pl.BlockSpec((tm, d), lambda i: (i, 0)),
        out_shape=jax.ShapeDtypeStruct((rows, d), F32),
        name="final_norm",
    )(x, g.reshape(1, d))


def _conv_rows(xp_ref, x, w, first, halo8, taps):
    t = x.shape[0]

    @pl.when(first)
    def _():
        xp_ref[0:SUBLANE, :] = halo8

    xp_ref[SUBLANE:SUBLANE + t, :] = x
    y = x * w[taps - 1:taps, :]
    for i in range(taps - 1):
        sh = taps - 1 - i
        y = y + xp_ref[SUBLANE - sh:SUBLANE - sh + t, :] * w[i:i + 1, :]
    xp_ref[0:SUBLANE, :] = xp_ref[t:t + SUBLANE, :]
    return y


def _split_bf16(a):
    hi = a.astype(BF16)
    return hi, (a - hi.astype(F32)).astype(BF16)


def _dot3(a, b):
    ah, al = _split_bf16(a)
    bh, bl = _split_bf16(b)
    return (jnp.dot(ah, bh, preferred_element_type=F32)
            + (jnp.dot(ah, bl, preferred_element_type=F32) + jnp.dot(al, bh, preferred_element_type=F32)))


def _dot_mask(a, mask):
    a1 = a.astype(BF16)
    r1 = a - a1.astype(F32)
    a2 = r1.astype(BF16)
    a3 = (r1 - a2.astype(F32)).astype(BF16)
    return (jnp.dot(a1, mask, preferred_element_type=F32)
            + (jnp.dot(a2, mask, preferred_element_type=F32) + jnp.dot(a3, mask, preferred_element_type=F32)))


def _unit_lower_inverse(mats, top):
    c = mats[0].shape[0]
    ri = lax.broadcasted_iota(jnp.int32, (c, c), 0)
    ci = lax.broadcasted_iota(jnp.int32, (c, c), 1)
    eye = (ri == ci).astype(F32)
    s = min(INV_BASE, top)
    ps = [jnp.where((ri // s) == (ci // s), -a, 0.0) for a in mats]
    ts = [eye + p for p in ps]
    k = 2
    while k < s:
        ps = [_dot(p, p) for p in ps]
        ts = [t + _dot(t, p) for t, p in zip(ts, ps)]
        k *= 2
    while s < top:
        blk = ((ri // (2 * s)) == (ci // (2 * s))) & ((ri // s) != (ci // s))
        tos = [_dot(t, jnp.where(blk, a, 0.0)) for t, a in zip(ts, mats)]
        ts = [t - _dot(to, t) for t, to in zip(ts, tos)]
        s *= 2
    return ts


def _delta_kernel(q_ref, k_ref, v_ref, gt_ref, ba_ref, cq_ref, ck_ref, cv_ref, wq_ref, wk_ref, wv_ref,
                  s0_ref, alog_ref, dt_ref, on_ref, o_ref, sn_ref, xq_ref, xk_ref, xv_ref, s_ref,
                  *, seg, nseg, hb, dk, chunk, taps, n_heads):
    hblk = pl.program_id(1)
    c = pl.program_id(2)
    rows = seg * nseg
    cc = chunk
    wid = hb * dk

    @pl.when(c == 0)
    def _():
        s_ref[...] = s0_ref[...]
        xq_ref[:, 0:SUBLANE, :] = cq_ref[...]
        xk_ref[:, 0:SUBLANE, :] = ck_ref[...]
        xv_ref[:, 0:SUBLANE, :] = cv_ref[...]

    def conv(x_ref, xp_ref, w_ref):
        w = w_ref[...]
        x3 = x_ref[...].reshape(nseg, seg, wid)
        xp_ref[:, SUBLANE:SUBLANE + seg, :] = x3
        y = x3 * w[taps - 1:taps, :]
        for i in range(taps - 1):
            sh = taps - 1 - i
            y = y + xp_ref[:, SUBLANE - sh:SUBLANE - sh + seg, :] * w[i:i + 1, :]
        xp_ref[:, 0:SUBLANE, :] = xp_ref[:, seg:seg + SUBLANE, :]
        return _silu(y).reshape(rows, wid)

    yq_all = conv(q_ref, xq_ref, wq_ref)
    yk_all = conv(k_ref, xk_ref, wk_ref)
    yv_all = conv(v_ref, xv_ref, wv_ref)

    ba = ba_ref[...]
    lane = lax.broadcasted_iota(jnp.int32, ba.shape, 1)
    sig = _sigmoid(ba)
    gfull = -jnp.exp(alog_ref[...]) * _softplus(ba + dt_ref[...])
    gate = gt_ref[...]

    ri = lax.broadcasted_iota(jnp.int32, (cc, cc), 0)
    ci = lax.broadcasted_iota(jnp.int32, (cc, cc), 1)
    same = (ri // seg) == (ci // seg)
    incl = same & (ci <= ri)
    strict = same & (ci < ri)
    seg_last = (ri // seg) * seg + (seg - 1)
    row_in_seg = ri % seg
    pad = cc - rows

    heads = range(hb)

    def prep(hh):
        h = hblk * hb + hh
        sl = slice(hh * dk, (hh + 1) * dk)
        yq, yk, v = yq_all[:, sl], yk_all[:, sl], yv_all[:, sl]
        q = yq * lax.rsqrt(jnp.sum(yq * yq, axis=-1, keepdims=True) + EPS) * (dk ** -0.5)
        k = yk * lax.rsqrt(jnp.sum(yk * yk, axis=-1, keepdims=True) + EPS)
        beta = jnp.sum(jnp.where(lane == h, sig, 0.0), axis=1, keepdims=True)
        g = jnp.sum(jnp.where(lane == n_heads + h, gfull, 0.0), axis=1, keepdims=True)
        if pad:
            zrow = jnp.zeros((pad, dk), F32)
            q = jnp.concatenate([q, zrow], axis=0)
            k = jnp.concatenate([k, zrow], axis=0)
            v = jnp.concatenate([v, zrow], axis=0)
            beta = jnp.concatenate([beta, jnp.zeros((pad, 1), F32)], axis=0)
            g = jnp.concatenate([g, jnp.zeros((pad, 1), F32)], axis=0)
        gcb = jnp.broadcast_to(g, (cc, cc))
        s = 1
        while s < seg:
            gcb = gcb + jnp.where(row_in_seg >= s, pltpu.roll(gcb, s, 0), 0.0)
            s *= 2
        grow = jnp.sum(jnp.where(ri == ci, gcb, 0.0), axis=0, keepdims=True)
        dec = jnp.where(incl, jnp.exp(jnp.where(incl, gcb - grow, 0.0)), 0.0)
        g_col = gcb[:, 0:1]
        g_end = jnp.sum(jnp.where(ci == seg_last, grow, 0.0), axis=1, keepdims=True)
        return dict(q=q, k=k, v=v, beta=beta, dec=dec, g_col=g_col, g_end=g_end, gamma=jnp.exp(g_col))

    hd_ = [prep(hh) for hh in heads]
    kks = [_dot_nt(p['k'], p['k']) for p in hd_]
    t_invs = _unit_lower_inverse([jnp.where(strict, p['beta'] * kk * p['dec'], 0.0) for p, kk in zip(hd_, kks)], seg)
    sols = [_dot3(t, jnp.concatenate([(p['beta'] * p['gamma']) * p['k'], p['beta'] * p['v']], axis=1))
            for p, t in zip(hd_, t_invs)]
    attns = [_dot_nt(p['q'], p['k']) * p['dec'] for p in hd_]
    kdts = [(p['k'] * jnp.exp(p['g_end'] - p['g_col'])).T for p in hd_]

    us, oqs = [], []
    for hh in heads:
        w_c, u_c = sols[hh][:, :dk], sols[hh][:, dk:]
        q_dec = hd_[hh]['q'] * hd_[hh]['gamma']
        ul, ol = [], []
        for i in range(nseg):
            r = slice(i * seg, (i + 1) * seg)
            st = s_ref[i, hh]
            ul.append(u_c[r] - _dot(w_c[r], st))
            ol.append(_dot(q_dec[r], st))
        if pad:
            ul.append(jnp.zeros((pad, dk), F32))
            ol.append(jnp.zeros((pad, dk), F32))
        us.append(ul[0] if len(ul) == 1 else jnp.concatenate(ul, axis=0))
        oqs.append(ol[0] if len(ol) == 1 else jnp.concatenate(ol, axis=0))
    os_ = [oq + _dot(attn, u) for oq, attn, u in zip(oqs, attns, us)]
    for hh in heads:
        g_end = hd_[hh]['g_end']
        for i in range(nseg):
            kd_i = kdts[hh] if cc == seg else jnp.where((ci // seg) == i, kdts[hh], 0.0)
            s_ref[i, hh] = jnp.exp(g_end[i * seg:i * seg + 1, :]) * s_ref[i, hh] + _dot(kd_i, us[hh])
    for hh in heads:
        sl = slice(hh * dk, (hh + 1) * dk)
        o = os_[hh][:rows]
        o = o * lax.rsqrt(jnp.mean(o * o, axis=-1, keepdims=True) + EPS) * on_ref[...]
        o_ref[:, sl] = (o * _silu(gate[:, sl])).astype(o_ref.dtype)

    @pl.when(c == pl.num_programs(2) - 1)
    def _():
        sn_ref[...] = s_ref[...]


def delta_heads(z, row0, nb, t, conv8, s0, conv_w, alog_row, dt_row, out_norm, *, n_heads, dk, cols, hb):
    cq, ck, cv, cg, cba = cols
    taps = conv_w.shape[0]
    chunk = DELTA_CHUNK
    assert dk == chunk == LANE and t % SUBLANE == 0 and n_heads % hb == 0
    if t >= chunk:
        assert t % chunk == 0
        seg, nseg = chunk, 1
    else:
        assert chunk % t == 0
        seg, nseg = t, _tile(nb, chunk // t, 1)
    rows = seg * nseg
    nc = t // seg
    assert row0 % rows == 0 and all(cb % hb == 0 for cb in (cq, ck, cv, cg))
    r0 = row0 // rows
    wid = hb * dk

    def zspec(cb):
        return pl.BlockSpec((rows, wid), lambda b, h, c: (r0 + b * nc + c, cb // hb + h))

    def cspec(cb):
        return pl.BlockSpec((nseg, SUBLANE, wid), lambda b, h, c: (b, 0, cb // hb + h))

    def wspec(cb):
        return pl.BlockSpec((taps, wid), lambda b, h, c: (0, cb // hb + h))

    row = pl.BlockSpec((1, LANE), lambda b, h, c: (0, 0))
    sspec = pl.BlockSpec((nseg, hb, dk, dk), lambda b, h, c: (b, h, 0, 0))
    kern = functools.partial(_delta_kernel, seg=seg, nseg=nseg, hb=hb, dk=dk, chunk=chunk, taps=taps,
                             n_heads=n_heads)
    return pl.pallas_call(
        kern,
        grid=(nb // nseg, n_heads // hb, nc),
        in_specs=[zspec(cq), zspec(ck), zspec(cv), zspec(cg),
                  pl.BlockSpec((rows, LANE), lambda b, h, c: (r0 + b * nc + c, cba)),
                  cspec(cq), cspec(ck), cspec(cv), wspec(cq), wspec(ck), wspec(cv),
                  sspec, row, row, row],
        out_specs=[pl.BlockSpec((rows, wid), lambda b, h, c: (b * nc + c, h)), sspec],
        out_shape=[jax.ShapeDtypeStruct((nb * t, n_heads * dk), BF16),
                   jax.ShapeDtypeStruct((nb, n_heads, dk, dk), F32)],
        scratch_shapes=[pltpu.VMEM((nseg, SUBLANE + seg, wid), F32)] * 3 + [pltpu.VMEM((nseg, hb, dk, dk), F32)],
        name="delta_heads",
    )(z, z, z, z, z, conv8, conv8, conv8, conv_w, conv_w, conv_w, s0, alog_row, dt_row, out_norm)


def _pool_kernel(u_ref, st_ref, w_ref, sc_ref, o_ref, xp_ref, *, tt, pos0, halo, gw):
    tb = pl.program_id(1)

    @pl.when(tb == 0)
    def _():
        xp_ref[0:halo, :] = st_ref[0]

    u = u_ref[...]
    xp_ref[halo:halo + tt, :] = u
    pos = pos0 + tb * tt + lax.broadcasted_iota(jnp.int32, (tt, 1), 0)
    for gi, wsz in enumerate(POOL_WINDOWS):
        lo, hi = gi * gw, (gi + 1) * gw
        acc = u[:, lo:hi]
        for i in range(1, wsz):
            acc = acc + xp_ref[halo - i:halo - i + tt, lo:hi]
        cnt = jnp.minimum(wsz, pos + 1).astype(F32)
        d = acc * (1.0 / cnt) - u[:, lo:hi]
        y = _dot(d, w_ref[gi]) * sc_ref[:, lo:hi]
        o_ref[:, lo:hi] = y.astype(o_ref.dtype)
    xp_ref[0:halo, :] = xp_ref[tt:tt + halo, :]


def pool_groups(z, row0, nb, t, state16, pool_w, pool_scale, *, col_block, pos0, tt_target=256):
    db = pool_scale.shape[-1]
    halo = state16.shape[1]
    tt = _tile(t, tt_target, SUBLANE)
    assert row0 % tt == 0
    nt, r0 = t // tt, row0 // tt
    gw = db // len(POOL_WINDOWS)
    kern = functools.partial(_pool_kernel, tt=tt, pos0=pos0, halo=halo, gw=gw)
    return pl.pallas_call(
        kern,
        grid=(nb, nt),
        in_specs=[pl.BlockSpec((tt, db), lambda b, i: (r0 + b * nt + i, col_block)),
                  pl.BlockSpec((1, halo, db), lambda b, i: (b, 0, 0)),
                  pl.BlockSpec(pool_w.shape, lambda b, i: (0, 0, 0)),
                  pl.BlockSpec((1, db), lambda b, i: (0, 0))],
        out_specs=pl.BlockSpec((tt, db), lambda b, i: (b * nt + i, 0)),
        out_shape=jax.ShapeDtypeStruct((nb * t, db), BF16),
        scratch_shapes=[pltpu.VMEM((halo + tt, db), F32)],
        name="pool_groups",
    )(z, state16, pool_w, pool_scale.reshape(1, db))


def _rglru_kernel(xc_ref, gc_ref, c8_ref, cw_ref, cb_ref, wa_ref, wx_ref, ba_ref, bx_ref, lam_ref, h0_ref,
                  y_ref, hl_ref, xp_ref, h_ref, *, taps, n_blocks, bs):
    tb = pl.program_id(1)
    first = tb == 0

    @pl.when(first)
    def _():
        h_ref[...] = h0_ref[0]

    x = _conv_rows(xp_ref, xc_ref[...], cw_ref[...], first, c8_ref[0], taps) + cb_ref[...]
    tt = x.shape[0]
    rs, is_ = [], []
    for n in range(n_blocks):
        xb = x[:, n * bs:(n + 1) * bs]
        rs.append(_dot(xb, wa_ref[n]))
        is_.append(_dot(xb, wx_ref[n]))
    r = _sigmoid(jnp.concatenate(rs, axis=1) + ba_ref[...])
    i = _sigmoid(jnp.concatenate(is_, axis=1) + bx_ref[...])
    log_a = -LRU_C * r * _softplus(-lam_ref[...])
    a = jnp.exp(log_a)
    b = jnp.sqrt(-_expm1(2.0 * log_a)) * (i * x)
    row = lax.broadcasted_iota(jnp.int32, a.shape, 0) % SUBLANE
    s = 1
    while s < SUBLANE:
        keep = row >= s
        b = jnp.where(keep, a * pltpu.roll(b, s, 0) + b, b)
        a = jnp.where(keep, a * pltpu.roll(a, s, 0), a)
        s *= 2
    carry = h_ref[...]
    groups = []
    for r0 in range(0, tt, SUBLANE):
        hg = b[r0:r0 + SUBLANE] + a[r0:r0 + SUBLANE] * carry
        carry = hg[SUBLANE - 1:SUBLANE, :]
        groups.append(hg)
    hs = groups[0] if len(groups) == 1 else jnp.concatenate(groups, axis=0)
    h_last = carry
    h_ref[...] = h_last
    y_ref[...] = (hs * _gelu_tanh(gc_ref[...])).astype(y_ref.dtype)

    @pl.when(tb == pl.num_programs(1) - 1)
    def _():
        hl_ref[0] = h_last


def rglru_blocks(z, row0, nb, t, conv8, h0, conv_w, conv_b, wa, ba, wx, bx, lam, *, tt_target=256):
    dc = lam.shape[-1]
    taps = conv_w.shape[0]
    n_blocks, bs = wa.shape[0], wa.shape[1]
    tt = _tile(t, tt_target, SUBLANE)
    assert row0 % tt == 0
    nt, r0 = t // tt, row0 // tt
    vec = pl.BlockSpec((1, dc), lambda b, i: (0, 0))
    kern = functools.partial(_rglru_kernel, taps=taps, n_blocks=n_blocks, bs=bs)
    return pl.pallas_call(
        kern,
        grid=(nb, nt),
        in_specs=[pl.BlockSpec((tt, dc), lambda b, i: (r0 + b * nt + i, 0)),
                  pl.BlockSpec((tt, dc), lambda b, i: (r0 + b * nt + i, 1)),
                  pl.BlockSpec((1, SUBLANE, dc), lambda b, i: (b, 0, 0)),
                  pl.BlockSpec((taps, dc), lambda b, i: (0, 0)),
                  vec,
                  pl.BlockSpec(wa.shape, lambda b, i: (0, 0, 0)),
                  pl.BlockSpec(wx.shape, lambda b, i: (0, 0, 0)),
                  vec, vec, vec,
                  pl.BlockSpec((1, 1, dc), lambda b, i: (b, 0, 0))],
        out_specs=[pl.BlockSpec((tt, dc), lambda b, i: (b * nt + i, 0)),
                   pl.BlockSpec((1, 1, dc), lambda b, i: (b, 0, 0))],
        out_shape=[jax.ShapeDtypeStruct((nb * t, dc), BF16),
                   jax.ShapeDtypeStruct((nb, 1, dc), F32)],
        scratch_shapes=[pltpu.VMEM((SUBLANE + tt, dc), F32), pltpu.VMEM((1, dc), F32)],
        name="rglru_blocks",
    )(z, z, conv8, conv_w, conv_b.reshape(1, dc), wa, wx, ba.reshape(1, dc), bx.reshape(1, dc),
      lam.reshape(1, dc), h0.reshape(nb, 1, dc))


def _fox_prep_kernel(f_ref, bf_ref, lf_ref, cum_ref, cumt_ref, *, n_heads):
    lf = -_softplus(-(f_ref[...] + bf_ref[...]))
    lf_ref[...] = lf
    cum = _cumsum_rows(lf)
    cum_ref[...] = cum
    cumt_ref[0] = cum.T[0:n_heads, :]


def fox_prep(z, nb, t, bf_row, *, col_block, n_heads):
    return pl.pallas_call(
        functools.partial(_fox_prep_kernel, n_heads=n_heads),
        grid=(nb,),
        in_specs=[pl.BlockSpec((t, LANE), lambda b: (b, col_block)),
                  pl.BlockSpec((1, LANE), lambda b: (0, 0))],
        out_specs=[pl.BlockSpec((t, LANE), lambda b: (b, 0)),
                   pl.BlockSpec((t, LANE), lambda b: (b, 0)),
                   pl.BlockSpec((1, n_heads, t), lambda b: (b, 0, 0))],
        out_shape=[jax.ShapeDtypeStruct((nb * t, LANE), F32),
                   jax.ShapeDtypeStruct((nb * t, LANE), F32),
                   jax.ShapeDtypeStruct((nb, n_heads, t), F32)],
        name="fox_prep",
    )(z, bf_row)


def _fox_flash_kernel(qi_ref, ki_ref, q_ref, k_ref, v_ref, fq_ref, fk_ref, o_ref, m_ref, l_ref, acc_ref, fqc_ref,
                      *, scale, hb, hd):
    hblk = pl.program_id(1)
    pair = pl.program_id(2)
    qi, ki = qi_ref[pair], ki_ref[pair]
    tq, tk = q_ref.shape[0], k_ref.shape[0]
    log2e = math.log2(math.e)

    @pl.when(ki == 0)
    def _():
        m_ref[...] = jnp.full(m_ref.shape, NEG, F32)
        l_ref[...] = jnp.zeros(l_ref.shape, F32)
        acc_ref[...] = jnp.zeros(acc_ref.shape, F32)
        fq = fq_ref[...] * log2e
        lane = lax.broadcasted_iota(jnp.int32, fq.shape, 1)
        for hh in range(hb):
            col = jnp.sum(jnp.where(lane == hblk * hb + hh, fq, 0.0), axis=1, keepdims=True)
            fqc_ref[hh] = jnp.broadcast_to(col, fqc_ref.shape[1:])

    def attend(diagonal):
        heads = range(hb)
        kb = k_ref[...].astype(BF16)
        vb = v_ref[...].astype(BF16)
        fks = [fk_ref[0, pl.ds(hblk * hb + hh, 1), :] * log2e for hh in heads]
        band = FLASH_ROW_BAND
        chains = [(r0, hh) for r0 in range(0, tq, band) for hh in heads]
        groups = [chains[c0:c0 + FLASH_CHAINS] for c0 in range(0, len(chains), FLASH_CHAINS)]

        def score_tiles(group):
            ts, nks = [], []
            for r0, hh in group:
                sl = slice(hh * hd, (hh + 1) * hd)
                nk = min(tk, -(-(r0 + band) // LANE) * LANE) if diagonal else tk
                qb = q_ref[r0:r0 + band, sl].astype(BF16)
                t2 = _dot_nt(qb, kb[:nk, sl]) * (scale * log2e) - fks[hh][:, :nk]
                if diagonal:
                    ri = lax.broadcasted_iota(jnp.int32, (band, nk), 0) + r0
                    ci = lax.broadcasted_iota(jnp.int32, (band, nk), 1)
                    t2 = jnp.where(ci <= ri, t2, NEG)
                ts.append(t2)
                nks.append(nk)
            return ts, nks

        pending = score_tiles(groups[0])
        for gi, group in enumerate(groups):
            ts, nks = pending
            if gi + 1 < len(groups):
                pending = score_tiles(groups[gi + 1])
            alphas, ps, m_news = [], [], []
            for (r0, hh), t2, nk in zip(group, ts, nks):
                rs = slice(r0, r0 + band)
                fq = fqc_ref[hh, rs, :]
                m_old = m_ref[hh, rs, :]
                m_new = jnp.maximum(m_old, jnp.max(t2, axis=1, keepdims=True) + fq)
                shift = fq - m_new
                alphas.append(jnp.exp2(m_old - m_new))
                ps.append([jnp.exp2(t2[:, j:j + LANE] + shift) for j in range(0, nk, LANE)])
                m_news.append(m_new)
            for (r0, hh), p, alpha, m_new, nk in zip(group, ps, alphas, m_news, nks):
                rs = slice(r0, r0 + band)
                sl = slice(hh * hd, (hh + 1) * hd)
                psum = p[0]
                for pj in p[1:]:
                    psum = psum + pj
                l_ref[hh, rs, :] = alpha * l_ref[hh, rs, :] + jnp.sum(psum, axis=1, keepdims=True)
                pb = jnp.concatenate([pj.astype(BF16) for pj in p], axis=1) if len(p) > 1 else p[0].astype(BF16)
                acc_ref[rs, sl] = alpha * acc_ref[rs, sl] + jnp.dot(pb, vb[:nk, sl], preferred_element_type=F32)
                m_ref[hh, rs, :] = m_new

    @pl.when(ki < qi)
    def _():
        attend(False)

    @pl.when(ki == qi)
    def _():
        attend(True)
        for hh in range(hb):
            sl = slice(hh * hd, (hh + 1) * hd)
            o_ref[:, sl] = (acc_ref[:, sl] / l_ref[hh]).astype(o_ref.dtype)


def fox_prompt(z, cum, cumt, nb, t, *, n_heads, hd, cols, tq_target=512, hb=2):
    cq, ck, cv = cols
    tq = _tile(t, tq_target, LANE)
    nq = t // tq
    assert n_heads % hb == 0 and all(cb % hb == 0 for cb in cols)
    pairs = [(i, j) for i in range(nq) for j in range(i + 1)]
    qi_tab = jnp.asarray([p[0] for p in pairs], jnp.int32)
    ki_tab = jnp.asarray([p[1] for p in pairs], jnp.int32)
    wid = hb * hd
    kern = functools.partial(_fox_flash_kernel, scale=hd ** -0.5, hb=hb, hd=hd)

    def kv(cb):
        return pl.BlockSpec((tq, wid), lambda b, h, p, qt, kt: (b * nq + kt[p], cb // hb + h))

    grid_spec = pltpu.PrefetchScalarGridSpec(
        num_scalar_prefetch=2,
        grid=(nb, n_heads // hb, len(pairs)),
        in_specs=[pl.BlockSpec((tq, wid), lambda b, h, p, qt, kt: (b * nq + qt[p], cq // hb + h)),
                  kv(ck), kv(cv),
                  pl.BlockSpec((tq, LANE), lambda b, h, p, qt, kt: (b * nq + qt[p], 0)),
                  pl.BlockSpec((1, n_heads, tq), lambda b, h, p, qt, kt: (b, 0, kt[p]))],
        out_specs=pl.BlockSpec((tq, wid), lambda b, h, p, qt, kt: (b * nq + qt[p], h)),
        scratch_shapes=[pltpu.VMEM((hb, tq, LANE), F32), pltpu.VMEM((hb, tq, LANE), F32), pltpu.VMEM((tq, wid), F32),
                        pltpu.VMEM((hb, tq, LANE), F32)],
    )
    assert hd == LANE
    return pl.pallas_call(
        kern,
        grid_spec=grid_spec,
        out_shape=jax.ShapeDtypeStruct((nb * t, n_heads * hd), BF16),
        name="fox_prompt",
    )(qi_tab, ki_tab, z, z, z, cum, cumt)


def _fox_sample_kernel(pt_ref, q_ref, kn_ref, vn_ref, f_ref, bf_ref, *rest, n_heads, hd, pages_per_step, scale):
    g_pages = pages_per_step
    k_refs = rest[0:g_pages]
    v_refs = rest[g_pages:2 * g_pages]
    lf_refs = rest[2 * g_pages:3 * g_pages]
    o_ref, lfo_ref, m_ref, l_ref, acc_ref, carry_ref, qall_ref, b0_ref = rest[3 * g_pages:]
    step = pl.program_id(1)
    t = q_ref.shape[0]
    page = lf_refs[0].shape[3]
    wide = page * n_heads
    rows = n_heads * t
    log2e = math.log2(math.e)

    @pl.when(step == 0)
    def _():
        lf = -_softplus(-(f_ref[...] + bf_ref[...]))
        lfo_ref[...] = lf
        fn = _cumsum_rows(lf) * log2e
        carry_ref[...] = jnp.zeros(carry_ref.shape, F32)
        q = q_ref[...].astype(BF16)
        kn = jnp.concatenate([kn_ref[...], jnp.zeros((page - t, n_heads * hd), F32)], axis=0)
        vn = jnp.concatenate([vn_ref[...], jnp.zeros((page - t, n_heads * hd), F32)], axis=0)
        ri = lax.broadcasted_iota(jnp.int32, (t, page), 0)
        ci = lax.broadcasted_iota(jnp.int32, (t, page), 1)
        lane_head = lax.broadcasted_iota(jnp.int32, (t, wide), 1) % n_heads
        for h in range(n_heads):
            rs = slice(h * t, (h + 1) * t)
            sl = slice(h * hd, (h + 1) * hd)
            col = lf[:, h:h + 1] * log2e
            fnrow = jnp.sum(jnp.where(ri <= ci, col, 0.0), axis=0, keepdims=True)
            fnc = fn[:, h:h + 1]
            s2 = _dot_nt(q[:, sl], kn[:, sl]) * (scale * log2e) + fnc - fnrow
            s2 = jnp.where(ci <= ri, s2, NEG)
            m0 = jnp.max(s2, axis=1, keepdims=True)
            p = jnp.exp2(s2 - m0)
            m_ref[rs, :] = jnp.broadcast_to(m0, (t, LANE))
            l_ref[rs, :] = jnp.broadcast_to(jnp.sum(p, axis=1, keepdims=True), (t, LANE))
            acc_ref[rs, :] = _dot(p, vn[:, sl])
            qall_ref[rs, :] = q[:, sl]
            b0_ref[rs, :] = jnp.where(lane_head == h, fnc, NEG)

    lft = jnp.concatenate([lf_refs[g][0, 0] for g in range(g_pages)], axis=0) * log2e
    ri = lax.broadcasted_iota(jnp.int32, (page, wide), 0)
    ci = lax.broadcasted_iota(jnp.int32, (page, wide), 1)
    rr = _dot_mask(lft, (ri > ci // n_heads).astype(BF16))
    tot = jnp.sum(lft, axis=1, keepdims=True)
    run = carry_ref[:, 0:1]
    rc = [None] * g_pages
    for g in reversed(range(g_pages)):
        rc[g] = rr[g * n_heads:(g + 1) * n_heads] + run
        run = run + tot[g * n_heads:(g + 1) * n_heads]
    carry_ref[...] = jnp.broadcast_to(run, carry_ref.shape)

    qall = qall_ref[...]
    tiles = range(0, wide, LANE)
    for g0 in range(0, g_pages, SAMPLE_PAGE_GROUP):
        group = range(g0, min(g0 + SAMPLE_PAGE_GROUP, g_pages))
        scores = []
        for g in group:
            bias = b0_ref[...] + jnp.broadcast_to(rc[g][:, None, :], (n_heads, t, wide)).reshape(rows, wide)
            scores.append(_dot_nt(qall, k_refs[g][0, 0]) * (scale * log2e) + bias)
        smax = scores[0]
        for s2 in scores[1:]:
            smax = jnp.maximum(smax, s2)
        m_old = m_ref[...]
        m_new = jnp.maximum(m_old, jnp.max(smax, axis=1, keepdims=True))
        alpha = jnp.exp2(m_old - m_new)
        psum, pv = None, None
        for g, s2 in zip(group, scores):
            p = [jnp.exp2(s2[:, j:j + LANE] - m_new) for j in tiles]
            for pj in p:
                psum = pj if psum is None else psum + pj
            pb = jnp.concatenate([pj.astype(BF16) for pj in p], axis=1)
            d = jnp.dot(pb, v_refs[g][0, 0].astype(BF16), preferred_element_type=F32)
            pv = d if pv is None else pv + d
        l_ref[...] = alpha * l_ref[...] + jnp.sum(psum, axis=1, keepdims=True)
        acc_ref[...] = alpha * acc_ref[...] + pv
        m_ref[...] = m_new

    @pl.when(step == pl.num_programs(1) - 1)
    def _():
        o = acc_ref[...] / l_ref[...]
        for h in range(n_heads):
            o_ref[:, h * hd:(h + 1) * hd] = o[h * t:(h + 1) * t, :].astype(o_ref.dtype)


def fox_sample(z, row0, nb, t, page_table, kc, vc, lfc, layer, bf_row, *, n_heads, hd, cols, pages_per_step=16):
    cq, ck, cv, cf = cols
    n_pages = page_table.shape[1]
    page = lfc.shape[3]
    g_pages = pages_per_step
    while n_pages % g_pages:
        g_pages //= 2
    ns = n_pages // g_pages
    dd = n_heads * hd
    assert row0 % t == 0 and t % SUBLANE == 0
    r0 = row0 // t

    def zspec(cb, w):
        return pl.BlockSpec((t, w), lambda b, s, pt: (r0 + b, cb))

    def pspec(g, r, w):
        return pl.BlockSpec((1, 1, r, w), lambda b, s, pt, g=g: (layer, pt[b, (ns - 1 - s) * g_pages + g], 0, 0))

    kern = functools.partial(_fox_sample_kernel, n_heads=n_heads, hd=hd, pages_per_step=g_pages, scale=hd ** -0.5)
    grid_spec = pltpu.PrefetchScalarGridSpec(
        num_scalar_prefetch=1,
        grid=(nb, ns),
        in_specs=[zspec(cq, dd), zspec(ck, dd), zspec(cv, dd), zspec(cf, LANE),
                  pl.BlockSpec((1, LANE), lambda b, s, pt: (0, 0))]
        + [pspec(g, page * n_heads, hd) for g in range(g_pages)] * 2
        + [pspec(g, n_heads, page) for g in range(g_pages)],
        out_specs=[pl.BlockSpec((t, dd), lambda b, s, pt: (b, 0)),
                   pl.BlockSpec((t, LANE), lambda b, s, pt: (b, 0))],
        scratch_shapes=[pltpu.VMEM((n_heads * t, LANE), F32), pltpu.VMEM((n_heads * t, LANE), F32),
                        pltpu.VMEM((n_heads * t, hd), F32), pltpu.VMEM((n_heads, LANE), F32),
                        pltpu.VMEM((n_heads * t, hd), BF16), pltpu.VMEM((n_heads * t, page * n_heads), F32)],
    )
    assert hd == LANE
    return pl.pallas_call(
        kern,
        grid_spec=grid_spec,
        out_shape=[jax.ShapeDtypeStruct((nb * t, dd), BF16), jax.ShapeDtypeStruct((nb * t, LANE), F32)],
        name="fox_sample",
    )(page_table, z, z, z, z, bf_row, *([kc] * g_pages), *([vc] * g_pages), *([lfc] * g_pages))


def _xattn_kernel(q_ref, k_ref, v_ref, o_ref, *, n_heads, hd, interleaved, scale):
    q = q_ref[...]
    for h in range(n_heads):
        sl = slice(h * hd, (h + 1) * hd)
        if interleaved:
            n_mem = k_ref.shape[2] // n_heads
            kh = k_ref[0, 0, pl.ds(h, n_mem, stride=n_heads), :]
            vh = v_ref[0, 0, pl.ds(h, n_mem, stride=n_heads), :]
        else:
            kh, vh = k_ref[:, sl], v_ref[:, sl]
        s = _dot_nt(q[:, sl], kh) * scale
        p = jnp.exp(s - jnp.max(s, axis=1, keepdims=True))
        o = _dot(p, vh) / jnp.sum(p, axis=1, keepdims=True)
        o_ref[:, sl] = o.astype(o_ref.dtype)


def xattn(qx, row0, nb, t, mk, mv, kspec, vspec, *, n_heads, hd, interleaved, tt_target=512):
    dx = n_heads * hd
    tt = _tile(t, tt_target, SUBLANE)
    assert row0 % tt == 0
    nt, r0 = t // tt, row0 // tt
    kern = functools.partial(_xattn_kernel, n_heads=n_heads, hd=hd, interleaved=interleaved, scale=hd ** -0.5)
    return pl.pallas_call(
        kern,
        grid=(nb, nt),
        in_specs=[pl.BlockSpec((tt, dx), lambda b, i: (r0 + b * nt + i, 0)), kspec, vspec],
        out_specs=pl.BlockSpec((tt, dx), lambda b, i: (b * nt + i, 0)),
        out_shape=jax.ShapeDtypeStruct((nb * t, dx), BF16),
        name="xattn",
    )(qx, mk, mv)


def _pad_cols(w, n):
    return jnp.pad(w, ((0, 0), (0, n - w.shape[1])))


def _halo_rows(buf, rows):
    return jnp.pad(buf, ((0, 0), (rows - buf.shape[1], 0), (0, 0)))


def kernel(x_prompt, x_sample, state_delta, state_conv_a, state_pool, state_lru, state_conv_c, cache_fox_k, cache_fox_v, cache_fox_logf, cache_mem_k, cache_mem_v, page_table, mem_prompt, ffn1_wg, ffn1_wu, ffn1_wd, ffn2_wg, ffn2_wu, ffn2_wd, norm_ffn1, norm_mix, norm_xattn, norm_ffn2, norm_mem, final_norm, xattn_wq, xattn_wk, xattn_wv, xattn_wo, even_w_in, even_w_out, delta_conv_w, delta_a_log, delta_dt_bias, delta_out_norm, pool_w, pool_scale, odd_w_in, odd_w_out, lru_conv_w, lru_conv_b, lru_wa, lru_ba, lru_wx, lru_bx, lru_lambda, fox_bf):
    bp, tp, d = x_prompt.shape
    bs, ts, _ = x_sample.shape
    mp, ms = bp * tp, bs * ts
    depth = ffn1_wg.shape[0]
    h_a, dk_a = state_delta.shape[2], state_delta.shape[3]
    d_a = h_a * dk_a
    d_b = state_pool.shape[-1]
    d_c = state_lru.shape[-1]
    h_d, hd_d = cache_fox_k.shape[3], cache_fox_k.shape[4]
    d_d = h_d * hd_d
    n_mem, h_x, hd_x = cache_mem_k.shape[2], cache_mem_k.shape[3], cache_mem_k.shape[4]
    d_x = h_x * hd_x
    page = cache_fox_k.shape[2]
    pos0_s = page_table.shape[1] * page
    assert d_a == d_b and d_c == d_d and 2 * h_a <= LANE and h_d <= LANE
    tn_z = 768
    nz = -(-(max(4 * d_a + d_b, 2 * d_c + 3 * d_d) + LANE) // tn_z) * tn_z
    bc =(4 * d_a + d_b) // LANE
    assert bc == (2 * d_c + 3 * d_d) // LANE

    x = jnp.concatenate([x_prompt.reshape(mp, d), x_sample.reshape(ms, d)], axis=0)

    mem2 = mem_prompt.reshape(bp * n_mem, d)
    mem_k = [norm_mm(mem2, norm_mem[l], xattn_wk, l, tn=d_x) for l in range(depth)]
    mem_v = [norm_mm(mem2, norm_mem[l], xattn_wv, l, tn=d_x) for l in range(depth)]
    kc4 = cache_fox_k.reshape(cache_fox_k.shape[0], cache_fox_k.shape[1], page * h_d, hd_d)
    vc4 = cache_fox_v.reshape(kc4.shape)
    lfc4 = jnp.swapaxes(cache_fox_logf, 2, 3)
    mk4 = cache_mem_k.reshape(depth, bs, n_mem * h_x, hd_x)
    mv4 = cache_mem_v.reshape(depth, bs, n_mem * h_x, hd_x)

    new_p = {n: [] for n in ('delta', 'conv_a', 'pool', 'lru', 'conv_c', 'fox_k', 'fox_v', 'fox_logf')}
    new_s = {n: [] for n in new_p}

    def seq_cols(zz, row0, nb, t, c0, c1, last=None):
        if last is None or last >= t:
            return lax.slice(zz, (row0, c0), (row0 + nb * t, c1)).reshape(nb, t, c1 - c0)
        idx = (row0 + jnp.arange(nb)[:, None] * t + (t - last) + jnp.arange(last)[None, :]).reshape(-1)
        return lax.slice_in_dim(zz[idx], c0, c1, axis=1).reshape(nb, last, c1 - c0)

    def new_buf(buf, zz, row0, nb, t, c0, c1):
        r = buf.shape[1]
        if t >= r:
            return seq_cols(zz, row0, nb, t, c0, c1, last=r)
        return jnp.concatenate([buf[:, t:], seq_cols(zz, row0, nb, t, c0, c1)], axis=1)

    for l in range(depth):
        j = l // 2
        a = ffn_up(x, norm_ffn1[l], ffn1_wg, ffn1_wu, l)
        x = mm_res([a], ffn1_wd, l, x, tm_target=1056, tn=FFN_DOWN_TN)

        if l % 2 == 0:
            w = even_w_in[j]
            w_in = _pad_cols(jnp.concatenate([w[:, :4 * d_a], w[:, 4 * d_a + 2 * h_a:], w[:, 4 * d_a:4 * d_a + 2 * h_a]],
                                             axis=1), nz).astype(BF16)
            z = norm_mm(x, norm_mix[l], w_in)
            hb = dk_a // LANE
            alog_row = jnp.zeros((1, LANE), F32).at[0, h_a:2 * h_a].set(delta_a_log[j])
            dt_row = jnp.zeros((1, LANE), F32).at[0, h_a:2 * h_a].set(delta_dt_bias[j])
            groups = ((0, bp, tp, jnp.zeros((bp,) + state_conv_a.shape[2:], F32),
                       jnp.zeros((bp,) + state_delta.shape[2:], F32), jnp.zeros((bp,) + state_pool.shape[2:], F32), 0, new_p),
                      (mp, bs, ts, state_conv_a[j], state_delta[j], state_pool[j], pos0_s, new_s))
            mix_a, mix_b = [], []
            for gi, (row0, nb, t, cbuf, s0, pbuf, pos0, new) in enumerate(groups):
                o, s_new = delta_heads(z, row0, nb, t, _halo_rows(cbuf, SUBLANE), s0, delta_conv_w[j], alog_row, dt_row,
                                       delta_out_norm[j].reshape(1, dk_a), n_heads=h_a, dk=dk_a,
                                       cols=(0, h_a * hb, 2 * h_a * hb, 3 * h_a * hb, bc),
                                       hb=DELTA_HEADS_PER_STEP if t >= DELTA_CHUNK else 1)
                yb = pool_groups(z, row0, nb, t, _halo_rows(pbuf, 2 * SUBLANE), pool_w[j].astype(BF16), pool_scale[j],
                                 col_block=4 * d_a // d_b, pos0=pos0)
                mix_a.append(o)
                mix_b.append(yb)
                new['delta'].append(s_new)
                new['conv_a'].append(new_buf(cbuf, z, row0, nb, t, 0, 3 * d_a))
                new['pool'].append(new_buf(pbuf, z, row0, nb, t, 4 * d_a, 4 * d_a + d_b))
            w_out = even_w_out
        else:
            w_in = _pad_cols(odd_w_in[j], nz).astype(BF16)
            z = norm_mm(x, norm_mix[l], w_in)
            bf_row = jnp.zeros((1, LANE), F32).at[0, :h_d].set(fox_bf[j])
            qb, kb, vb = 2 * d_c // hd_d, (2 * d_c + d_d) // hd_d, (2 * d_c + 2 * d_d) // hd_d
            groups = ((0, bp, tp, jnp.zeros((bp,) + state_conv_c.shape[2:], F32), jnp.zeros((bp, d_c), F32), new_p),
                      (mp, bs, ts, state_conv_c[j], state_lru[j], new_s))
            mix_a, mix_b = [], []
            for gi, (row0, nb, t, cbuf, h0, new) in enumerate(groups):
                yc, h_last = rglru_blocks(z, row0, nb, t, _halo_rows(cbuf, SUBLANE), h0, lru_conv_w[j], lru_conv_b[j],
                                          lru_wa[j].astype(BF16), lru_ba[j], lru_wx[j].astype(BF16), lru_bx[j],
                                          lru_lambda[j])
                if gi == 0:
                    lf, cum, cumt = fox_prep(z, nb, t, bf_row, col_block=bc, n_heads=h_d)
                    o = fox_prompt(z, cum, cumt, nb, t, n_heads=h_d, hd=hd_d, cols=(qb, kb, vb))
                else:
                    o, lf = fox_sample(z, row0, nb, t, page_table, kc4, vc4, lfc4, j, bf_row,
                                       n_heads=h_d, hd=hd_d,
                                       cols=(2 * d_c // d_d, (2 * d_c + d_d) // d_d, (2 * d_c + 2 * d_d) // d_d, bc))
                mix_a.append(yc)
                mix_b.append(o)
                new['lru'].append(h_last.reshape(nb, d_c))
                new['conv_c'].append(new_buf(cbuf, z, row0, nb, t, 0, d_c))
                new['fox_k'].append(seq_cols(z, row0, nb, t, 2 * d_c + d_d, 2 * d_c + 2 * d_d).reshape(nb, t, h_d, hd_d))
                new['fox_v'].append(seq_cols(z, row0, nb, t, 2 * d_c + 2 * d_d, 2 * d_c + 3 * d_d).reshape(nb, t, h_d, hd_d))
                new['fox_logf'].append(lf[:, :h_d].reshape(nb, t, h_d))
            w_out = odd_w_out

        x = mm_res([jnp.concatenate(mix_a, axis=0), jnp.concatenate(mix_b, axis=0)], w_out, j, x)

        qx = norm_mm(x, norm_xattn[l], xattn_wq, l, tn=d_x)
        o_p = xattn(qx, 0, bp, tp, mem_k[l], mem_v[l],
                    pl.BlockSpec((n_mem, d_x), lambda b, i: (b, 0)), pl.BlockSpec((n_mem, d_x), lambda b, i: (b, 0)),
                    n_heads=h_x, hd=hd_x, interleaved=False)
        o_s = xattn(qx, mp, bs, ts, mk4, mv4,
                    pl.BlockSpec((1, 1, n_mem * h_x, hd_x), lambda b, i, l=l: (l, b, 0, 0)),
                    pl.BlockSpec((1, 1, n_mem * h_x, hd_x), lambda b, i, l=l: (l, b, 0, 0)),
                    n_heads=h_x, hd=hd_x, interleaved=True)
        x = mm_res([jnp.concatenate([o_p, o_s], axis=0)], xattn_wo, l, x)

        a = ffn_up(x, norm_ffn2[l], ffn2_wg, ffn2_wu, l)
        x = mm_res([a], ffn2_wd, l, x, tm_target=1056, tn=FFN_DOWN_TN)

    y_p = rms_rows(x, final_norm, 0, mp).reshape(bp, tp, d)
    y_s = rms_rows(x, final_norm, mp, ms).reshape(bs, ts, d)
    mem_k_p = jnp.stack([m.reshape(bp, n_mem, h_x, hd_x) for m in mem_k])
    mem_v_p = jnp.stack([m.reshape(bp, n_mem, h_x, hd_x) for m in mem_v])
    names = ('delta', 'conv_a', 'pool', 'lru', 'conv_c', 'fox_k', 'fox_v', 'fox_logf')
    return ((y_p, y_s) + tuple(jnp.stack(new_p[n]) for n in names) + (mem_k_p, mem_v_p)
            + tuple(jnp.stack(new_s[n]) for n in names))
```

```python
import functools
import math

import jax
import jax.numpy as jnp
from jax import lax
from jax.experimental import pallas as pl
from jax.experimental.pallas import tpu as pltpu

F32 = jnp.float32
BF16 = jnp.bfloat16
EPS = 1e-6
LRU_C = 8.0
POOL_WINDOWS = (2, 4, 8, 16)
LANE = 128
SUBLANE = 8
DELTA_CHUNK = 128
FFN_DOWN_TN = 256
DELTA_HEADS_PER_STEP = 8
INV_BASE = 16
FLASH_ROW_BAND = 128
SAMPLE_PAGE_GROUP = 16
FLASH_CHAINS = 2
NEG = -1e30


def _tile(n, target, align):
    best = None
    for d in range(align, min(n, target) + 1, align):
        if n % d == 0:
            best = d
    return n if best is None else best


def _row_dest(dest, own_rows, block_rows, kern, width, n_prefetch=0):
    if dest is None:
        return own_rows, 0, kern, [], [], {}
    total, row0, into = dest
    assert row0 % block_rows == 0
    if into is None:
        into = jnp.zeros((total, width), BF16)

    def skipping(*refs):
        return kern(*refs[:n_prefetch], *refs[n_prefetch + 1:])

    return (total, row0 // block_rows, skipping, [pl.BlockSpec(memory_space=pl.ANY)], [into],
            dict(input_output_aliases={n_prefetch: 0}))


def _dot(a, b, hi=False):
    if hi:
        return jnp.dot(a.astype(F32), b.astype(F32), precision=lax.Precision.HIGHEST, preferred_element_type=F32)
    return jnp.dot(a.astype(BF16), b.astype(BF16), preferred_element_type=F32)


def _dot_nt(a, b):
    return lax.dot_general(a.astype(BF16), b.astype(BF16), (((1,), (1,)), ((), ())), preferred_element_type=F32)


def _sigmoid(x):
    return 1.0 / (1.0 + jnp.exp(-x))


def _silu(x):
    return x * _sigmoid(x)


def _softplus(x):
    return jnp.maximum(x, 0.0) + jnp.log1p(jnp.exp(-jnp.abs(x)))


def _expm1(x):
    u = jnp.exp(x)
    degenerate = (u == 1.0) | (u == 0.0)
    safe = jnp.where(degenerate, 2.0, u)
    return jnp.where(u == 1.0, x, jnp.where(u == 0.0, -1.0, (u - 1.0) * x / jnp.log(safe)))


def _gelu_tanh(x):
    return 0.5 * x * (1.0 + jnp.tanh(math.sqrt(2.0 / math.pi) * (x + 0.044715 * (x * x * x))))


def _cumsum_rows(x):
    n = x.shape[0]
    row = lax.broadcasted_iota(jnp.int32, x.shape, 0)
    s = 1
    while s < n:
        x = x + jnp.where(row >= s, pltpu.roll(x, s, 0), 0.0)
        s *= 2
    return x


def _rms_rows(x, g):
    ms = jnp.mean(x * x, axis=-1, keepdims=True)
    return x * lax.rsqrt(ms + EPS) * g


def _norm_mm_kernel(x_ref, g_ref, w_ref, o_ref, hn_ref):
    @pl.when(pl.program_id(1) == 0)
    def _():
        hn_ref[...] = _rms_rows(x_ref[...], g_ref[...]).astype(BF16)

    o_ref[...] = jnp.dot(hn_ref[...], w_ref[0].astype(BF16), preferred_element_type=F32).astype(o_ref.dtype)


def norm_mm(x, g, w, layer=0, *, tm_target=1056, tn=768, out_dtype=F32):
    m, d = x.shape
    w = w.reshape((1,) + w.shape) if w.ndim == 2 else w
    n = w.shape[2]
    tm = _tile(m, tm_target, 16)
    tn = min(tn, n)
    return pl.pallas_call(
        _norm_mm_kernel,
        grid=(m // tm, pl.cdiv(n, tn)),
        in_specs=[pl.BlockSpec((tm, d), lambda i, j: (i, 0)),
                  pl.BlockSpec((1, d), lambda i, j: (0, 0)),
                  pl.BlockSpec((1, d, tn), lambda i, j: (layer, 0, j))],
        out_specs=pl.BlockSpec((tm, tn), lambda i, j: (i, j)),
        out_shape=jax.ShapeDtypeStruct((m, n), out_dtype),
        scratch_shapes=[pltpu.VMEM((tm, d), BF16)],
        name="norm_mm",
    )(x, g.reshape(1, d), w)


def _ffn_up_kernel(x_ref, g_ref, wg_ref, wu_ref, o_ref, hn_ref):
    @pl.when(pl.program_id(1) == 0)
    def _():
        hn_ref[...] = _rms_rows(x_ref[...], g_ref[...]).astype(BF16)

    hn = hn_ref[...]
    gate = jnp.dot(hn, wg_ref[0].astype(BF16), preferred_element_type=F32)
    up = jnp.dot(hn, wu_ref[0].astype(BF16), preferred_element_type=F32)
    o_ref[...] = (0.5 * _silu(gate) * up).astype(o_ref.dtype)


def ffn_up(x, g, wg, wu, layer, *, tm_target=1056, tf=512):
    m, d = x.shape
    f = wg.shape[2]
    tm = _tile(m, tm_target, 16)
    return pl.pallas_call(
        _ffn_up_kernel,
        grid=(m // tm, pl.cdiv(f, tf)),
        in_specs=[pl.BlockSpec((tm, d), lambda i, j: (i, 0)),
                  pl.BlockSpec((1, d), lambda i, j: (0, 0)),
                  pl.BlockSpec((1, d, tf), lambda i, j: (layer, 0, j)),
                  pl.BlockSpec((1, d, tf), lambda i, j: (layer, 0, j))],
        out_specs=pl.BlockSpec((tm, tf), lambda i, j: (i, j)),
        out_shape=jax.ShapeDtypeStruct((m, f), BF16),
        scratch_shapes=[pltpu.VMEM((tm, d), BF16)],
        name="ffn_up",
    )(x, g.reshape(1, d), wg, wu)


def _mm_res_kernel(*refs, n_lhs):
    a_refs = refs[:n_lhs]
    w_refs = refs[n_lhs:2 * n_lhs]
    x_ref, o_ref = refs[2 * n_lhs], refs[2 * n_lhs + 1]
    acc = x_ref[...]
    for a_ref, w_ref in zip(a_refs, w_refs):
        acc = acc + jnp.dot(a_ref[...], w_ref[0].astype(BF16), preferred_element_type=F32)
    o_ref[...] = acc


def mm_res(lhs, w, layer, x, *, tm_target=1056, tn=512):
    m, n = x.shape
    kk = lhs[0].shape[1]
    assert all(a.shape == (m, kk) for a in lhs) and w.shape[1] == kk * len(lhs)
    tm = _tile(m, tm_target, 16)
    tn = min(tn, n)
    nl = len(lhs)
    in_specs = [pl.BlockSpec((tm, kk), lambda i, j: (i, 0)) for _ in lhs]
    in_specs += [pl.BlockSpec((1, kk, tn), lambda i, j, r=r: (layer, r, j)) for r in range(nl)]
    in_specs += [pl.BlockSpec((tm, tn), lambda i, j: (i, j))]
    return pl.pallas_call(
        functools.partial(_mm_res_kernel, n_lhs=nl),
        grid=(m // tm, n // tn),
        in_specs=in_specs,
        out_specs=pl.BlockSpec((tm, tn), lambda i, j: (i, j)),
        out_shape=jax.ShapeDtypeStruct((m, n), F32),
        name="mm_res",
    )(*lhs, *([w] * nl), x)


def _rms_kernel(x_ref, g_ref, o_ref):
    o_ref[...] = _rms_rows(x_ref[...], g_ref[...])


def rms_rows(x, g, row0, rows, *, tm_target=512):
    d = x.shape[1]
    tm = _tile(rows, tm_target, 8)
    assert row0 % tm == 0
    off = row0 // tm
    return pl.pallas_call(
        _rms_kernel,
        grid=(rows // tm,),
        in_specs=[pl.BlockSpec((tm, d), lambda i: (off + i, 0)),
                  pl.BlockSpec((1, d), lambda i: (0, 0))],
        out_specs=pl.BlockSpec((tm, d), lambda i: (i, 0)),
        out_shape=jax.ShapeDtypeStruct((rows, d), F32),
        name="final_norm",
    )(x, g.reshape(1, d))


def _conv_rows(xp_ref, x, w, first, halo8, taps):
    t = x.shape[0]

    @pl.when(first)
    def _():
        xp_ref[0:SUBLANE, :] = halo8

    xp_ref[SUBLANE:SUBLANE + t, :] = x
    y = x * w[taps - 1:taps, :]
    for i in range(taps - 1):
        sh = taps - 1 - i
        y = y + xp_ref[SUBLANE - sh:SUBLANE - sh + t, :] * w[i:i + 1, :]
    xp_ref[0:SUBLANE, :] = xp_ref[t:t + SUBLANE, :]
    return y


def _split_bf16(a):
    hi = a.astype(BF16)
    return hi, (a - hi.astype(F32)).astype(BF16)


def _dot3(a, b):
    ah, al = _split_bf16(a)
    bh, bl = _split_bf16(b)
    return (jnp.dot(ah, bh, preferred_element_type=F32)
            + (jnp.dot(ah, bl, preferred_element_type=F32) + jnp.dot(al, bh, preferred_element_type=F32)))


def _dot_mask(a, mask):
    a1 = a.astype(BF16)
    r1 = a - a1.astype(F32)
    a2 = r1.astype(BF16)
    a3 = (r1 - a2.astype(F32)).astype(BF16)
    return (jnp.dot(a1, mask, preferred_element_type=F32)
            + (jnp.dot(a2, mask, preferred_element_type=F32) + jnp.dot(a3, mask, preferred_element_type=F32)))


def _unit_lower_inverse(mats, top):
    c = mats[0].shape[0]
    ri = lax.broadcasted_iota(jnp.int32, (c, c), 0)
    ci = lax.broadcasted_iota(jnp.int32, (c, c), 1)
    eye = (ri == ci).astype(F32)
    s = min(INV_BASE, top)
    ps = [jnp.where((ri // s) == (ci // s), -a, 0.0) for a in mats]
    ts = [eye + p for p in ps]
    k = 2
    while k < s:
        ps = [_dot(p, p) for p in ps]
        ts = [t + _dot(t, p) for t, p in zip(ts, ps)]
        k *= 2
    while s < top:
        blk = ((ri // (2 * s)) == (ci // (2 * s))) & ((ri // s) != (ci // s))
        tos = [_dot(t, jnp.where(blk, a, 0.0)) for t, a in zip(ts, mats)]
        ts = [t - _dot(to, t) for t, to in zip(ts, tos)]
        s *= 2
    return ts


def _delta_kernel(q_ref, k_ref, v_ref, gt_ref, ba_ref, cq_ref, ck_ref, cv_ref, wq_ref, wk_ref, wv_ref,
                  s0_ref, alog_ref, dt_ref, on_ref, o_ref, sn_ref, xq_ref, xk_ref, xv_ref, s_ref,
                  *, seg, nseg, hb, dk, chunk, taps, n_heads):
    hblk = pl.program_id(1)
    c = pl.program_id(2)
    rows = seg * nseg
    cc = chunk
    wid = hb * dk

    @pl.when(c == 0)
    def _():
        s_ref[...] = s0_ref[...]
        xq_ref[:, 0:SUBLANE, :] = cq_ref[...]
        xk_ref[:, 0:SUBLANE, :] = ck_ref[...]
        xv_ref[:, 0:SUBLANE, :] = cv_ref[...]

    def conv(x_ref, xp_ref, w_ref):
        w = w_ref[...]
        x3 = x_ref[...].reshape(nseg, seg, wid)
        xp_ref[:, SUBLANE:SUBLANE + seg, :] = x3
        y = x3 * w[taps - 1:taps, :]
        for i in range(taps - 1):
            sh = taps - 1 - i
            y = y + xp_ref[:, SUBLANE - sh:SUBLANE - sh + seg, :] * w[i:i + 1, :]
        xp_ref[:, 0:SUBLANE, :] = xp_ref[:, seg:seg + SUBLANE, :]
        return _silu(y).reshape(rows, wid)

    yq_all = conv(q_ref, xq_ref, wq_ref)
    yk_all = conv(k_ref, xk_ref, wk_ref)
    yv_all = conv(v_ref, xv_ref, wv_ref)

    ba = ba_ref[...]
    lane = lax.broadcasted_iota(jnp.int32, ba.shape, 1)
    sig = _sigmoid(ba)
    gfull = -jnp.exp(alog_ref[...]) * _softplus(ba + dt_ref[...])
    gate = gt_ref[...]

    ri = lax.broadcasted_iota(jnp.int32, (cc, cc), 0)
    ci = lax.broadcasted_iota(jnp.int32, (cc, cc), 1)
    same = (ri // seg) == (ci // seg)
    incl = same & (ci <= ri)
    strict = same & (ci < ri)
    seg_last = (ri // seg) * seg + (seg - 1)
    row_in_seg = ri % seg
    pad = cc - rows

    heads = range(hb)

    def prep(hh):
        h = hblk * hb + hh
        sl = slice(hh * dk, (hh + 1) * dk)
        yq, yk, v = yq_all[:, sl], yk_all[:, sl], yv_all[:, sl]
        q = yq * lax.rsqrt(jnp.sum(yq * yq, axis=-1, keepdims=True) + EPS) * (dk ** -0.5)
        k = yk * lax.rsqrt(jnp.sum(yk * yk, axis=-1, keepdims=True) + EPS)
        beta = jnp.sum(jnp.where(lane == h, sig, 0.0), axis=1, keepdims=True)
        g = jnp.sum(jnp.where(lane == n_heads + h, gfull, 0.0), axis=1, keepdims=True)
        if pad:
            zrow = jnp.zeros((pad, dk), F32)
            q = jnp.concatenate([q, zrow], axis=0)
            k = jnp.concatenate([k, zrow], axis=0)
            v = jnp.concatenate([v, zrow], axis=0)
            beta = jnp.concatenate([beta, jnp.zeros((pad, 1), F32)], axis=0)
            g = jnp.concatenate([g, jnp.zeros((pad, 1), F32)], axis=0)
        gcb = jnp.broadcast_to(g, (cc, cc))
        s = 1
        while s < seg:
            gcb = gcb + jnp.where(row_in_seg >= s, pltpu.roll(gcb, s, 0), 0.0)
            s *= 2
        grow = jnp.sum(jnp.where(ri == ci, gcb, 0.0), axis=0, keepdims=True)
        dec = jnp.where(incl, jnp.exp(jnp.where(incl, gcb - grow, 0.0)), 0.0)
        g_col = gcb[:, 0:1]
        g_end = jnp.sum(jnp.where(ci == seg_last, grow, 0.0), axis=1, keepdims=True)
        return dict(q=q, k=k, v=v, beta=beta, dec=dec, g_col=g_col, g_end=g_end, gamma=jnp.exp(g_col))

    hd_ = [prep(hh) for hh in heads]
    kks = [_dot_nt(p['k'], p['k']) for p in hd_]
    t_invs = _unit_lower_inverse([jnp.where(strict, p['beta'] * kk * p['dec'], 0.0) for p, kk in zip(hd_, kks)], seg)
    sols = [_dot3(t, jnp.concatenate([(p['beta'] * p['gamma']) * p['k'], p['beta'] * p['v']], axis=1))
            for p, t in zip(hd_, t_invs)]
    attns = [_dot_nt(p['q'], p['k']) * p['dec'] for p in hd_]
    kdts = [(p['k'] * jnp.exp(p['g_end'] - p['g_col'])).T for p in hd_]

    us, oqs = [], []
    for hh in heads:
        w_c, u_c = sols[hh][:, :dk], sols[hh][:, dk:]
        q_dec = hd_[hh]['q'] * hd_[hh]['gamma']
        ul, ol = [], []
        for i in range(nseg):
            r = slice(i * seg, (i + 1) * seg)
            st = s_ref[i, hh]
            ul.append(u_c[r] - _dot(w_c[r], st))
            ol.append(_dot(q_dec[r], st))
        if pad:
            ul.append(jnp.zeros((pad, dk), F32))
            ol.append(jnp.zeros((pad, dk), F32))
        us.append(ul[0] if len(ul) == 1 else jnp.concatenate(ul, axis=0))
        oqs.append(ol[0] if len(ol) == 1 else jnp.concatenate(ol, axis=0))
    os_ = [oq + _dot(attn, u) for oq, attn, u in zip(oqs, attns, us)]
    for hh in heads:
        g_end = hd_[hh]['g_end']
        for i in range(nseg):
            kd_i = kdts[hh] if cc == seg else jnp.where((ci // seg) == i, kdts[hh], 0.0)
            s_ref[i, hh] = jnp.exp(g_end[i * seg:i * seg + 1, :]) * s_ref[i, hh] + _dot(kd_i, us[hh])
    for hh in heads:
        sl = slice(hh * dk, (hh + 1) * dk)
        o = os_[hh][:rows]
        o = o * lax.rsqrt(jnp.mean(o * o, axis=-1, keepdims=True) + EPS) * on_ref[...]
        o_ref[:, sl] = (o * _silu(gate[:, sl])).astype(o_ref.dtype)

    @pl.when(c == pl.num_programs(2) - 1)
    def _():
        sn_ref[...] = s_ref[...]


def delta_heads(z, row0, nb, t, conv8, s0, conv_w, alog_row, dt_row, out_norm, *, n_heads, dk, cols, hb, dest=None):
    cq, ck, cv, cg, cba = cols
    taps = conv_w.shape[0]
    chunk = DELTA_CHUNK
    assert dk == chunk == LANE and t % SUBLANE == 0 and n_heads % hb == 0
    if t >= chunk:
        assert t % chunk == 0
        seg, nseg = chunk, 1
    else:
        assert chunk % t == 0
        seg, nseg = t, _tile(nb, chunk // t, 1)
    rows = seg * nseg
    nc = t // seg
    assert row0 % rows == 0 and all(cb % hb == 0 for cb in (cq, ck, cv, cg))
    r0 = row0 // rows
    wid = hb * dk

    def zspec(cb):
        return pl.BlockSpec((rows, wid), lambda b, h, c: (r0 + b * nc + c, cb // hb + h))

    def cspec(cb):
        return pl.BlockSpec((nseg, SUBLANE, wid), lambda b, h, c: (b, 0, cb // hb + h))

    def wspec(cb):
        return pl.BlockSpec((taps, wid), lambda b, h, c: (0, cb // hb + h))

    row = pl.BlockSpec((1, LANE), lambda b, h, c: (0, 0))
    sspec = pl.BlockSpec((nseg, hb, dk, dk), lambda b, h, c: (b, h, 0, 0))
    kern = functools.partial(_delta_kernel, seg=seg, nseg=nseg, hb=hb, dk=dk, chunk=chunk, taps=taps,
                             n_heads=n_heads)
    total, ob, kern, pre_specs, pre_ops, alias = _row_dest(dest, nb * t, rows, kern, n_heads * dk)
    return pl.pallas_call(
        kern,
        grid=(nb // nseg, n_heads // hb, nc),
        in_specs=pre_specs + [zspec(cq), zspec(ck), zspec(cv), zspec(cg),
                              pl.BlockSpec((rows, LANE), lambda b, h, c: (r0 + b * nc + c, cba)),
                              cspec(cq), cspec(ck), cspec(cv), wspec(cq), wspec(ck), wspec(cv),
                              sspec, row, row, row],
        out_specs=[pl.BlockSpec((rows, wid), lambda b, h, c: (ob + b * nc + c, h)), sspec],
        out_shape=[jax.ShapeDtypeStruct((total, n_heads * dk), BF16),
                   jax.ShapeDtypeStruct((nb, n_heads, dk, dk), F32)],
        scratch_shapes=[pltpu.VMEM((nseg, SUBLANE + seg, wid), F32)] * 3 + [pltpu.VMEM((nseg, hb, dk, dk), F32)],
        name="delta_heads",
        **alias,
    )(*pre_ops, z, z, z, z, z, conv8, conv8, conv8, conv_w, conv_w, conv_w, s0, alog_row, dt_row, out_norm)


def _pool_kernel(u_ref, st_ref, w_ref, sc_ref, o_ref, xp_ref, *, tt, pos0, halo, gw):
    tb = pl.program_id(1)

    @pl.when(tb == 0)
    def _():
        xp_ref[0:halo, :] = st_ref[0]

    u = u_ref[...]
    xp_ref[halo:halo + tt, :] = u
    pos = pos0 + tb * tt + lax.broadcasted_iota(jnp.int32, (tt, 1), 0)
    for gi, wsz in enumerate(POOL_WINDOWS):
        lo, hi = gi * gw, (gi + 1) * gw
        acc = u[:, lo:hi]
        for i in range(1, wsz):
            acc = acc + xp_ref[halo - i:halo - i + tt, lo:hi]
        cnt = jnp.minimum(wsz, pos + 1).astype(F32)
        d = acc * (1.0 / cnt) - u[:, lo:hi]
        y = _dot(d, w_ref[gi]) * sc_ref[:, lo:hi]
        o_ref[:, lo:hi] = y.astype(o_ref.dtype)
    xp_ref[0:halo, :] = xp_ref[tt:tt + halo, :]


def pool_groups(z, row0, nb, t, state16, pool_w, pool_scale, *, col_block, pos0, tt_target=256, dest=None):
    db = pool_scale.shape[-1]
    halo = state16.shape[1]
    tt = _tile(t, tt_target, SUBLANE)
    assert row0 % tt == 0
    nt, r0 = t // tt, row0 // tt
    gw = db // len(POOL_WINDOWS)
    kern = functools.partial(_pool_kernel, tt=tt, pos0=pos0, halo=halo, gw=gw)
    total, ob, kern, pre_specs, pre_ops, alias = _row_dest(dest, nb * t, tt, kern, db)
    return pl.pallas_call(
        kern,
        grid=(nb, nt),
        in_specs=pre_specs + [pl.BlockSpec((tt, db), lambda b, i: (r0 + b * nt + i, col_block)),
                              pl.BlockSpec((1, halo, db), lambda b, i: (b, 0, 0)),
                              pl.BlockSpec(pool_w.shape, lambda b, i: (0, 0, 0)),
                              pl.BlockSpec((1, db), lambda b, i: (0, 0))],
        out_specs=pl.BlockSpec((tt, db), lambda b, i: (ob + b * nt + i, 0)),
        out_shape=jax.ShapeDtypeStruct((total, db), BF16),
        scratch_shapes=[pltpu.VMEM((halo + tt, db), F32)],
        name="pool_groups",
        **alias,
    )(*pre_ops, z, state16, pool_w, pool_scale.reshape(1, db))


def _rglru_kernel(xc_ref, gc_ref, c8_ref, cw_ref, cb_ref, wa_ref, wx_ref, ba_ref, bx_ref, lam_ref, h0_ref,
                  y_ref, hl_ref, xp_ref, h_ref, *, taps, n_blocks, bs):
    tb = pl.program_id(1)
    first = tb == 0

    @pl.when(first)
    def _():
        h_ref[...] = h0_ref[0]

    x = _conv_rows(xp_ref, xc_ref[...], cw_ref[...], first, c8_ref[0], taps) + cb_ref[...]
    tt = x.shape[0]
    rs, is_ = [], []
    for n in range(n_blocks):
        xb = x[:, n * bs:(n + 1) * bs]
        rs.append(_dot(xb, wa_ref[n]))
        is_.append(_dot(xb, wx_ref[n]))
    r = _sigmoid(jnp.concatenate(rs, axis=1) + ba_ref[...])
    i = _sigmoid(jnp.concatenate(is_, axis=1) + bx_ref[...])
    log_a = -LRU_C * r * _softplus(-lam_ref[...])
    a = jnp.exp(log_a)
    b = jnp.sqrt(-_expm1(2.0 * log_a)) * (i * x)
    row = lax.broadcasted_iota(jnp.int32, a.shape, 0) % SUBLANE
    s = 1
    while s < SUBLANE:
        keep = row >= s
        b = jnp.where(keep, a * pltpu.roll(b, s, 0) + b, b)
        a = jnp.where(keep, a * pltpu.roll(a, s, 0), a)
        s *= 2
    carry = h_ref[...]
    groups = []
    for r0 in range(0, tt, SUBLANE):
        hg = b[r0:r0 + SUBLANE] + a[r0:r0 + SUBLANE] * carry
        carry = hg[SUBLANE - 1:SUBLANE, :]
        groups.append(hg)
    hs = groups[0] if len(groups) == 1 else jnp.concatenate(groups, axis=0)
    h_last = carry
    h_ref[...] = h_last
    y_ref[...] = (hs * _gelu_tanh(gc_ref[...])).astype(y_ref.dtype)

    @pl.when(tb == pl.num_programs(1) - 1)
    def _():
        hl_ref[0] = h_last


def rglru_blocks(z, row0, nb, t, conv8, h0, conv_w, conv_b, wa, ba, wx, bx, lam, *, tt_target=256, dest=None):
    dc = lam.shape[-1]
    taps = conv_w.shape[0]
    n_blocks, bs = wa.shape[0], wa.shape[1]
    tt = _tile(t, tt_target, SUBLANE)
    assert row0 % tt == 0
    nt, r0 = t // tt, row0 // tt
    vec = pl.BlockSpec((1, dc), lambda b, i: (0, 0))
    kern = functools.partial(_rglru_kernel, taps=taps, n_blocks=n_blocks, bs=bs)
    total, ob, kern, pre_specs, pre_ops, alias = _row_dest(dest, nb * t, tt, kern, dc)
    return pl.pallas_call(
        kern,
        grid=(nb, nt),
        in_specs=pre_specs + [pl.BlockSpec((tt, dc), lambda b, i: (r0 + b * nt + i, 0)),
                              pl.BlockSpec((tt, dc), lambda b, i: (r0 + b * nt + i, 1)),
                              pl.BlockSpec((1, SUBLANE, dc), lambda b, i: (b, 0, 0)),
                              pl.BlockSpec((taps, dc), lambda b, i: (0, 0)),
                              vec,
                              pl.BlockSpec(wa.shape, lambda b, i: (0, 0, 0)),
                              pl.BlockSpec(wx.shape, lambda b, i: (0, 0, 0)),
                              vec, vec, vec,
                              pl.BlockSpec((1, 1, dc), lambda b, i: (b, 0, 0))],
        out_specs=[pl.BlockSpec((tt, dc), lambda b, i: (ob + b * nt + i, 0)),
                   pl.BlockSpec((1, 1, dc), lambda b, i: (b, 0, 0))],
        out_shape=[jax.ShapeDtypeStruct((total, dc), BF16),
                   jax.ShapeDtypeStruct((nb, 1, dc), F32)],
        scratch_shapes=[pltpu.VMEM((SUBLANE + tt, dc), F32), pltpu.VMEM((1, dc), F32)],
        name="rglru_blocks",
        **alias,
    )(*pre_ops, z, z, conv8, conv_w, conv_b.reshape(1, dc), wa, wx, ba.reshape(1, dc), bx.reshape(1, dc),
      lam.reshape(1, dc), h0.reshape(nb, 1, dc))


def _fox_prep_kernel(f_ref, bf_ref, lf_ref, cum_ref, cumt_ref, *, n_heads):
    lf = -_softplus(-(f_ref[...] + bf_ref[...]))
    lf_ref[...] = lf
    cum = _cumsum_rows(lf)
    cum_ref[...] = cum
    cumt_ref[0] = cum.T[0:n_heads, :]


def fox_prep(z, nb, t, bf_row, *, col_block, n_heads):
    return pl.pallas_call(
        functools.partial(_fox_prep_kernel, n_heads=n_heads),
        grid=(nb,),
        in_specs=[pl.BlockSpec((t, LANE), lambda b: (b, col_block)),
                  pl.BlockSpec((1, LANE), lambda b: (0, 0))],
        out_specs=[pl.BlockSpec((t, LANE), lambda b: (b, 0)),
                   pl.BlockSpec((t, LANE), lambda b: (b, 0)),
                   pl.BlockSpec((1, n_heads, t), lambda b: (b, 0, 0))],
        out_shape=[jax.ShapeDtypeStruct((nb * t, LANE), F32),
                   jax.ShapeDtypeStruct((nb * t, LANE), F32),
                   jax.ShapeDtypeStruct((nb, n_heads, t), F32)],
        name="fox_prep",
    )(z, bf_row)


def _fox_flash_kernel(qi_ref, ki_ref, q_ref, k_ref, v_ref, fq_ref, fk_ref, o_ref, m_ref, l_ref, acc_ref, fqc_ref,
                      *, scale, hb, hd):
    hblk = pl.program_id(1)
    pair = pl.program_id(2)
    qi, ki = qi_ref[pair], ki_ref[pair]
    tq, tk = q_ref.shape[0], k_ref.shape[0]
    log2e = math.log2(math.e)

    @pl.when(ki == 0)
    def _():
        m_ref[...] = jnp.full(m_ref.shape, NEG, F32)
        l_ref[...] = jnp.zeros(l_ref.shape, F32)
        acc_ref[...] = jnp.zeros(acc_ref.shape, F32)
        fq = fq_ref[...] * log2e
        lane = lax.broadcasted_iota(jnp.int32, fq.shape, 1)
        for hh in range(hb):
            col = jnp.sum(jnp.where(lane == hblk * hb + hh, fq, 0.0), axis=1, keepdims=True)
            fqc_ref[hh] = jnp.broadcast_to(col, fqc_ref.shape[1:])

    def attend(diagonal):
        heads = range(hb)
        kb = k_ref[...].astype(BF16)
        vb = v_ref[...].astype(BF16)
        fks = [fk_ref[0, pl.ds(hblk * hb + hh, 1), :] * log2e for hh in heads]
        band = FLASH_ROW_BAND
        chains = [(r0, hh) for r0 in range(0, tq, band) for hh in heads]
        groups = [chains[c0:c0 + FLASH_CHAINS] for c0 in range(0, len(chains), FLASH_CHAINS)]

        def score_tiles(group):
            ts, nks = [], []
            for r0, hh in group:
                sl = slice(hh * hd, (hh + 1) * hd)
                nk = min(tk, -(-(r0 + band) // LANE) * LANE) if diagonal else tk
                qb = q_ref[r0:r0 + band, sl].astype(BF16)
                t2 = _dot_nt(qb, kb[:nk, sl]) * (scale * log2e) - fks[hh][:, :nk]
                if diagonal:
                    ri = lax.broadcasted_iota(jnp.int32, (band, nk), 0) + r0
                    ci = lax.broadcasted_iota(jnp.int32, (band, nk), 1)
                    t2 = jnp.where(ci <= ri, t2, NEG)
                ts.append(t2)
                nks.append(nk)
            return ts, nks

        pending = score_tiles(groups[0])
        for gi, group in enumerate(groups):
            ts, nks = pending
            if gi + 1 < len(groups):
                pending = score_tiles(groups[gi + 1])
            alphas, ps, m_news = [], [], []
            for (r0, hh), t2, nk in zip(group, ts, nks):
                rs = slice(r0, r0 + band)
                fq = fqc_ref[hh, rs, :]
                m_old = m_ref[hh, rs, :]
                m_new = jnp.maximum(m_old, jnp.max(t2, axis=1, keepdims=True) + fq)
                shift = fq - m_new
                alphas.append(jnp.exp2(m_old - m_new))
                ps.append([jnp.exp2(t2[:, j:j + LANE] + shift) for j in range(0, nk, LANE)])
                m_news.append(m_new)
            for (r0, hh), p, alpha, m_new, nk in zip(group, ps, alphas, m_news, nks):
                rs = slice(r0, r0 + band)
                sl = slice(hh * hd, (hh + 1) * hd)
                psum = p[0]
                for pj in p[1:]:
                    psum = psum + pj
                l_ref[hh, rs, :] = alpha * l_ref[hh, rs, :] + jnp.sum(psum, axis=1, keepdims=True)
                pb = jnp.concatenate([pj.astype(BF16) for pj in p], axis=1) if len(p) > 1 else p[0].astype(BF16)
                acc_ref[rs, sl] = alpha * acc_ref[rs, sl] + jnp.dot(pb, vb[:nk, sl], preferred_element_type=F32)
                m_ref[hh, rs, :] = m_new

    @pl.when(ki < qi)
    def _():
        attend(False)

    @pl.when(ki == qi)
    def _():
        attend(True)
        for hh in range(hb):
            sl = slice(hh * hd, (hh + 1) * hd)
            o_ref[:, sl] = (acc_ref[:, sl] / l_ref[hh]).astype(o_ref.dtype)


def fox_prompt(z, cum, cumt, nb, t, *, n_heads, hd, cols, tq_target=2048, hb=2, dest=None):
    cq, ck, cv = cols
    tq = _tile(t, tq_target, LANE)
    nq = t // tq
    assert n_heads % hb == 0 and all(cb % hb == 0 for cb in cols)
    pairs = [(i, j) for i in range(nq) for j in range(i + 1)]
    qi_tab = jnp.asarray([p[0] for p in pairs], jnp.int32)
    ki_tab = jnp.asarray([p[1] for p in pairs], jnp.int32)
    wid = hb * hd
    kern = functools.partial(_fox_flash_kernel, scale=hd ** -0.5, hb=hb, hd=hd)
    total, ob, kern, pre_specs, pre_ops, alias = _row_dest(dest, nb * t, tq, kern, n_heads * hd, n_prefetch=2)

    def kv(cb):
        return pl.BlockSpec((tq, wid), lambda b, h, p, qt, kt: (b * nq + kt[p], cb // hb + h))

    grid_spec = pltpu.PrefetchScalarGridSpec(
        num_scalar_prefetch=2,
        grid=(nb, n_heads // hb, len(pairs)),
        in_specs=pre_specs + [pl.BlockSpec((tq, wid), lambda b, h, p, qt, kt: (b * nq + qt[p], cq // hb + h)),
                              kv(ck), kv(cv),
                              pl.BlockSpec((tq, LANE), lambda b, h, p, qt, kt: (b * nq + qt[p], 0)),
                              pl.BlockSpec((1, n_heads, tq), lambda b, h, p, qt, kt: (b, 0, kt[p]))],
        out_specs=pl.BlockSpec((tq, wid), lambda b, h, p, qt, kt: (ob + b * nq + qt[p], h)),
        scratch_shapes=[pltpu.VMEM((hb, tq, LANE), F32), pltpu.VMEM((hb, tq, LANE), F32), pltpu.VMEM((tq, wid), F32),
                        pltpu.VMEM((hb, tq, LANE), F32)],
    )
    assert hd == LANE
    return pl.pallas_call(
        kern,
        grid_spec=grid_spec,
        out_shape=jax.ShapeDtypeStruct((total, n_heads * hd), BF16),
        name="fox_prompt",
        **alias,
    )(qi_tab, ki_tab, *pre_ops, z, z, z, cum, cumt)


def _fox_sample_kernel(pt_ref, q_ref, kn_ref, vn_ref, f_ref, bf_ref, *rest, n_heads, hd, pages_per_step, scale):
    g_pages = pages_per_step
    k_refs = rest[0:g_pages]
    v_refs = rest[g_pages:2 * g_pages]
    lf_refs = rest[2 * g_pages:3 * g_pages]
    o_ref, lfo_ref, m_ref, l_ref, acc_ref, carry_ref, qall_ref, b0_ref = rest[3 * g_pages:]
    step = pl.program_id(1)
    t = q_ref.shape[0]
    page = lf_refs[0].shape[3]
    wide = page * n_heads
    rows = n_heads * t
    log2e = math.log2(math.e)

    @pl.when(step == 0)
    def _():
        lf = -_softplus(-(f_ref[...] + bf_ref[...]))
        lfo_ref[...] = lf
        fn = _cumsum_rows(lf) * log2e
        carry_ref[...] = jnp.zeros(carry_ref.shape, F32)
        q = q_ref[...].astype(BF16)
        kn = jnp.concatenate([kn_ref[...], jnp.zeros((page - t, n_heads * hd), F32)], axis=0)
        vn = jnp.concatenate([vn_ref[...], jnp.zeros((page - t, n_heads * hd), F32)], axis=0)
        ri = lax.broadcasted_iota(jnp.int32, (t, page), 0)
        ci = lax.broadcasted_iota(jnp.int32, (t, page), 1)
        lane_head = lax.broadcasted_iota(jnp.int32, (t, wide), 1) % n_heads
        for h in range(n_heads):
            rs = slice(h * t, (h + 1) * t)
            sl = slice(h * hd, (h + 1) * hd)
            col = lf[:, h:h + 1] * log2e
            fnrow = jnp.sum(jnp.where(ri <= ci, col, 0.0), axis=0, keepdims=True)
            fnc = fn[:, h:h + 1]
            s2 = _dot_nt(q[:, sl], kn[:, sl]) * (scale * log2e) + fnc - fnrow
            s2 = jnp.where(ci <= ri, s2, NEG)
            m0 = jnp.max(s2, axis=1, keepdims=True)
            p = jnp.exp2(s2 - m0)
            m_ref[rs, :] = jnp.broadcast_to(m0, (t, LANE))
            l_ref[rs, :] = jnp.broadcast_to(jnp.sum(p, axis=1, keepdims=True), (t, LANE))
            acc_ref[rs, :] = _dot(p, vn[:, sl])
            qall_ref[rs, :] = q[:, sl]
            b0_ref[rs, :] = jnp.where(lane_head == h, fnc, NEG)

    lft = jnp.concatenate([lf_refs[g][0, 0] for g in range(g_pages)], axis=0) * log2e
    ri = lax.broadcasted_iota(jnp.int32, (page, wide), 0)
    ci = lax.broadcasted_iota(jnp.int32, (page, wide), 1)
    rr = _dot_mask(lft, (ri > ci // n_heads).astype(BF16))
    tot = jnp.sum(lft, axis=1, keepdims=True)
    run = carry_ref[:, 0:1]
    rc = [None] * g_pages
    for g in reversed(range(g_pages)):
        rc[g] = rr[g * n_heads:(g + 1) * n_heads] + run
        run = run + tot[g * n_heads:(g + 1) * n_heads]
    carry_ref[...] = jnp.broadcast_to(run, carry_ref.shape)

    qall = qall_ref[...]
    tiles = range(0, wide, LANE)
    for g0 in range(0, g_pages, SAMPLE_PAGE_GROUP):
        group = range(g0, min(g0 + SAMPLE_PAGE_GROUP, g_pages))
        scores = []
        for g in group:
            bias = b0_ref[...] + jnp.broadcast_to(rc[g][:, None, :], (n_heads, t, wide)).reshape(rows, wide)
            scores.append(_dot_nt(qall, k_refs[g][0, 0]) * (scale * log2e) + bias)
        smax = scores[0]
        for s2 in scores[1:]:
            smax = jnp.maximum(smax, s2)
        m_old = m_ref[...]
        m_new = jnp.maximum(m_old, jnp.max(smax, axis=1, keepdims=True))
        alpha = jnp.exp2(m_old - m_new)
        psum, pv = None, None
        for g, s2 in zip(group, scores):
            p = [jnp.exp2(s2[:, j:j + LANE] - m_new) for j in tiles]
            for pj in p:
                psum = pj if psum is None else psum + pj
            pb = jnp.concatenate([pj.astype(BF16) for pj in p], axis=1)
            d = jnp.dot(pb, v_refs[g][0, 0].astype(BF16), preferred_element_type=F32)
            pv = d if pv is None else pv + d
        l_ref[...] = alpha * l_ref[...] + jnp.sum(psum, axis=1, keepdims=True)
        acc_ref[...] = alpha * acc_ref[...] + pv
        m_ref[...] = m_new

    @pl.when(step == pl.num_programs(1) - 1)
    def _():
        o = acc_ref[...] / l_ref[...]
        for h in range(n_heads):
            o_ref[:, h * hd:(h + 1) * hd] = o[h * t:(h + 1) * t, :].astype(o_ref.dtype)


def fox_sample(z, row0, nb, t, page_table, kc, vc, lfc, layer, bf_row, *, n_heads, hd, cols, pages_per_step=16, dest=None):
    cq, ck, cv, cf = cols
    n_pages = page_table.shape[1]
    page = lfc.shape[3]
    g_pages = pages_per_step
    while n_pages % g_pages:
        g_pages //= 2
    ns = n_pages // g_pages
    dd = n_heads * hd
    assert row0 % t == 0 and t % SUBLANE == 0
    r0 = row0 // t

    def zspec(cb, w):
        return pl.BlockSpec((t, w), lambda b, s, pt: (r0 + b, cb))

    def pspec(g, r, w):
        return pl.BlockSpec((1, 1, r, w), lambda b, s, pt, g=g: (layer, pt[b, (ns - 1 - s) * g_pages + g], 0, 0))

    kern = functools.partial(_fox_sample_kernel, n_heads=n_heads, hd=hd, pages_per_step=g_pages, scale=hd ** -0.5)
    total, ob, kern, pre_specs, pre_ops, alias = _row_dest(dest, nb * t, t, kern, dd, n_prefetch=1)
    grid_spec = pltpu.PrefetchScalarGridSpec(
        num_scalar_prefetch=1,
        grid=(nb, ns),
        in_specs=pre_specs + [zspec(cq, dd), zspec(ck, dd), zspec(cv, dd), zspec(cf, LANE),
                              pl.BlockSpec((1, LANE), lambda b, s, pt: (0, 0))]
        + [pspec(g, page * n_heads, hd) for g in range(g_pages)] * 2
        + [pspec(g, n_heads, page) for g in range(g_pages)],
        out_specs=[pl.BlockSpec((t, dd), lambda b, s, pt: (ob + b, 0)),
                   pl.BlockSpec((t, LANE), lambda b, s, pt: (b, 0))],
        scratch_shapes=[pltpu.VMEM((n_heads * t, LANE), F32), pltpu.VMEM((n_heads * t, LANE), F32),
                        pltpu.VMEM((n_heads * t, hd), F32), pltpu.VMEM((n_heads, LANE), F32),
                        pltpu.VMEM((n_heads * t, hd), BF16), pltpu.VMEM((n_heads * t, page * n_heads), F32)],
    )
    assert hd == LANE
    return pl.pallas_call(
        kern,
        grid_spec=grid_spec,
        out_shape=[jax.ShapeDtypeStruct((total, dd), BF16), jax.ShapeDtypeStruct((nb * t, LANE), F32)],
        name="fox_sample",
        **alias,
    )(page_table, *pre_ops, z, z, z, z, bf_row, *([kc] * g_pages), *([vc] * g_pages), *([lfc] * g_pages))


def _xattn_kernel(q_ref, k_ref, v_ref, o_ref, *, n_heads, hd, interleaved, scale):
    q = q_ref[...]
    for h in range(n_heads):
        sl = slice(h * hd, (h + 1) * hd)
        if interleaved:
            n_mem = k_ref.shape[2] // n_heads
            kh = k_ref[0, 0, pl.ds(h, n_mem, stride=n_heads), :]
            vh = v_ref[0, 0, pl.ds(h, n_mem, stride=n_heads), :]
        else:
            kh, vh = k_ref[:, sl], v_ref[:, sl]
        s = _dot_nt(q[:, sl], kh) * scale
        p = jnp.exp(s - jnp.max(s, axis=1, keepdims=True))
        o = _dot(p, vh) / jnp.sum(p, axis=1, keepdims=True)
        o_ref[:, sl] = o.astype(o_ref.dtype)


def xattn(qx, row0, nb, t, mk, mv, kspec, vspec, *, n_heads, hd, interleaved, tt_target=512, dest=None):
    dx = n_heads * hd
    tt = _tile(t, tt_target, SUBLANE)
    assert row0 % tt == 0
    nt, r0 = t // tt, row0 // tt
    kern = functools.partial(_xattn_kernel, n_heads=n_heads, hd=hd, interleaved=interleaved, scale=hd ** -0.5)
    total, ob, kern, pre_specs, pre_ops, alias = _row_dest(dest, nb * t, tt, kern, dx)
    return pl.pallas_call(
        kern,
        grid=(nb, nt),
        in_specs=pre_specs + [pl.BlockSpec((tt, dx), lambda b, i: (r0 + b * nt + i, 0)), kspec, vspec],
        out_specs=pl.BlockSpec((tt, dx), lambda b, i: (ob + b * nt + i, 0)),
        out_shape=jax.ShapeDtypeStruct((total, dx), BF16),
        name="xattn",
        **alias,
    )(*pre_ops, qx, mk, mv)


def _pad_cols(w, n):
    return jnp.pad(w, ((0, 0), (0, n - w.shape[1])))


def _halo_rows(buf, rows):
    return jnp.pad(buf, ((0, 0), (rows - buf.shape[1], 0), (0, 0)))


def kernel(x_prompt, x_sample, state_delta, state_conv_a, state_pool, state_lru, state_conv_c, cache_fox_k, cache_fox_v, cache_fox_logf, cache_mem_k, cache_mem_v, page_table, mem_prompt, ffn1_wg, ffn1_wu, ffn1_wd, ffn2_wg, ffn2_wu, ffn2_wd, norm_ffn1, norm_mix, norm_xattn, norm_ffn2, norm_mem, final_norm, xattn_wq, xattn_wk, xattn_wv, xattn_wo, even_w_in, even_w_out, delta_conv_w, delta_a_log, delta_dt_bias, delta_out_norm, pool_w, pool_scale, odd_w_in, odd_w_out, lru_conv_w, lru_conv_b, lru_wa, lru_ba, lru_wx, lru_bx, lru_lambda, fox_bf):
    bp, tp, d = x_prompt.shape
    bs, ts, _ = x_sample.shape
    mp, ms = bp * tp, bs * ts
    depth = ffn1_wg.shape[0]
    h_a, dk_a = state_delta.shape[2], state_delta.shape[3]
    d_a = h_a * dk_a
    d_b = state_pool.shape[-1]
    d_c = state_lru.shape[-1]
    h_d, hd_d = cache_fox_k.shape[3], cache_fox_k.shape[4]
    d_d = h_d * hd_d
    n_mem, h_x, hd_x = cache_mem_k.shape[2], cache_mem_k.shape[3], cache_mem_k.shape[4]
    d_x = h_x * hd_x
    page = cache_fox_k.shape[2]
    pos0_s = page_table.shape[1] * page
    assert d_a == d_b and d_c == d_d and 2 * h_a <= LANE and h_d <= LANE
    tn_z = 768
    nz = -(-(max(4 * d_a + d_b, 2 * d_c + 3 * d_d) + LANE) // tn_z) * tn_z
    bc =(4 * d_a + d_b) // LANE
    assert bc == (2 * d_c + 3 * d_d) // LANE

    x = jnp.concatenate([x_prompt.reshape(mp, d), x_sample.reshape(ms, d)], axis=0)

    mem2 = mem_prompt.reshape(bp * n_mem, d)
    mem_k = [norm_mm(mem2, norm_mem[l], xattn_wk, l, tn=d_x) for l in range(depth)]
    mem_v = [norm_mm(mem2, norm_mem[l], xattn_wv, l, tn=d_x) for l in range(depth)]
    kc4 = cache_fox_k.reshape(cache_fox_k.shape[0], cache_fox_k.shape[1], page * h_d, hd_d)
    vc4 = cache_fox_v.reshape(kc4.shape)
    lfc4 = jnp.swapaxes(cache_fox_logf, 2, 3)
    mk4 = cache_mem_k.reshape(depth, bs, n_mem * h_x, hd_x)
    mv4 = cache_mem_v.reshape(depth, bs, n_mem * h_x, hd_x)

    new_p = {n: [] for n in ('delta', 'conv_a', 'pool', 'lru', 'conv_c', 'fox_k', 'fox_v', 'fox_logf')}
    new_s = {n: [] for n in new_p}

    def seq_cols(zz, row0, nb, t, c0, c1, last=None):
        if last is None or last >= t:
            return lax.slice(zz, (row0, c0), (row0 + nb * t, c1)).reshape(nb, t, c1 - c0)
        idx = (row0 + jnp.arange(nb)[:, None] * t + (t - last) + jnp.arange(last)[None, :]).reshape(-1)
        return lax.slice_in_dim(zz[idx], c0, c1, axis=1).reshape(nb, last, c1 - c0)

    def new_buf(buf, zz, row0, nb, t, c0, c1):
        r = buf.shape[1]
        if t >= r:
            return seq_cols(zz, row0, nb, t, c0, c1, last=r)
        return jnp.concatenate([buf[:, t:], seq_cols(zz, row0, nb, t, c0, c1)], axis=1)

    for l in range(depth):
        j = l // 2
        a = ffn_up(x, norm_ffn1[l], ffn1_wg, ffn1_wu, l)
        x = mm_res([a], ffn1_wd, l, x, tm_target=1056, tn=FFN_DOWN_TN)

        if l % 2 == 0:
            w = even_w_in[j]
            w_in = _pad_cols(jnp.concatenate([w[:, :4 * d_a], w[:, 4 * d_a + 2 * h_a:], w[:, 4 * d_a:4 * d_a + 2 * h_a]],
                                             axis=1), nz).astype(BF16)
            z = norm_mm(x, norm_mix[l], w_in)
            hb = dk_a // LANE
            alog_row = jnp.zeros((1, LANE), F32).at[0, h_a:2 * h_a].set(delta_a_log[j])
            dt_row = jnp.zeros((1, LANE), F32).at[0, h_a:2 * h_a].set(delta_dt_bias[j])
            groups = ((0, bp, tp, jnp.zeros((bp,) + state_conv_a.shape[2:], F32),
                       jnp.zeros((bp,) + state_delta.shape[2:], F32), jnp.zeros((bp,) + state_pool.shape[2:], F32), 0, new_p),
                      (mp, bs, ts, state_conv_a[j], state_delta[j], state_pool[j], pos0_s, new_s))
            mix_a, mix_b = None, None
            for gi, (row0, nb, t, cbuf, s0, pbuf, pos0, new) in enumerate(groups):
                o, s_new = delta_heads(z, row0, nb, t, _halo_rows(cbuf, SUBLANE), s0, delta_conv_w[j], alog_row, dt_row,
                                       delta_out_norm[j].reshape(1, dk_a), n_heads=h_a, dk=dk_a,
                                       cols=(0, h_a * hb, 2 * h_a * hb, 3 * h_a * hb, bc),
                                       hb=DELTA_HEADS_PER_STEP if t >= DELTA_CHUNK else 1,
                                       dest=(mp + ms, row0, mix_a))
                yb = pool_groups(z, row0, nb, t, _halo_rows(pbuf, 2 * SUBLANE), pool_w[j].astype(BF16), pool_scale[j],
                                 col_block=4 * d_a // d_b, pos0=pos0, dest=(mp + ms, row0, mix_b))
                mix_a, mix_b = o, yb
                new['delta'].append(s_new)
                new['conv_a'].append(new_buf(cbuf, z, row0, nb, t, 0, 3 * d_a))
                new['pool'].append(new_buf(pbuf, z, row0, nb, t, 4 * d_a, 4 * d_a + d_b))
            w_out = even_w_out
        else:
            w_in = _pad_cols(odd_w_in[j], nz).astype(BF16)
            z = norm_mm(x, norm_mix[l], w_in)
            bf_row = jnp.zeros((1, LANE), F32).at[0, :h_d].set(fox_bf[j])
            qb, kb, vb = 2 * d_c // hd_d, (2 * d_c + d_d) // hd_d, (2 * d_c + 2 * d_d) // hd_d
            groups = ((0, bp, tp, jnp.zeros((bp,) + state_conv_c.shape[2:], F32), jnp.zeros((bp, d_c), F32), new_p),
                      (mp, bs, ts, state_conv_c[j], state_lru[j], new_s))
            mix_a, mix_b = None, None
            for gi, (row0, nb, t, cbuf, h0, new) in enumerate(groups):
                yc, h_last = rglru_blocks(z, row0, nb, t, _halo_rows(cbuf, SUBLANE), h0, lru_conv_w[j], lru_conv_b[j],
                                          lru_wa[j].astype(BF16), lru_ba[j], lru_wx[j].astype(BF16), lru_bx[j],
                                          lru_lambda[j], dest=(mp + ms, row0, mix_a))
                if gi == 0:
                    lf, cum, cumt = fox_prep(z, nb, t, bf_row, col_block=bc, n_heads=h_d)
                    o = fox_prompt(z, cum, cumt, nb, t, n_heads=h_d, hd=hd_d, cols=(qb, kb, vb), dest=(mp + ms, row0, mix_b))
                else:
                    o, lf = fox_sample(z, row0, nb, t, page_table, kc4, vc4, lfc4, j, bf_row,
                                       n_heads=h_d, hd=hd_d,
                                       cols=(2 * d_c // d_d, (2 * d_c + d_d) // d_d, (2 * d_c + 2 * d_d) // d_d, bc),
                                       dest=(mp + ms, row0, mix_b))
                mix_a, mix_b = yc, o
                new['lru'].append(h_last.reshape(nb, d_c))
                new['conv_c'].append(new_buf(cbuf, z, row0, nb, t, 0, d_c))
                new['fox_k'].append(seq_cols(z, row0, nb, t, 2 * d_c + d_d, 2 * d_c + 2 * d_d).reshape(nb, t, h_d, hd_d))
                new['fox_v'].append(seq_cols(z, row0, nb, t, 2 * d_c + 2 * d_d, 2 * d_c + 3 * d_d).reshape(nb, t, h_d, hd_d))
                new['fox_logf'].append(lf[:, :h_d].reshape(nb, t, h_d))
            w_out = odd_w_out

        x = mm_res([mix_a, mix_b], w_out, j, x)

        qx = norm_mm(x, norm_xattn[l], xattn_wq, l, tn=d_x)
        o_x = xattn(qx, 0, bp, tp, mem_k[l], mem_v[l],
                    pl.BlockSpec((n_mem, d_x), lambda b, i: (b, 0)), pl.BlockSpec((n_mem, d_x), lambda b, i: (b, 0)),
                    n_heads=h_x, hd=hd_x, interleaved=False, dest=(mp + ms, 0, None))
        o_x = xattn(qx, mp, bs, ts, mk4, mv4,
                    pl.BlockSpec((1, 1, n_mem * h_x, hd_x), lambda b, i, l=l: (l, b, 0, 0)),
                    pl.BlockSpec((1, 1, n_mem * h_x, hd_x), lambda b, i, l=l: (l, b, 0, 0)),
                    n_heads=h_x, hd=hd_x, interleaved=True, dest=(mp + ms, mp, o_x))
        x = mm_res([o_x], xattn_wo, l, x)

        a = ffn_up(x, norm_ffn2[l], ffn2_wg, ffn2_wu, l)
        x = mm_res([a], ffn2_wd, l, x, tm_target=1056, tn=FFN_DOWN_TN)

    y_p = rms_rows(x, final_norm, 0, mp).reshape(bp, tp, d)
    y_s = rms_rows(x, final_norm, mp, ms).reshape(bs, ts, d)
    mem_k_p = jnp.stack([m.reshape(bp, n_mem, h_x, hd_x) for m in mem_k])
    mem_v_p = jnp.stack([m.reshape(bp, n_mem, h_x, hd_x) for m in mem_v])
    names = ('delta', 'conv_a', 'pool', 'lru', 'conv_c', 'fox_k', 'fox_v', 'fox_logf')
    return ((y_p, y_s) + tuple(jnp.stack(new_p[n]) for n in names) + (mem_k_p, mem_v_p)
            + tuple(jnp.stack(new_s[n]) for n in names))
```

```python
import functools
import math

import jax
import jax.numpy as jnp
from jax import lax
from jax.experimental import pallas as pl
from jax.experimental.pallas import tpu as pltpu

F32 = jnp.float32
BF16 = jnp.bfloat16
EPS = 1e-6
LRU_C = 8.0
POOL_WINDOWS = (2, 4, 8, 16)
LANE = 128
SUBLANE = 8
DELTA_CHUNK = 128
DELTA_HEADS_PER_STEP = 8
INV_BASE = 16
FLASH_ROW_BAND = 128
SAMPLE_PAGE_GROUP = 16
FLASH_CHAINS = 2
NEG = -1e30


def _tile(n, target, align):
    best = None
    for d in range(align, min(n, target) + 1, align):
        if n % d == 0:
            best = d
    return n if best is None else best


def _row_dest(dest, own_rows, block_rows, kern, width, n_prefetch=0):
    if dest is None:
        return own_rows, 0, kern, [], [], {}
    total, row0, into = dest
    assert row0 % block_rows == 0
    if into is None:
        into = jnp.zeros((total, width), BF16)

    def skipping(*refs):
        return kern(*refs[:n_prefetch], *refs[n_prefetch + 1:])

    return (total, row0 // block_rows, skipping, [pl.BlockSpec(memory_space=pl.ANY)], [into],
            dict(input_output_aliases={n_prefetch: 0}))


def _dot(a, b):
    return jnp.dot(a.astype(BF16), b.astype(BF16), preferred_element_type=F32)


def _dot_nt(a, b):
    return lax.dot_general(a.astype(BF16), b.astype(BF16), (((1,), (1,)), ((), ())), preferred_element_type=F32)


def _sigmoid(x):
    return 1.0 / (1.0 + jnp.exp(-x))


def _silu(x):
    return x * _sigmoid(x)


def _softplus(x):
    return jnp.maximum(x, 0.0) + jnp.log1p(jnp.exp(-jnp.abs(x)))


def _expm1(x):
    u = jnp.exp(x)
    degenerate = (u == 1.0) | (u == 0.0)
    safe = jnp.where(degenerate, 2.0, u)
    return jnp.where(u == 1.0, x, jnp.where(u == 0.0, -1.0, (u - 1.0) * x / jnp.log(safe)))


def _gelu_tanh(x):
    return 0.5 * x * (1.0 + jnp.tanh(math.sqrt(2.0 / math.pi) * (x + 0.044715 * (x * x * x))))


def _cumsum_rows(x):
    n = x.shape[0]
    row = lax.broadcasted_iota(jnp.int32, x.shape, 0)
    s = 1
    while s < n:
        x = x + jnp.where(row >= s, pltpu.roll(x, s, 0), 0.0)
        s *= 2
    return x


def _rms_rows(x, g):
    ms = jnp.mean(x * x, axis=-1, keepdims=True)
    return x * lax.rsqrt(ms + EPS) * g


def _norm_mm_kernel(x_ref, g_ref, w_ref, o_ref, hn_ref):
    @pl.when(pl.program_id(1) == 0)
    def _():
        hn_ref[...] = _rms_rows(x_ref[...], g_ref[...]).astype(BF16)

    o_ref[...] = jnp.dot(hn_ref[...], w_ref[0].astype(BF16), preferred_element_type=F32).astype(o_ref.dtype)


def norm_mm(x, g, w, layer=0, *, tm_target=1056, tn=768, out_dtype=F32):
    m, d = x.shape
    w = w.reshape((1,) + w.shape) if w.ndim == 2 else w
    n = w.shape[2]
    tm = _tile(m, tm_target, 16)
    tn = min(tn, n)
    return pl.pallas_call(
        _norm_mm_kernel,
        grid=(m // tm, pl.cdiv(n, tn)),
        in_specs=[pl.BlockSpec((tm, d), lambda i, j: (i, 0)),
                  pl.BlockSpec((1, d), lambda i, j: (0, 0)),
                  pl.BlockSpec((1, d, tn), lambda i, j: (layer, 0, j))],
        out_specs=pl.BlockSpec((tm, tn), lambda i, j: (i, j)),
        out_shape=jax.ShapeDtypeStruct((m, n), out_dtype),
        scratch_shapes=[pltpu.VMEM((tm, d), BF16)],
        name="norm_mm",
    )(x, g.reshape(1, d), w)


def _ffn_up_kernel(x_ref, g_ref, wg_ref, wu_ref, o_ref, hn_ref):
    @pl.when(pl.program_id(1) == 0)
    def _():
        hn_ref[...] = _rms_rows(x_ref[...], g_ref[...]).astype(BF16)

    hn = hn_ref[...]
    gate = jnp.dot(hn, wg_ref[0].astype(BF16), preferred_element_type=F32)
    up = jnp.dot(hn, wu_ref[0].astype(BF16), preferred_element_type=F32)
    o_ref[...] = (0.5 * _silu(gate) * up).astype(o_ref.dtype)


def ffn_up(x, g, wg, wu, layer, *, tm_target=1056, tf=512):
    m, d = x.shape
    f = wg.shape[2]
    tm = _tile(m, tm_target, 16)
    return pl.pallas_call(
        _ffn_up_kernel,
        grid=(m // tm, pl.cdiv(f, tf)),
        in_specs=[pl.BlockSpec((tm, d), lambda i, j: (i, 0)),
                  pl.BlockSpec((1, d), lambda i, j: (0, 0)),
                  pl.BlockSpec((1, d, tf), lambda i, j: (layer, 0, j)),
                  pl.BlockSpec((1, d, tf), lambda i, j: (layer, 0, j))],
        out_specs=pl.BlockSpec((tm, tf), lambda i, j: (i, j)),
        out_shape=jax.ShapeDtypeStruct((m, f), BF16),
        scratch_shapes=[pltpu.VMEM((tm, d), BF16)],
        name="ffn_up",
    )(x, g.reshape(1, d), wg, wu)


def _mm_res_kernel(*refs, n_lhs):
    a_refs = refs[:n_lhs]
    w_refs = refs[n_lhs:2 * n_lhs]
    x_ref, o_ref = refs[2 * n_lhs], refs[2 * n_lhs + 1]
    acc = x_ref[...]
    for a_ref, w_ref in zip(a_refs, w_refs):
        acc = acc + jnp.dot(a_ref[...], w_ref[0].astype(BF16), preferred_element_type=F32)
    o_ref[...] = acc


def mm_res(lhs, w, layer, x, *, tm_target=1056, tn=512):
    m, n = x.shape
    kk = lhs[0].shape[1]
    assert all(a.shape == (m, kk) for a in lhs) and w.shape[1] == kk * len(lhs)
    tm = _tile(m, tm_target, 16)
    tn = min(tn, n)
    nl = len(lhs)
    in_specs = [pl.BlockSpec((tm, kk), lambda i, j: (i, 0)) for _ in lhs]
    in_specs += [pl.BlockSpec((1, kk, tn), lambda i, j, r=r: (layer, r, j)) for r in range(nl)]
    in_specs += [pl.BlockSpec((tm, tn), lambda i, j: (i, j))]
    return pl.pallas_call(
        functools.partial(_mm_res_kernel, n_lhs=nl),
        grid=(m // tm, n // tn),
        in_specs=in_specs,
        out_specs=pl.BlockSpec((tm, tn), lambda i, j: (i, j)),
        out_shape=jax.ShapeDtypeStruct((m, n), F32),
        name="mm_res",
    )(*lhs, *([w] * nl), x)


def _mm_res_wres_kernel(a_ref, w_ref, x_ref, o_ref, wb_ref):
    @pl.when(pl.program_id(1) == 0)
    def _():
        wb_ref[...] = w_ref[0].astype(BF16)

    o_ref[...] = x_ref[...] + jnp.dot(a_ref[...], wb_ref[...], preferred_element_type=F32)


def mm_res_wres(a, w, layer, x, *, tm_target=704, tn=512):
    m, n = x.shape
    kk = a.shape[1]
    assert a.shape == (m, kk) and w.shape[1] == kk and n % tn == 0
    tm = _tile(m, tm_target, 16)
    return pl.pallas_call(
        _mm_res_wres_kernel,
        grid=(n // tn, m // tm),
        in_specs=[pl.BlockSpec((tm, kk), lambda j, i: (i, 0)),
                  pl.BlockSpec((1, kk, tn), lambda j, i: (layer, 0, j)),
                  pl.BlockSpec((tm, tn), lambda j, i: (i, j))],
        out_specs=pl.BlockSpec((tm, tn), lambda j, i: (i, j)),
        out_shape=jax.ShapeDtypeStruct((m, n), F32),
        scratch_shapes=[pltpu.VMEM((kk, tn), BF16)],
        name="mm_res_wres",
    )(a, w, x)


def _rms_kernel(x_ref, g_ref, o_ref):
    o_ref[...] = _rms_rows(x_ref[...], g_ref[...])


def rms_rows(x, g, row0, rows, *, tm_target=512):
    d = x.shape[1]
    tm = _tile(rows, tm_target, 8)
    assert row0 % tm == 0
    off = row0 // tm
    return pl.pallas_call(
        _rms_kernel,
        grid=(rows // tm,),
        in_specs=[pl.BlockSpec((tm, d), lambda i: (off + i, 0)),
                  pl.BlockSpec((1, d), lambda i: (0, 0))],
        out_specs=pl.BlockSpec((tm, d), lambda i: (i, 0)),
        out_shape=jax.ShapeDtypeStruct((rows, d), F32),
        name="final_norm",
    )(x, g.reshape(1, d))


def _conv_rows(xp_ref, x, w, first, halo8, taps):
    t = x.shape[0]

    @pl.when(first)
    def _():
        xp_ref[0:SUBLANE, :] = halo8

    xp_ref[SUBLANE:SUBLANE + t, :] = x
    y = x * w[taps - 1:taps, :]
    for i in range(taps - 1):
        sh = taps - 1 - i
        y = y + xp_ref[SUBLANE - sh:SUBLANE - sh + t, :] * w[i:i + 1, :]
    xp_ref[0:SUBLANE, :] = xp_ref[t:t + SUBLANE, :]
    return y


def _split_bf16(a):
    hi = a.astype(BF16)
    return hi, (a - hi.astype(F32)).astype(BF16)


def _dot3(a, b):
    ah, al = _split_bf16(a)
    bh, bl = _split_bf16(b)
    return (jnp.dot(ah, bh, preferred_element_type=F32)
            + (jnp.dot(ah, bl, preferred_element_type=F32) + jnp.dot(al, bh, preferred_element_type=F32)))


def _dot_mask(a, mask):
    a1 = a.astype(BF16)
    r1 = a - a1.astype(F32)
    a2 = r1.astype(BF16)
    a3 = (r1 - a2.astype(F32)).astype(BF16)
    return (jnp.dot(a1, mask, preferred_element_type=F32)
            + (jnp.dot(a2, mask, preferred_element_type=F32) + jnp.dot(a3, mask, preferred_element_type=F32)))


def _mask_dot(mask, a):
    a1 = a.astype(BF16)
    r1 = a - a1.astype(F32)
    a2 = r1.astype(BF16)
    a3 = (r1 - a2.astype(F32)).astype(BF16)
    return (jnp.dot(mask, a1, preferred_element_type=F32)
            + (jnp.dot(mask, a2, preferred_element_type=F32) + jnp.dot(mask, a3, preferred_element_type=F32)))


def _unit_lower_inverse(mats, top):
    c = mats[0].shape[0]
    ri = lax.broadcasted_iota(jnp.int32, (c, c), 0)
    ci = lax.broadcasted_iota(jnp.int32, (c, c), 1)
    eye = (ri == ci).astype(F32)
    s = min(INV_BASE, top)
    ps = [jnp.where((ri // s) == (ci // s), -a, 0.0) for a in mats]
    ts = [eye + p for p in ps]
    k = 2
    while k < s:
        ps = [_dot(p, p) for p in ps]
        ts = [t + _dot(t, p) for t, p in zip(ts, ps)]
        k *= 2
    while s < top:
        blk = ((ri // (2 * s)) == (ci // (2 * s))) & ((ri // s) != (ci // s))
        tos = [_dot(t, jnp.where(blk, a, 0.0)) for t, a in zip(ts, mats)]
        ts = [t - _dot(to, t) for t, to in zip(ts, tos)]
        s *= 2
    return ts


def _delta_kernel(q_ref, k_ref, v_ref, gt_ref, ba_ref, cq_ref, ck_ref, cv_ref, wq_ref, wk_ref, wv_ref,
                  s0_ref, alog_ref, dt_ref, on_ref, o_ref, sn_ref, xq_ref, xk_ref, xv_ref, s_ref,
                  *, seg, nseg, hb, dk, chunk, taps, n_heads):
    hblk = pl.program_id(1)
    c = pl.program_id(2)
    rows = seg * nseg
    cc = chunk
    wid = hb * dk

    @pl.when(c == 0)
    def _():
        s_ref[...] = s0_ref[...]
        xq_ref[:, 0:SUBLANE, :] = cq_ref[...]
        xk_ref[:, 0:SUBLANE, :] = ck_ref[...]
        xv_ref[:, 0:SUBLANE, :] = cv_ref[...]

    def conv(x_ref, xp_ref, w_ref):
        w = w_ref[...]
        x3 = x_ref[...].reshape(nseg, seg, wid)
        xp_ref[:, SUBLANE:SUBLANE + seg, :] = x3
        y = x3 * w[taps - 1:taps, :]
        for i in range(taps - 1):
            sh = taps - 1 - i
            y = y + xp_ref[:, SUBLANE - sh:SUBLANE - sh + seg, :] * w[i:i + 1, :]
        xp_ref[:, 0:SUBLANE, :] = xp_ref[:, seg:seg + SUBLANE, :]
        return _silu(y).reshape(rows, wid)

    yq_all = conv(q_ref, xq_ref, wq_ref)
    yk_all = conv(k_ref, xk_ref, wk_ref)
    yv_all = conv(v_ref, xv_ref, wv_ref)

    ba = ba_ref[...]
    lane = lax.broadcasted_iota(jnp.int32, ba.shape, 1)
    sig = _sigmoid(ba)
    gfull = -jnp.exp(alog_ref[...]) * _softplus(ba + dt_ref[...])
    gate = gt_ref[...]

    ri = lax.broadcasted_iota(jnp.int32, (cc, cc), 0)
    ci = lax.broadcasted_iota(jnp.int32, (cc, cc), 1)
    same = (ri // seg) == (ci // seg)
    incl = same & (ci <= ri)
    strict = same & (ci < ri)
    seg_last = (ri // seg) * seg + (seg - 1)
    pad = cc - rows
    gpad = jnp.concatenate([gfull, jnp.zeros((pad, LANE), F32)], axis=0) if pad else gfull
    g_run = _mask_dot(incl.astype(BF16), gpad)

    heads = range(hb)

    def prep(hh):
        h = hblk * hb + hh
        sl = slice(hh * dk, (hh + 1) * dk)
        yq, yk, v = yq_all[:, sl], yk_all[:, sl], yv_all[:, sl]
        q = yq * lax.rsqrt(jnp.sum(yq * yq, axis=-1, keepdims=True) + EPS) * (dk ** -0.5)
        k = yk * lax.rsqrt(jnp.sum(yk * yk, axis=-1, keepdims=True) + EPS)
        beta = jnp.sum(jnp.where(lane == h, sig, 0.0), axis=1, keepdims=True)
        if pad:
            zrow = jnp.zeros((pad, dk), F32)
            q = jnp.concatenate([q, zrow], axis=0)
            k = jnp.concatenate([k, zrow], axis=0)
            v = jnp.concatenate([v, zrow], axis=0)
            beta = jnp.concatenate([beta, jnp.zeros((pad, 1), F32)], axis=0)
        lane_c = lax.broadcasted_iota(jnp.int32, (cc, LANE), 1)
        g_i = jnp.sum(jnp.where(lane_c == n_heads + h, g_run, 0.0), axis=1, keepdims=True)
        gcb = jnp.broadcast_to(g_i, (cc, cc))
        grow = jnp.sum(jnp.where(ri == ci, gcb, 0.0), axis=0, keepdims=True)
        dec = jnp.where(incl, jnp.exp(jnp.where(incl, gcb - grow, 0.0)), 0.0)
        g_col = gcb[:, 0:1]
        g_end = jnp.sum(jnp.where(ci == seg_last, grow, 0.0), axis=1, keepdims=True)
        return dict(q=q, k=k, v=v, beta=beta, dec=dec, g_col=g_col, g_end=g_end, gamma=jnp.exp(g_col))

    hd_ = [prep(hh) for hh in heads]
    kks = [_dot_nt(p['k'], p['k']) for p in hd_]
    t_invs = _unit_lower_inverse([jnp.where(strict, p['beta'] * kk * p['dec'], 0.0) for p, kk in zip(hd_, kks)], seg)
    sols = [_dot3(t, jnp.concatenate([(p['beta'] * p['gamma']) * p['k'], p['beta'] * p['v']], axis=1))
            for p, t in zip(hd_, t_invs)]
    attns = [_dot_nt(p['q'], p['k']) * p['dec'] for p in hd_]
    kdts = [(p['k'] * jnp.exp(p['g_end'] - p['g_col'])).T for p in hd_]

    us, oqs = [], []
    for hh in heads:
        w_c, u_c = sols[hh][:, :dk], sols[hh][:, dk:]
        q_dec = hd_[hh]['q'] * hd_[hh]['gamma']
        ul, ol = [], []
        for i in range(nseg):
            r = slice(i * seg, (i + 1) * seg)
            st = s_ref[i, hh]
            ul.append(u_c[r] - _dot(w_c[r], st))
            ol.append(_dot(q_dec[r], st))
        if pad:
            ul.append(jnp.zeros((pad, dk), F32))
            ol.append(jnp.zeros((pad, dk), F32))
        us.append(ul[0] if len(ul) == 1 else jnp.concatenate(ul, axis=0))
        oqs.append(ol[0] if len(ol) == 1 else jnp.concatenate(ol, axis=0))
    os_ = [oq + _dot(attn, u) for oq, attn, u in zip(oqs, attns, us)]
    for hh in heads:
        g_end = hd_[hh]['g_end']
        for i in range(nseg):
            kd_i = kdts[hh] if cc == seg else jnp.where((ci // seg) == i, kdts[hh], 0.0)
            s_ref[i, hh] = jnp.exp(g_end[i * seg:i * seg + 1, :]) * s_ref[i, hh] + _dot(kd_i, us[hh])
    for hh in heads:
        sl = slice(hh * dk, (hh + 1) * dk)
        o = os_[hh][:rows]
        o = o * lax.rsqrt(jnp.mean(o * o, axis=-1, keepdims=True) + EPS) * on_ref[...]
        o_ref[:, sl] = (o * _silu(gate[:, sl])).astype(o_ref.dtype)

    @pl.when(c == pl.num_programs(2) - 1)
    def _():
        sn_ref[...] = s_ref[...]


def delta_heads(z, row0, nb, t, conv8, s0, conv_w, alog_row, dt_row, out_norm, *, n_heads, dk, cols, hb, dest=None):
    cq, ck, cv, cg, cba = cols
    taps = conv_w.shape[0]
    chunk = DELTA_CHUNK
    assert dk == chunk == LANE and t % SUBLANE == 0 and n_heads % hb == 0
    if t >= chunk:
        assert t % chunk == 0
        seg, nseg = chunk, 1
    else:
        assert chunk % t == 0
        seg, nseg = t, _tile(nb, chunk // t, 1)
    rows = seg * nseg
    nc = t // seg
    assert row0 % rows == 0 and all(cb % hb == 0 for cb in (cq, ck, cv, cg))
    r0 = row0 // rows
    wid = hb * dk

    def zspec(cb):
        return pl.BlockSpec((rows, wid), lambda b, h, c: (r0 + b * nc + c, cb // hb + h))

    def cspec(cb):
        return pl.BlockSpec((nseg, SUBLANE, wid), lambda b, h, c: (b, 0, cb // hb + h))

    def wspec(cb):
        return pl.BlockSpec((taps, wid), lambda b, h, c: (0, cb // hb + h))

    row = pl.BlockSpec((1, LANE), lambda b, h, c: (0, 0))
    sspec = pl.BlockSpec((nseg, hb, dk, dk), lambda b, h, c: (b, h, 0, 0))
    kern = functools.partial(_delta_kernel, seg=seg, nseg=nseg, hb=hb, dk=dk, chunk=chunk, taps=taps,
                             n_heads=n_heads)
    total, ob, kern, pre_specs, pre_ops, alias = _row_dest(dest, nb * t, rows, kern, n_heads * dk)
    return pl.pallas_call(
        kern,
        grid=(nb // nseg, n_heads // hb, nc),
        in_specs=pre_specs + [zspec(cq), zspec(ck), zspec(cv), zspec(cg),
                              pl.BlockSpec((rows, LANE), lambda b, h, c: (r0 + b * nc + c, cba)),
                              cspec(cq), cspec(ck), cspec(cv), wspec(cq), wspec(ck), wspec(cv),
                              sspec, row, row, row],
        out_specs=[pl.BlockSpec((rows, wid), lambda b, h, c: (ob + b * nc + c, h)), sspec],
        out_shape=[jax.ShapeDtypeStruct((total, n_heads * dk), BF16),
                   jax.ShapeDtypeStruct((nb, n_heads, dk, dk), F32)],
        scratch_shapes=[pltpu.VMEM((nseg, SUBLANE + seg, wid), F32)] * 3 + [pltpu.VMEM((nseg, hb, dk, dk), F32)],
        name="delta_heads",
        **alias,
    )(*pre_ops, z, z, z, z, z, conv8, conv8, conv8, conv_w, conv_w, conv_w, s0, alog_row, dt_row, out_norm)


def _pool_kernel(u_ref, st_ref, w_ref, sc_ref, o_ref, xp_ref, *, tt, pos0, halo, gw):
    tb = pl.program_id(1)

    @pl.when(tb == 0)
    def _():
        xp_ref[0:halo, :] = st_ref[0]

    u = u_ref[...]
    xp_ref[halo:halo + tt, :] = u
    pos = pos0 + tb * tt + lax.broadcasted_iota(jnp.int32, (tt, 1), 0)
    for gi, wsz in enumerate(POOL_WINDOWS):
        lo, hi = gi * gw, (gi + 1) * gw
        acc = u[:, lo:hi]
        for i in range(1, wsz):
            acc = acc + xp_ref[halo - i:halo - i + tt, lo:hi]
        cnt = jnp.minimum(wsz, pos + 1).astype(F32)
        d = acc * (1.0 / cnt) - u[:, lo:hi]
        y = _dot(d, w_ref[gi]) * sc_ref[:, lo:hi]
        o_ref[:, lo:hi] = y.astype(o_ref.dtype)
    xp_ref[0:halo, :] = xp_ref[tt:tt + halo, :]


def pool_groups(z, row0, nb, t, state16, pool_w, pool_scale, *, col_block, pos0, tt_target=256, dest=None):
    db = pool_scale.shape[-1]
    halo = state16.shape[1]
    tt = _tile(t, tt_target, SUBLANE)
    assert row0 % tt == 0
    nt, r0 = t // tt, row0 // tt
    gw = db // len(POOL_WINDOWS)
    kern = functools.partial(_pool_kernel, tt=tt, pos0=pos0, halo=halo, gw=gw)
    total, ob, kern, pre_specs, pre_ops, alias = _row_dest(dest, nb * t, tt, kern, db)
    return pl.pallas_call(
        kern,
        grid=(nb, nt),
        in_specs=pre_specs + [pl.BlockSpec((tt, db), lambda b, i: (r0 + b * nt + i, col_block)),
                              pl.BlockSpec((1, halo, db), lambda b, i: (b, 0, 0)),
                              pl.BlockSpec(pool_w.shape, lambda b, i: (0, 0, 0)),
                              pl.BlockSpec((1, db), lambda b, i: (0, 0))],
        out_specs=pl.BlockSpec((tt, db), lambda b, i: (ob + b * nt + i, 0)),
        out_shape=jax.ShapeDtypeStruct((total, db), BF16),
        scratch_shapes=[pltpu.VMEM((halo + tt, db), F32)],
        name="pool_groups",
        **alias,
    )(*pre_ops, z, state16, pool_w, pool_scale.reshape(1, db))


def _rglru_kernel(xc_ref, gc_ref, c8_ref, cw_ref, cb_ref, wa_ref, wx_ref, ba_ref, bx_ref, lam_ref, h0_ref,
                  y_ref, hl_ref, xp_ref, h_ref, *, taps, n_blocks, bs):
    tb = pl.program_id(1)
    first = tb == 0

    @pl.when(first)
    def _():
        h_ref[...] = h0_ref[0]

    x = _conv_rows(xp_ref, xc_ref[...], cw_ref[...], first, c8_ref[0], taps) + cb_ref[...]
    tt = x.shape[0]
    rs, is_ = [], []
    for n in range(n_blocks):
        xb = x[:, n * bs:(n + 1) * bs]
        rs.append(_dot(xb, wa_ref[n]))
        is_.append(_dot(xb, wx_ref[n]))
    r = _sigmoid(jnp.concatenate(rs, axis=1) + ba_ref[...])
    i = _sigmoid(jnp.concatenate(is_, axis=1) + bx_ref[...])
    log_a = -LRU_C * r * _softplus(-lam_ref[...])
    a = jnp.exp(log_a)
    b = jnp.sqrt(-_expm1(2.0 * log_a)) * (i * x)
    row = lax.broadcasted_iota(jnp.int32, a.shape, 0) % SUBLANE
    s = 1
    while s < SUBLANE:
        keep = row >= s
        b = jnp.where(keep, a * pltpu.roll(b, s, 0) + b, b)
        a = jnp.where(keep, a * pltpu.roll(a, s, 0), a)
        s *= 2
    carry = h_ref[...]
    groups = []
    for r0 in range(0, tt, SUBLANE):
        hg = b[r0:r0 + SUBLANE] + a[r0:r0 + SUBLANE] * carry
        carry = hg[SUBLANE - 1:SUBLANE, :]
        groups.append(hg)
    hs = groups[0] if len(groups) == 1 else jnp.concatenate(groups, axis=0)
    h_last = carry
    h_ref[...] = h_last
    y_ref[...] = (hs * _gelu_tanh(gc_ref[...])).astype(y_ref.dtype)

    @pl.when(tb == pl.num_programs(1) - 1)
    def _():
        hl_ref[0] = h_last


def rglru_blocks(z, row0, nb, t, conv8, h0, conv_w, conv_b, wa, ba, wx, bx, lam, *, tt_target=256, dest=None):
    dc = lam.shape[-1]
    taps = conv_w.shape[0]
    n_blocks, bs = wa.shape[0], wa.shape[1]
    tt = _tile(t, tt_target, SUBLANE)
    assert row0 % tt == 0
    nt, r0 = t // tt, row0 // tt
    vec = pl.BlockSpec((1, dc), lambda b, i: (0, 0))
    kern = functools.partial(_rglru_kernel, taps=taps, n_blocks=n_blocks, bs=bs)
    total, ob, kern, pre_specs, pre_ops, alias = _row_dest(dest, nb * t, tt, kern, dc)
    return pl.pallas_call(
        kern,
        grid=(nb, nt),
        in_specs=pre_specs + [pl.BlockSpec((tt, dc), lambda b, i: (r0 + b * nt + i, 0)),
                              pl.BlockSpec((tt, dc), lambda b, i: (r0 + b * nt + i, 1)),
                              pl.BlockSpec((1, SUBLANE, dc), lambda b, i: (b, 0, 0)),
                              pl.BlockSpec((taps, dc), lambda b, i: (0, 0)),
                              vec,
                              pl.BlockSpec(wa.shape, lambda b, i: (0, 0, 0)),
                              pl.BlockSpec(wx.shape, lambda b, i: (0, 0, 0)),
                              vec, vec, vec,
                              pl.BlockSpec((1, 1, dc), lambda b, i: (b, 0, 0))],
        out_specs=[pl.BlockSpec((tt, dc), lambda b, i: (ob + b * nt + i, 0)),
                   pl.BlockSpec((1, 1, dc), lambda b, i: (b, 0, 0))],
        out_shape=[jax.ShapeDtypeStruct((total, dc), BF16),
                   jax.ShapeDtypeStruct((nb, 1, dc), F32)],
        scratch_shapes=[pltpu.VMEM((SUBLANE + tt, dc), F32), pltpu.VMEM((1, dc), F32)],
        name="rglru_blocks",
        **alias,
    )(*pre_ops, z, z, conv8, conv_w, conv_b.reshape(1, dc), wa, wx, ba.reshape(1, dc), bx.reshape(1, dc),
      lam.reshape(1, dc), h0.reshape(nb, 1, dc))


def _fox_prep_kernel(f_ref, bf_ref, lf_ref, cum_ref, cumt_ref, *, n_heads):
    lf = -_softplus(-(f_ref[...] + bf_ref[...]))
    lf_ref[...] = lf
    cum = _cumsum_rows(lf)
    cum_ref[...] = cum
    cumt_ref[0] = cum.T[0:n_heads, :]


def fox_prep(z, nb, t, bf_row, *, col_block, n_heads):
    return pl.pallas_call(
        functools.partial(_fox_prep_kernel, n_heads=n_heads),
        grid=(nb,),
        in_specs=[pl.BlockSpec((t, LANE), lambda b: (b, col_block)),
                  pl.BlockSpec((1, LANE), lambda b: (0, 0))],
        out_specs=[pl.BlockSpec((t, LANE), lambda b: (b, 0)),
                   pl.BlockSpec((t, LANE), lambda b: (b, 0)),
                   pl.BlockSpec((1, n_heads, t), lambda b: (b, 0, 0))],
        out_shape=[jax.ShapeDtypeStruct((nb * t, LANE), F32),
                   jax.ShapeDtypeStruct((nb * t, LANE), F32),
                   jax.ShapeDtypeStruct((nb, n_heads, t), F32)],
        name="fox_prep",
    )(z, bf_row)


def _fox_flash_kernel(qi_ref, ki_ref, q_ref, k_ref, v_ref, fq_ref, fk_ref, o_ref, m_ref, l_ref, acc_ref, fqc_ref,
                      *, scale, hb, hd):
    hblk = pl.program_id(1)
    pair = pl.program_id(2)
    qi, ki = qi_ref[pair], ki_ref[pair]
    tq, tk = q_ref.shape[0], k_ref.shape[0]
    log2e = math.log2(math.e)

    @pl.when(ki == 0)
    def _():
        m_ref[...] = jnp.full(m_ref.shape, NEG, F32)
        l_ref[...] = jnp.zeros(l_ref.shape, F32)
        acc_ref[...] = jnp.zeros(acc_ref.shape, F32)
        fq = fq_ref[...] * log2e
        lane = lax.broadcasted_iota(jnp.int32, fq.shape, 1)
        for hh in range(hb):
            col = jnp.sum(jnp.where(lane == hblk * hb + hh, fq, 0.0), axis=1, keepdims=True)
            fqc_ref[hh] = jnp.broadcast_to(col, fqc_ref.shape[1:])

    def attend(diagonal):
        heads = range(hb)
        kb = k_ref[...].astype(BF16)
        vb = v_ref[...].astype(BF16)
        fks = [fk_ref[0, pl.ds(hblk * hb + hh, 1), :] * log2e for hh in heads]
        band = FLASH_ROW_BAND
        chains = [(r0, hh) for r0 in range(0, tq, band) for hh in heads]
        groups = [chains[c0:c0 + FLASH_CHAINS] for c0 in range(0, len(chains), FLASH_CHAINS)]

        def score_tiles(group):
            ts, nks = [], []
            for r0, hh in group:
                sl = slice(hh * hd, (hh + 1) * hd)
                nk = min(tk, -(-(r0 + band) // LANE) * LANE) if diagonal else tk
                qb = q_ref[r0:r0 + band, sl].astype(BF16)
                t2 = _dot_nt(qb, kb[:nk, sl]) * (scale * log2e) - fks[hh][:, :nk]
                if diagonal:
                    ri = lax.broadcasted_iota(jnp.int32, (band, nk), 0) + r0
                    ci = lax.broadcasted_iota(jnp.int32, (band, nk), 1)
                    t2 = jnp.where(ci <= ri, t2, NEG)
                ts.append(t2)
                nks.append(nk)
            return ts, nks

        pending = score_tiles(groups[0])
        for gi, group in enumerate(groups):
            ts, nks = pending
            if gi + 1 < len(groups):
                pending = score_tiles(groups[gi + 1])
            alphas, ps, m_news = [], [], []
            for (r0, hh), t2, nk in zip(group, ts, nks):
                rs = slice(r0, r0 + band)
                fq = fqc_ref[hh, rs, :]
                m_old = m_ref[hh, rs, :]
                m_new = jnp.maximum(m_old, jnp.max(t2, axis=1, keepdims=True) + fq)
                shift = fq - m_new
                alphas.append(jnp.exp2(m_old - m_new))
                ps.append([jnp.exp2(t2[:, j:j + LANE] + shift) for j in range(0, nk, LANE)])
                m_news.append(m_new)
            for (r0, hh), p, alpha, m_new, nk in zip(group, ps, alphas, m_news, nks):
                rs = slice(r0, r0 + band)
                sl = slice(hh * hd, (hh + 1) * hd)
                psum = p[0]
                for pj in p[1:]:
                    psum = psum + pj
                l_ref[hh, rs, :] = alpha * l_ref[hh, rs, :] + jnp.sum(psum, axis=1, keepdims=True)
                pb = jnp.concatenate([pj.astype(BF16) for pj in p], axis=1) if len(p) > 1 else p[0].astype(BF16)
                acc_ref[rs, sl] = alpha * acc_ref[rs, sl] + jnp.dot(pb, vb[:nk, sl], preferred_element_type=F32)
                m_ref[hh, rs, :] = m_new

    @pl.when(ki < qi)
    def _():
        attend(False)

    @pl.when(ki == qi)
    def _():
        attend(True)
        for hh in range(hb):
            sl = slice(hh * hd, (hh + 1) * hd)
            o_ref[:, sl] = (acc_ref[:, sl] / l_ref[hh]).astype(o_ref.dtype)


def fox_prompt(z, cum, cumt, nb, t, *, n_heads, hd, cols, tq_target=2048, hb=2, dest=None):
    cq, ck, cv = cols
    tq = _tile(t, tq_target, LANE)
    nq = t // tq
    assert n_heads % hb == 0 and all(cb % hb == 0 for cb in cols)
    pairs = [(i, j) for i in range(nq) for j in range(i + 1)]
    qi_tab = jnp.asarray([p[0] for p in pairs], jnp.int32)
    ki_tab = jnp.asarray([p[1] for p in pairs], jnp.int32)
    wid = hb * hd
    kern = functools.partial(_fox_flash_kernel, scale=hd ** -0.5, hb=hb, hd=hd)
    total, ob, kern, pre_specs, pre_ops, alias = _row_dest(dest, nb * t, tq, kern, n_heads * hd, n_prefetch=2)

    def kv(cb):
        return pl.BlockSpec((tq, wid), lambda b, h, p, qt, kt: (b * nq + kt[p], cb // hb + h))

    grid_spec = pltpu.PrefetchScalarGridSpec(
        num_scalar_prefetch=2,
        grid=(nb, n_heads // hb, len(pairs)),
        in_specs=pre_specs + [pl.BlockSpec((tq, wid), lambda b, h, p, qt, kt: (b * nq + qt[p], cq // hb + h)),
                              kv(ck), kv(cv),
                              pl.BlockSpec((tq, LANE), lambda b, h, p, qt, kt: (b * nq + qt[p], 0)),
                              pl.BlockSpec((1, n_heads, tq), lambda b, h, p, qt, kt: (b, 0, kt[p]))],
        out_specs=pl.BlockSpec((tq, wid), lambda b, h, p, qt, kt: (ob + b * nq + qt[p], h)),
        scratch_shapes=[pltpu.VMEM((hb, tq, LANE), F32), pltpu.VMEM((hb, tq, LANE), F32), pltpu.VMEM((tq, wid), F32),
                        pltpu.VMEM((hb, tq, LANE), F32)],
    )
    assert hd == LANE
    return pl.pallas_call(
        kern,
        grid_spec=grid_spec,
        out_shape=jax.ShapeDtypeStruct((total, n_heads * hd), BF16),
        name="fox_prompt",
        **alias,
    )(qi_tab, ki_tab, *pre_ops, z, z, z, cum, cumt)


def _fox_sample_kernel(pt_ref, q_ref, kn_ref, vn_ref, f_ref, bf_ref, *rest, n_heads, hd, pages_per_step, scale):
    g_pages = pages_per_step
    k_refs = rest[0:g_pages]
    v_refs = rest[g_pages:2 * g_pages]
    lf_refs = rest[2 * g_pages:3 * g_pages]
    o_ref, lfo_ref, m_ref, l_ref, acc_ref, carry_ref, qall_ref, b0_ref = rest[3 * g_pages:]
    step = pl.program_id(1)
    t = q_ref.shape[0]
    page = lf_refs[0].shape[3]
    wide = page * n_heads
    rows = n_heads * t
    log2e = math.log2(math.e)

    @pl.when(step == 0)
    def _():
        lf = -_softplus(-(f_ref[...] + bf_ref[...]))
        lfo_ref[...] = lf
        fn = _cumsum_rows(lf) * log2e
        carry_ref[...] = jnp.zeros(carry_ref.shape, F32)
        q = q_ref[...].astype(BF16)
        kn = jnp.concatenate([kn_ref[...], jnp.zeros((page - t, n_heads * hd), F32)], axis=0)
        vn = jnp.concatenate([vn_ref[...], jnp.zeros((page - t, n_heads * hd), F32)], axis=0)
        ri = lax.broadcasted_iota(jnp.int32, (t, page), 0)
        ci = lax.broadcasted_iota(jnp.int32, (t, page), 1)
        lane_head = lax.broadcasted_iota(jnp.int32, (t, wide), 1) % n_heads
        for h in range(n_heads):
            rs = slice(h * t, (h + 1) * t)
            sl = slice(h * hd, (h + 1) * hd)
            col = lf[:, h:h + 1] * log2e
            fnrow = jnp.sum(jnp.where(ri <= ci, col, 0.0), axis=0, keepdims=True)
            fnc = fn[:, h:h + 1]
            s2 = _dot_nt(q[:, sl], kn[:, sl]) * (scale * log2e) + fnc - fnrow
            s2 = jnp.where(ci <= ri, s2, NEG)
            m0 = jnp.max(s2, axis=1, keepdims=True)
            p = jnp.exp2(s2 - m0)
            m_ref[rs, :] = jnp.broadcast_to(m0, (t, LANE))
            l_ref[rs, :] = jnp.broadcast_to(jnp.sum(p, axis=1, keepdims=True), (t, LANE))
            acc_ref[rs, :] = _dot(p, vn[:, sl])
            qall_ref[rs, :] = q[:, sl]
            b0_ref[rs, :] = jnp.where(lane_head == h, fnc, NEG)

    lft = jnp.concatenate([lf_refs[g][0, 0] for g in range(g_pages)], axis=0) * log2e
    ri = lax.broadcasted_iota(jnp.int32, (page, wide), 0)
    ci = lax.broadcasted_iota(jnp.int32, (page, wide), 1)
    rr = _dot_mask(lft, (ri > ci // n_heads).astype(BF16))
    tot = jnp.sum(lft, axis=1, keepdims=True)
    run = carry_ref[:, 0:1]
    rc = [None] * g_pages
    for g in reversed(range(g_pages)):
        rc[g] = rr[g * n_heads:(g + 1) * n_heads] + run
        run = run + tot[g * n_heads:(g + 1) * n_heads]
    carry_ref[...] = jnp.broadcast_to(run, carry_ref.shape)

    qall = qall_ref[...]
    tiles = range(0, wide, LANE)
    for g0 in range(0, g_pages, SAMPLE_PAGE_GROUP):
        group = range(g0, min(g0 + SAMPLE_PAGE_GROUP, g_pages))
        scores = []
        for g in group:
            bias = b0_ref[...] + jnp.broadcast_to(rc[g][:, None, :], (n_heads, t, wide)).reshape(rows, wide)
            scores.append(_dot_nt(qall, k_refs[g][0, 0]) * (scale * log2e) + bias)
        smax = scores[0]
        for s2 in scores[1:]:
            smax = jnp.maximum(smax, s2)
        m_old = m_ref[...]
        m_new = jnp.maximum(m_old, jnp.max(smax, axis=1, keepdims=True))
        alpha = jnp.exp2(m_old - m_new)
        psum, pv = None, None
        for g, s2 in zip(group, scores):
            p = [jnp.exp2(s2[:, j:j + LANE] - m_new) for j in tiles]
            for pj in p:
                psum = pj if psum is None else psum + pj
            pb = jnp.concatenate([pj.astype(BF16) for pj in p], axis=1)
            d = jnp.dot(pb, v_refs[g][0, 0].astype(BF16), preferred_element_type=F32)
            pv = d if pv is None else pv + d
        l_ref[...] = alpha * l_ref[...] + jnp.sum(psum, axis=1, keepdims=True)
        acc_ref[...] = alpha * acc_ref[...] + pv
        m_ref[...] = m_new

    @pl.when(step == pl.num_programs(1) - 1)
    def _():
        o = acc_ref[...] / l_ref[...]
        for h in range(n_heads):
            o_ref[:, h * hd:(h + 1) * hd] = o[h * t:(h + 1) * t, :].astype(o_ref.dtype)


def fox_sample(z, row0, nb, t, page_table, kc, vc, lfc, layer, bf_row, *, n_heads, hd, cols, pages_per_step=16, dest=None):
    cq, ck, cv, cf = cols
    n_pages = page_table.shape[1]
    page = lfc.shape[3]
    g_pages = pages_per_step
    while n_pages % g_pages:
        g_pages //= 2
    ns = n_pages // g_pages
    dd = n_heads * hd
    assert row0 % t == 0 and t % SUBLANE == 0
    r0 = row0 // t

    def zspec(cb, w):
        return pl.BlockSpec((t, w), lambda b, s, pt: (r0 + b, cb))

    def pspec(g, r, w):
        return pl.BlockSpec((1, 1, r, w), lambda b, s, pt, g=g: (layer, pt[b, (ns - 1 - s) * g_pages + g], 0, 0))

    kern = functools.partial(_fox_sample_kernel, n_heads=n_heads, hd=hd, pages_per_step=g_pages, scale=hd ** -0.5)
    total, ob, kern, pre_specs, pre_ops, alias = _row_dest(dest, nb * t, t, kern, dd, n_prefetch=1)
    grid_spec = pltpu.PrefetchScalarGridSpec(
        num_scalar_prefetch=1,
        grid=(nb, ns),
        in_specs=pre_specs + [zspec(cq, dd), zspec(ck, dd), zspec(cv, dd), zspec(cf, LANE),
                              pl.BlockSpec((1, LANE), lambda b, s, pt: (0, 0))]
        + [pspec(g, page * n_heads, hd) for g in range(g_pages)] * 2
        + [pspec(g, n_heads, page) for g in range(g_pages)],
        out_specs=[pl.BlockSpec((t, dd), lambda b, s, pt: (ob + b, 0)),
                   pl.BlockSpec((t, LANE), lambda b, s, pt: (b, 0))],
        scratch_shapes=[pltpu.VMEM((n_heads * t, LANE), F32), pltpu.VMEM((n_heads * t, LANE), F32),
                        pltpu.VMEM((n_heads * t, hd), F32), pltpu.VMEM((n_heads, LANE), F32),
                        pltpu.VMEM((n_heads * t, hd), BF16), pltpu.VMEM((n_heads * t, page * n_heads), F32)],
    )
    assert hd == LANE
    return pl.pallas_call(
        kern,
        grid_spec=grid_spec,
        out_shape=[jax.ShapeDtypeStruct((total, dd), BF16), jax.ShapeDtypeStruct((nb * t, LANE), F32)],
        name="fox_sample",
        **alias,
    )(page_table, *pre_ops, z, z, z, z, bf_row, *([kc] * g_pages), *([vc] * g_pages), *([lfc] * g_pages))


def _xattn_kernel(q_ref, k_ref, v_ref, o_ref, *, n_heads, hd, interleaved, scale):
    q = q_ref[...]
    for h in range(n_heads):
        sl = slice(h * hd, (h + 1) * hd)
        if interleaved:
            n_mem = k_ref.shape[2] // n_heads
            kh = k_ref[0, 0, pl.ds(h, n_mem, stride=n_heads), :]
            vh = v_ref[0, 0, pl.ds(h, n_mem, stride=n_heads), :]
        else:
            kh, vh = k_ref[:, sl], v_ref[:, sl]
        s = _dot_nt(q[:, sl], kh) * scale
        p = jnp.exp(s - jnp.max(s, axis=1, keepdims=True))
        o = _dot(p, vh) / jnp.sum(p, axis=1, keepdims=True)
        o_ref[:, sl] = o.astype(o_ref.dtype)


def xattn(qx, row0, nb, t, mk, mv, kspec, vspec, *, n_heads, hd, interleaved, tt_target=512, dest=None):
    dx = n_heads * hd
    tt = _tile(t, tt_target, SUBLANE)
    assert row0 % tt == 0
    nt, r0 = t // tt, row0 // tt
    kern = functools.partial(_xattn_kernel, n_heads=n_heads, hd=hd, interleaved=interleaved, scale=hd ** -0.5)
    total, ob, kern, pre_specs, pre_ops, alias = _row_dest(dest, nb * t, tt, kern, dx)
    return pl.pallas_call(
        kern,
        grid=(nb, nt),
        in_specs=pre_specs + [pl.BlockSpec((tt, dx), lambda b, i: (r0 + b * nt + i, 0)), kspec, vspec],
        out_specs=pl.BlockSpec((tt, dx), lambda b, i: (ob + b * nt + i, 0)),
        out_shape=jax.ShapeDtypeStruct((total, dx), BF16),
        name="xattn",
        **alias,
    )(*pre_ops, qx, mk, mv)


def _pad_cols(w, n):
    return jnp.pad(w, ((0, 0), (0, n - w.shape[1])))


def _halo_rows(buf, rows):
    return jnp.pad(buf, ((0, 0), (rows - buf.shape[1], 0), (0, 0)))


def kernel(x_prompt, x_sample, state_delta, state_conv_a, state_pool, state_lru, state_conv_c, cache_fox_k, cache_fox_v, cache_fox_logf, cache_mem_k, cache_mem_v, page_table, mem_prompt, ffn1_wg, ffn1_wu, ffn1_wd, ffn2_wg, ffn2_wu, ffn2_wd, norm_ffn1, norm_mix, norm_xattn, norm_ffn2, norm_mem, final_norm, xattn_wq, xattn_wk, xattn_wv, xattn_wo, even_w_in, even_w_out, delta_conv_w, delta_a_log, delta_dt_bias, delta_out_norm, pool_w, pool_scale, odd_w_in, odd_w_out, lru_conv_w, lru_conv_b, lru_wa, lru_ba, lru_wx, lru_bx, lru_lambda, fox_bf):
    bp, tp, d = x_prompt.shape
    bs, ts, _ = x_sample.shape
    mp, ms = bp * tp, bs * ts
    depth = ffn1_wg.shape[0]
    h_a, dk_a = state_delta.shape[2], state_delta.shape[3]
    d_a = h_a * dk_a
    d_b = state_pool.shape[-1]
    d_c = state_lru.shape[-1]
    h_d, hd_d = cache_fox_k.shape[3], cache_fox_k.shape[4]
    d_d = h_d * hd_d
    n_mem, h_x, hd_x = cache_mem_k.shape[2], cache_mem_k.shape[3], cache_mem_k.shape[4]
    d_x = h_x * hd_x
    page = cache_fox_k.shape[2]
    pos0_s = page_table.shape[1] * page
    assert d_a == d_b and d_c == d_d and 2 * h_a <= LANE and h_d <= LANE
    tn_z = 768
    nz = -(-(max(4 * d_a + d_b, 2 * d_c + 3 * d_d) + LANE) // tn_z) * tn_z
    bc =(4 * d_a + d_b) // LANE
    assert bc == (2 * d_c + 3 * d_d) // LANE

    x = jnp.concatenate([x_prompt.reshape(mp, d), x_sample.reshape(ms, d)], axis=0)

    mem2 = mem_prompt.reshape(bp * n_mem, d)
    mem_k = [norm_mm(mem2, norm_mem[l], xattn_wk, l, tn=d_x) for l in range(depth)]
    mem_v = [norm_mm(mem2, norm_mem[l], xattn_wv, l, tn=d_x) for l in range(depth)]
    kc4 = cache_fox_k.reshape(cache_fox_k.shape[0], cache_fox_k.shape[1], page * h_d, hd_d)
    vc4 = cache_fox_v.reshape(kc4.shape)
    lfc4 = jnp.swapaxes(cache_fox_logf, 2, 3)
    mk4 = cache_mem_k.reshape(depth, bs, n_mem * h_x, hd_x)
    mv4 = cache_mem_v.reshape(depth, bs, n_mem * h_x, hd_x)

    new_p = {n: [] for n in ('delta', 'conv_a', 'pool', 'lru', 'conv_c', 'fox_k', 'fox_v', 'fox_logf')}
    new_s = {n: [] for n in new_p}

    def seq_cols(zz, row0, nb, t, c0, c1, last=None):
        if last is None or last >= t:
            return lax.slice(zz, (row0, c0), (row0 + nb * t, c1)).reshape(nb, t, c1 - c0)
        if t >= 64 * last:
            return jnp.stack([lax.slice(zz, (row0 + (b + 1) * t - last, c0), (row0 + (b + 1) * t, c1))
                              for b in range(nb)])
        return lax.slice(zz, (row0, c0), (row0 + nb * t, c1)).reshape(nb, t, c1 - c0)[:, t - last:]

    def new_buf(buf, zz, row0, nb, t, c0, c1):
        r = buf.shape[1]
        if t >= r:
            return seq_cols(zz, row0, nb, t, c0, c1, last=r)
        return jnp.concatenate([buf[:, t:], seq_cols(zz, row0, nb, t, c0, c1)], axis=1)

    for l in range(depth):
        j = l // 2
        a = ffn_up(x, norm_ffn1[l], ffn1_wg, ffn1_wu, l)
        x = mm_res_wres(a, ffn1_wd, l, x)

        if l % 2 == 0:
            w = even_w_in[j]
            w_in = _pad_cols(jnp.concatenate([w[:, :4 * d_a], w[:, 4 * d_a + 2 * h_a:], w[:, 4 * d_a:4 * d_a + 2 * h_a]],
                                             axis=1), nz).astype(BF16)
            z = norm_mm(x, norm_mix[l], w_in)
            hb = dk_a // LANE
            alog_row = jnp.zeros((1, LANE), F32).at[0, h_a:2 * h_a].set(delta_a_log[j])
            dt_row = jnp.zeros((1, LANE), F32).at[0, h_a:2 * h_a].set(delta_dt_bias[j])
            groups = ((0, bp, tp, jnp.zeros((bp,) + state_conv_a.shape[2:], F32),
                       jnp.zeros((bp,) + state_delta.shape[2:], F32), jnp.zeros((bp,) + state_pool.shape[2:], F32), 0, new_p),
                      (mp, bs, ts, state_conv_a[j], state_delta[j], state_pool[j], pos0_s, new_s))
            mix_a, mix_b = None, None
            for gi, (row0, nb, t, cbuf, s0, pbuf, pos0, new) in enumerate(groups):
                o, s_new = delta_heads(z, row0, nb, t, _halo_rows(cbuf, SUBLANE), s0, delta_conv_w[j], alog_row, dt_row,
                                       delta_out_norm[j].reshape(1, dk_a), n_heads=h_a, dk=dk_a,
                                       cols=(0, h_a * hb, 2 * h_a * hb, 3 * h_a * hb, bc),
                                       hb=DELTA_HEADS_PER_STEP if t >= DELTA_CHUNK else 1,
                                       dest=(mp + ms, row0, mix_a))
                yb = pool_groups(z, row0, nb, t, _halo_rows(pbuf, 2 * SUBLANE), pool_w[j].astype(BF16), pool_scale[j],
                                 col_block=4 * d_a // d_b, pos0=pos0, dest=(mp + ms, row0, mix_b))
                mix_a, mix_b = o, yb
                new['delta'].append(s_new)
                new['conv_a'].append(new_buf(cbuf, z, row0, nb, t, 0, 3 * d_a))
                new['pool'].append(new_buf(pbuf, z, row0, nb, t, 4 * d_a, 4 * d_a + d_b))
            w_out = even_w_out
        else:
            w_in = _pad_cols(odd_w_in[j], nz).astype(BF16)
            z = norm_mm(x, norm_mix[l], w_in)
            bf_row = jnp.zeros((1, LANE), F32).at[0, :h_d].set(fox_bf[j])
            qb, kb, vb = 2 * d_c // hd_d, (2 * d_c + d_d) // hd_d, (2 * d_c + 2 * d_d) // hd_d
            groups = ((0, bp, tp, jnp.zeros((bp,) + state_conv_c.shape[2:], F32), jnp.zeros((bp, d_c), F32), new_p),
                      (mp, bs, ts, state_conv_c[j], state_lru[j], new_s))
            mix_a, mix_b = None, None
            for gi, (row0, nb, t, cbuf, h0, new) in enumerate(groups):
                yc, h_last = rglru_blocks(z, row0, nb, t, _halo_rows(cbuf, SUBLANE), h0, lru_conv_w[j], lru_conv_b[j],
                                          lru_wa[j].astype(BF16), lru_ba[j], lru_wx[j].astype(BF16), lru_bx[j],
                                          lru_lambda[j], dest=(mp + ms, row0, mix_a))
                if gi == 0:
                    lf, cum, cumt = fox_prep(z, nb, t, bf_row, col_block=bc, n_heads=h_d)
                    o = fox_prompt(z, cum, cumt, nb, t, n_heads=h_d, hd=hd_d, cols=(qb, kb, vb), dest=(mp + ms, row0, mix_b))
                else:
                    o, lf = fox_sample(z, row0, nb, t, page_table, kc4, vc4, lfc4, j, bf_row,
                                       n_heads=h_d, hd=hd_d,
                                       cols=(2 * d_c // d_d, (2 * d_c + d_d) // d_d, (2 * d_c + 2 * d_d) // d_d, bc),
                                       dest=(mp + ms, row0, mix_b))
                mix_a, mix_b = yc, o
                new['lru'].append(h_last.reshape(nb, d_c))
                new['conv_c'].append(new_buf(cbuf, z, row0, nb, t, 0, d_c))
                new['fox_k'].append(seq_cols(z, row0, nb, t, 2 * d_c + d_d, 2 * d_c + 2 * d_d).reshape(nb, t, h_d, hd_d))
                new['fox_v'].append(seq_cols(z, row0, nb, t, 2 * d_c + 2 * d_d, 2 * d_c + 3 * d_d).reshape(nb, t, h_d, hd_d))
                new['fox_logf'].append(lf[:, :h_d].reshape(nb, t, h_d))
            w_out = odd_w_out

        x = mm_res([mix_a, mix_b], w_out, j, x)

        qx = norm_mm(x, norm_xattn[l], xattn_wq, l, tn=d_x)
        o_x = xattn(qx, 0, bp, tp, mem_k[l], mem_v[l],
                    pl.BlockSpec((n_mem, d_x), lambda b, i: (b, 0)), pl.BlockSpec((n_mem, d_x), lambda b, i: (b, 0)),
                    n_heads=h_x, hd=hd_x, interleaved=False, dest=(mp + ms, 0, None))
        o_x = xattn(qx, mp, bs, ts, mk4, mv4,
                    pl.BlockSpec((1, 1, n_mem * h_x, hd_x), lambda b, i, l=l: (l, b, 0, 0)),
                    pl.BlockSpec((1, 1, n_mem * h_x, hd_x), lambda b, i, l=l: (l, b, 0, 0)),
                    n_heads=h_x, hd=hd_x, interleaved=True, dest=(mp + ms, mp, o_x))
        x = mm_res([o_x], xattn_wo, l, x)

        a = ffn_up(x, norm_ffn2[l], ffn2_wg, ffn2_wu, l)
        x = mm_res_wres(a, ffn2_wd, l, x)

    y_p = rms_rows(x, final_norm, 0, mp).reshape(bp, tp, d)
    y_s = rms_rows(x, final_norm, mp, ms).reshape(bs, ts, d)
    mem_k_p = jnp.stack([m.reshape(bp, n_mem, h_x, hd_x) for m in mem_k])
    mem_v_p = jnp.stack([m.reshape(bp, n_mem, h_x, hd_x) for m in mem_v])
    names = ('delta', 'conv_a', 'pool', 'lru', 'conv_c', 'fox_k', 'fox_v', 'fox_logf')
    return ((y_p, y_s) + tuple(jnp.stack(new_p[n]) for n in names) + (mem_k_p, mem_v_p)
            + tuple(jnp.stack(new_s[n]) for n in names))
```

```python
import functools
import math

import jax
import jax.numpy as jnp
from jax import lax
from jax.experimental import pallas as pl
from jax.experimental.pallas import tpu as pltpu

F32 = jnp.float32
BF16 = jnp.bfloat16
EPS = 1e-6
LRU_C = 8.0
POOL_WINDOWS = (2, 4, 8, 16)
LANE = 128
SUBLANE = 8
DELTA_CHUNK = 128
DELTA_HEADS_PER_STEP = 8
INV_BASE = 16
FLASH_ROW_BAND = 128
SAMPLE_PAGE_GROUP = 16
FLASH_CHAINS = 2
NEG = -1e30


def _tile(n, target, align):
    best = None
    for d in range(align, min(n, target) + 1, align):
        if n % d == 0:
            best = d
    return n if best is None else best


def _row_dest(dest, own_rows, block_rows, kern, width, n_prefetch=0):
    if dest is None:
        return own_rows, 0, kern, [], [], {}
    total, row0, into = dest
    assert row0 % block_rows == 0
    if into is None:
        into = jnp.zeros((total, width), BF16)

    def skipping(*refs):
        return kern(*refs[:n_prefetch], *refs[n_prefetch + 1:])

    return (total, row0 // block_rows, skipping, [pl.BlockSpec(memory_space=pl.ANY)], [into],
            dict(input_output_aliases={n_prefetch: 0}))


def _dot(a, b):
    return jnp.dot(a.astype(BF16), b.astype(BF16), preferred_element_type=F32)


def _dot_nt(a, b):
    return lax.dot_general(a.astype(BF16), b.astype(BF16), (((1,), (1,)), ((), ())), preferred_element_type=F32)


def _sigmoid(x):
    return 1.0 / (1.0 + jnp.exp(-x))


def _silu(x):
    return x * _sigmoid(x)


def _softplus(x):
    return jnp.maximum(x, 0.0) + jnp.log1p(jnp.exp(-jnp.abs(x)))


def _expm1(x, u):
    degenerate = (u == 1.0) | (u == 0.0)
    safe = jnp.where(degenerate, 2.0, u)
    return jnp.where(u == 1.0, x, jnp.where(u == 0.0, -1.0, (u - 1.0) * x / jnp.log(safe)))


def _gelu_tanh(x):
    return 0.5 * x * (1.0 + jnp.tanh(math.sqrt(2.0 / math.pi) * (x + 0.044715 * (x * x * x))))


def _cumsum_rows(x):
    n = x.shape[0]
    row = lax.broadcasted_iota(jnp.int32, x.shape, 0)
    s = 1
    while s < n:
        x = x + jnp.where(row >= s, pltpu.roll(x, s, 0), 0.0)
        s *= 2
    return x


def _rms_rows(x, g):
    ms = jnp.mean(x * x, axis=-1, keepdims=True)
    return x * lax.rsqrt(ms + EPS) * g


def _norm_mm_kernel(x_ref, g_ref, w_ref, o_ref, hn_ref):
    @pl.when(pl.program_id(1) == 0)
    def _():
        hn_ref[...] = _rms_rows(x_ref[...], g_ref[...]).astype(BF16)

    o_ref[...] = jnp.dot(hn_ref[...], w_ref[0].astype(BF16), preferred_element_type=F32).astype(o_ref.dtype)


def norm_mm(x, g, w, layer=0, *, tm_target=1056, tn=768, out_dtype=F32):
    m, d = x.shape
    w = w.reshape((1,) + w.shape) if w.ndim == 2 else w
    n = w.shape[2]
    tm = _tile(m, tm_target, 16)
    tn = min(tn, n)
    return pl.pallas_call(
        _norm_mm_kernel,
        grid=(m // tm, pl.cdiv(n, tn)),
        in_specs=[pl.BlockSpec((tm, d), lambda i, j: (i, 0)),
                  pl.BlockSpec((1, d), lambda i, j: (0, 0)),
                  pl.BlockSpec((1, d, tn), lambda i, j: (layer, 0, j))],
        out_specs=pl.BlockSpec((tm, tn), lambda i, j: (i, j)),
        out_shape=jax.ShapeDtypeStruct((m, n), out_dtype),
        scratch_shapes=[pltpu.VMEM((tm, d), BF16)],
        name="norm_mm",
    )(x, g.reshape(1, d), w)


def _ffn_up_kernel(x_ref, g_ref, wg_ref, wu_ref, o_ref, hn_ref):
    @pl.when(pl.program_id(1) == 0)
    def _():
        hn_ref[...] = _rms_rows(x_ref[...], g_ref[...]).astype(BF16)

    hn = hn_ref[...]
    gate = jnp.dot(hn, wg_ref[0].astype(BF16), preferred_element_type=F32)
    up = jnp.dot(hn, wu_ref[0].astype(BF16), preferred_element_type=F32)
    o_ref[...] = (0.5 * _silu(gate) * up).astype(o_ref.dtype)


def ffn_up(x, g, wg, wu, layer, *, tm_target=1056, tf=512):
    m, d = x.shape
    f = wg.shape[2]
    tm = _tile(m, tm_target, 16)
    return pl.pallas_call(
        _ffn_up_kernel,
        grid=(m // tm, pl.cdiv(f, tf)),
        in_specs=[pl.BlockSpec((tm, d), lambda i, j: (i, 0)),
                  pl.BlockSpec((1, d), lambda i, j: (0, 0)),
                  pl.BlockSpec((1, d, tf), lambda i, j: (layer, 0, j)),
                  pl.BlockSpec((1, d, tf), lambda i, j: (layer, 0, j))],
        out_specs=pl.BlockSpec((tm, tf), lambda i, j: (i, j)),
        out_shape=jax.ShapeDtypeStruct((m, f), BF16),
        scratch_shapes=[pltpu.VMEM((tm, d), BF16)],
        name="ffn_up",
    )(x, g.reshape(1, d), wg, wu)


def _mm_res_kernel(*refs, n_lhs):
    a_refs = refs[:n_lhs]
    w_refs = refs[n_lhs:2 * n_lhs]
    x_ref, o_ref = refs[2 * n_lhs], refs[2 * n_lhs + 1]
    acc = x_ref[...]
    for a_ref, w_ref in zip(a_refs, w_refs):
        acc = acc + jnp.dot(a_ref[...], w_ref[0].astype(BF16), preferred_element_type=F32)
    o_ref[...] = acc


def mm_res(lhs, w, layer, x, *, tm_target=1056, tn=512):
    m, n = x.shape
    kk = lhs[0].shape[1]
    assert all(a.shape == (m, kk) for a in lhs) and w.shape[1] == kk * len(lhs)
    tm = _tile(m, tm_target, 16)
    tn = min(tn, n)
    nl = len(lhs)
    in_specs = [pl.BlockSpec((tm, kk), lambda i, j: (i, 0)) for _ in lhs]
    in_specs += [pl.BlockSpec((1, kk, tn), lambda i, j, r=r: (layer, r, j)) for r in range(nl)]
    in_specs += [pl.BlockSpec((tm, tn), lambda i, j: (i, j))]
    return pl.pallas_call(
        functools.partial(_mm_res_kernel, n_lhs=nl),
        grid=(m // tm, n // tn),
        in_specs=in_specs,
        out_specs=pl.BlockSpec((tm, tn), lambda i, j: (i, j)),
        out_shape=jax.ShapeDtypeStruct((m, n), F32),
        name="mm_res",
    )(*lhs, *([w] * nl), x)


def _mm_res_wres_kernel(a_ref, w_ref, x_ref, o_ref, wb_ref):
    @pl.when(pl.program_id(1) == 0)
    def _():
        wb_ref[...] = w_ref[0].astype(BF16)

    o_ref[...] = x_ref[...] + jnp.dot(a_ref[...], wb_ref[...], preferred_element_type=F32)


def mm_res_wres(a, w, layer, x, *, tm_target=704, tn=512):
    m, n = x.shape
    kk = a.shape[1]
    assert a.shape == (m, kk) and w.shape[1] == kk and n % tn == 0
    tm = _tile(m, tm_target, 16)
    return pl.pallas_call(
        _mm_res_wres_kernel,
        grid=(n // tn, m // tm),
        in_specs=[pl.BlockSpec((tm, kk), lambda j, i: (i, 0)),
                  pl.BlockSpec((1, kk, tn), lambda j, i: (layer, 0, j)),
                  pl.BlockSpec((tm, tn), lambda j, i: (i, j))],
        out_specs=pl.BlockSpec((tm, tn), lambda j, i: (i, j)),
        out_shape=jax.ShapeDtypeStruct((m, n), F32),
        scratch_shapes=[pltpu.VMEM((kk, tn), BF16)],
        name="mm_res_wres",
    )(a, w, x)


def _rms_kernel(x_ref, g_ref, o_ref):
    o_ref[...] = _rms_rows(x_ref[...], g_ref[...])


def rms_rows(x, g, row0, rows, *, tm_target=512):
    d = x.shape[1]
    tm = _tile(rows, tm_target, 8)
    assert row0 % tm == 0
    off = row0 // tm
    return pl.pallas_call(
        _rms_kernel,
        grid=(rows // tm,),
        in_specs=[pl.BlockSpec((tm, d), lambda i: (off + i, 0)),
                  pl.BlockSpec((1, d), lambda i: (0, 0))],
        out_specs=pl.BlockSpec((tm, d), lambda i: (i, 0)),
        out_shape=jax.ShapeDtypeStruct((rows, d), F32),
        name="final_norm",
    )(x, g.reshape(1, d))


def _causal_taps(x4, halo, w):
    taps = w.shape[0]
    s_, g_, _, wd = x4.shape
    ext = jnp.concatenate([halo[:, None], x4], axis=1).reshape(s_ * (g_ + 1), SUBLANE, wd)
    row = lax.broadcasted_iota(jnp.int32, (s_ * g_, SUBLANE, wd), 1)
    y = x4.reshape(s_ * g_, SUBLANE, wd) * w[taps - 1:taps, :]
    for i in range(taps - 1):
        sh = taps - 1 - i
        rolled = pltpu.roll(ext, sh, 1).reshape(s_, g_ + 1, SUBLANE, wd)
        cur = rolled[:, 1:].reshape(s_ * g_, SUBLANE, wd)
        prv = rolled[:, :g_].reshape(s_ * g_, SUBLANE, wd)
        y = y + jnp.where(row >= sh, cur, prv) * w[i:i + 1, :]
    return y


def _conv_rows(halo_ref, x, w, first, halo8):
    t, wd = x.shape

    @pl.when(first)
    def _():
        halo_ref[...] = halo8

    y = _causal_taps(x.reshape(1, t // SUBLANE, SUBLANE, wd), halo_ref[...].reshape(1, SUBLANE, wd), w)
    halo_ref[...] = x[t - SUBLANE:, :]
    return y.reshape(t, wd)


def _split_bf16(a):
    hi = a.astype(BF16)
    return hi, (a - hi.astype(F32)).astype(BF16)


def _dot3(a, b):
    ah, al = _split_bf16(a)
    bh, bl = _split_bf16(b)
    return (jnp.dot(ah, bh, preferred_element_type=F32)
            + (jnp.dot(ah, bl, preferred_element_type=F32) + jnp.dot(al, bh, preferred_element_type=F32)))


def _dot_mask(a, mask):
    a1 = a.astype(BF16)
    r1 = a - a1.astype(F32)
    a2 = r1.astype(BF16)
    a3 = (r1 - a2.astype(F32)).astype(BF16)
    return (jnp.dot(a1, mask, preferred_element_type=F32)
            + (jnp.dot(a2, mask, preferred_element_type=F32) + jnp.dot(a3, mask, preferred_element_type=F32)))


def _mask_dot(mask, a):
    a1 = a.astype(BF16)
    r1 = a - a1.astype(F32)
    a2 = r1.astype(BF16)
    a3 = (r1 - a2.astype(F32)).astype(BF16)
    return (jnp.dot(mask, a1, preferred_element_type=F32)
            + (jnp.dot(mask, a2, preferred_element_type=F32) + jnp.dot(mask, a3, preferred_element_type=F32)))


def _unit_lower_inverse(mats, top):
    c = mats[0].shape[0]
    ri = lax.broadcasted_iota(jnp.int32, (c, c), 0)
    ci = lax.broadcasted_iota(jnp.int32, (c, c), 1)
    eye = (ri == ci).astype(F32)
    s = min(INV_BASE, top)
    ps = [jnp.where((ri // s) == (ci // s), -a, 0.0) for a in mats]
    ts = [eye + p for p in ps]
    k = 2
    while k < s:
        ps = [_dot(p, p) for p in ps]
        ts = [t + _dot(t, p) for t, p in zip(ts, ps)]
        k *= 2
    while s < top:
        blk = ((ri // (2 * s)) == (ci // (2 * s))) & ((ri // s) != (ci // s))
        tos = [_dot(t, jnp.where(blk, a, 0.0)) for t, a in zip(ts, mats)]
        ts = [t - _dot(to, t) for t, to in zip(ts, tos)]
        s *= 2
    return ts


def _delta_kernel(q_ref, k_ref, v_ref, gt_ref, ba_ref, cq_ref, ck_ref, cv_ref, wq_ref, wk_ref, wv_ref,
                  s0_ref, alog_ref, dt_ref, on_ref, o_ref, sn_ref, xq_ref, xk_ref, xv_ref, s_ref,
                  *, seg, nseg, hb, dk, chunk, taps, n_heads):
    hblk = pl.program_id(1)
    c = pl.program_id(2)
    rows = seg * nseg
    cc = chunk
    wid = hb * dk

    @pl.when(c == 0)
    def _():
        s_ref[...] = s0_ref[...]
        xq_ref[...] = cq_ref[...]
        xk_ref[...] = ck_ref[...]
        xv_ref[...] = cv_ref[...]

    def conv(x_ref, xp_ref, w_ref):
        x4 = x_ref[...].reshape(nseg, seg // SUBLANE, SUBLANE, wid)
        y = _causal_taps(x4, xp_ref[...], w_ref[...])
        xp_ref[...] = x4[:, seg // SUBLANE - 1]
        return _silu(y).reshape(rows, wid)

    yq_all = conv(q_ref, xq_ref, wq_ref)
    yk_all = conv(k_ref, xk_ref, wk_ref)
    yv_all = conv(v_ref, xv_ref, wv_ref)

    ba = ba_ref[...]
    lane = lax.broadcasted_iota(jnp.int32, ba.shape, 1)
    sig = _sigmoid(ba)
    gfull = -jnp.exp(alog_ref[...]) * _softplus(ba + dt_ref[...])
    gate = gt_ref[...]

    ri = lax.broadcasted_iota(jnp.int32, (cc, cc), 0)
    ci = lax.broadcasted_iota(jnp.int32, (cc, cc), 1)
    same = (ri // seg) == (ci // seg)
    incl = same & (ci <= ri)
    strict = same & (ci < ri)
    seg_last = (ri // seg) * seg + (seg - 1)
    pad = cc - rows
    gpad = jnp.concatenate([gfull, jnp.zeros((pad, LANE), F32)], axis=0) if pad else gfull
    g_run = _mask_dot(incl.astype(BF16), gpad)

    heads = range(hb)

    def prep(hh):
        h = hblk * hb + hh
        sl = slice(hh * dk, (hh + 1) * dk)
        yq, yk, v = yq_all[:, sl], yk_all[:, sl], yv_all[:, sl]
        q = yq * lax.rsqrt(jnp.sum(yq * yq, axis=-1, keepdims=True) + EPS) * (dk ** -0.5)
        k = yk * lax.rsqrt(jnp.sum(yk * yk, axis=-1, keepdims=True) + EPS)
        beta = jnp.sum(jnp.where(lane == h, sig, 0.0), axis=1, keepdims=True)
        if pad:
            zrow = jnp.zeros((pad, dk), F32)
            q = jnp.concatenate([q, zrow], axis=0)
            k = jnp.concatenate([k, zrow], axis=0)
            v = jnp.concatenate([v, zrow], axis=0)
            beta = jnp.concatenate([beta, jnp.zeros((pad, 1), F32)], axis=0)
        lane_c = lax.broadcasted_iota(jnp.int32, (cc, LANE), 1)
        g_i = jnp.sum(jnp.where(lane_c == n_heads + h, g_run, 0.0), axis=1, keepdims=True)
        gcb = jnp.broadcast_to(g_i, (cc, cc))
        grow = jnp.sum(jnp.where(ri == ci, gcb, 0.0), axis=0, keepdims=True)
        dec = jnp.where(incl, jnp.exp(jnp.where(incl, gcb - grow, 0.0)), 0.0)
        g_col = gcb[:, 0:1]
        g_end = jnp.sum(jnp.where(ci == seg_last, grow, 0.0), axis=1, keepdims=True)
        return dict(q=q, k=k, v=v, beta=beta, dec=dec, g_col=g_col, g_end=g_end, gamma=jnp.exp(g_col))

    hd_ = [prep(hh) for hh in heads]
    kks = [_dot_nt(p['k'], p['k']) for p in hd_]
    t_invs = _unit_lower_inverse([jnp.where(strict, p['beta'] * kk * p['dec'], 0.0) for p, kk in zip(hd_, kks)], seg)
    sols = [_dot3(t, jnp.concatenate([(p['beta'] * p['gamma']) * p['k'], p['beta'] * p['v']], axis=1))
            for p, t in zip(hd_, t_invs)]
    attns = [_dot_nt(p['q'], p['k']) * p['dec'] for p in hd_]
    kdts = [(p['k'] * jnp.exp(p['g_end'] - p['g_col'])).T for p in hd_]

    us, oqs = [], []
    for hh in heads:
        w_c, u_c = sols[hh][:, :dk], sols[hh][:, dk:]
        q_dec = hd_[hh]['q'] * hd_[hh]['gamma']
        ul, ol = [], []
        for i in range(nseg):
            r = slice(i * seg, (i + 1) * seg)
            st = s_ref[i, hh]
            ul.append(u_c[r] - _dot(w_c[r], st))
            ol.append(_dot(q_dec[r], st))
        if pad:
            ul.append(jnp.zeros((pad, dk), F32))
            ol.append(jnp.zeros((pad, dk), F32))
        us.append(ul[0] if len(ul) == 1 else jnp.concatenate(ul, axis=0))
        oqs.append(ol[0] if len(ol) == 1 else jnp.concatenate(ol, axis=0))
    os_ = [oq + _dot(attn, u) for oq, attn, u in zip(oqs, attns, us)]
    for hh in heads:
        g_end = hd_[hh]['g_end']
        for i in range(nseg):
            kd_i = kdts[hh] if cc == seg else jnp.where((ci // seg) == i, kdts[hh], 0.0)
            s_ref[i, hh] = jnp.exp(g_end[i * seg:i * seg + 1, :]) * s_ref[i, hh] + _dot(kd_i, us[hh])
    for hh in heads:
        sl = slice(hh * dk, (hh + 1) * dk)
        o = os_[hh][:rows]
        o = o * lax.rsqrt(jnp.mean(o * o, axis=-1, keepdims=True) + EPS) * on_ref[...]
        o_ref[:, sl] = (o * _silu(gate[:, sl])).astype(o_ref.dtype)

    @pl.when(c == pl.num_programs(2) - 1)
    def _():
        sn_ref[...] = s_ref[...]


def delta_heads(z, row0, nb, t, conv8, s0, conv_w, alog_row, dt_row, out_norm, *, n_heads, dk, cols, hb, dest=None):
    cq, ck, cv, cg, cba = cols
    taps = conv_w.shape[0]
    chunk = DELTA_CHUNK
    assert dk == chunk == LANE and t % SUBLANE == 0 and n_heads % hb == 0
    if t >= chunk:
        assert t % chunk == 0
        seg, nseg = chunk, 1
    else:
        assert chunk % t == 0
        seg, nseg = t, _tile(nb, chunk // t, 1)
    rows = seg * nseg
    nc = t // seg
    assert row0 % rows == 0 and all(cb % hb == 0 for cb in (cq, ck, cv, cg))
    r0 = row0 // rows
    wid = hb * dk

    def zspec(cb):
        return pl.BlockSpec((rows, wid), lambda b, h, c: (r0 + b * nc + c, cb // hb + h))

    def cspec(cb):
        return pl.BlockSpec((nseg, SUBLANE, wid), lambda b, h, c: (b, 0, cb // hb + h))

    def wspec(cb):
        return pl.BlockSpec((taps, wid), lambda b, h, c: (0, cb // hb + h))

    row = pl.BlockSpec((1, LANE), lambda b, h, c: (0, 0))
    sspec = pl.BlockSpec((nseg, hb, dk, dk), lambda b, h, c: (b, h, 0, 0))
    kern = functools.partial(_delta_kernel, seg=seg, nseg=nseg, hb=hb, dk=dk, chunk=chunk, taps=taps,
                             n_heads=n_heads)
    total, ob, kern, pre_specs, pre_ops, alias = _row_dest(dest, nb * t, rows, kern, n_heads * dk)
    return pl.pallas_call(
        kern,
        grid=(nb // nseg, n_heads // hb, nc),
        in_specs=pre_specs + [zspec(cq), zspec(ck), zspec(cv), zspec(cg),
                              pl.BlockSpec((rows, LANE), lambda b, h, c: (r0 + b * nc + c, cba)),
                              cspec(cq), cspec(ck), cspec(cv), wspec(cq), wspec(ck), wspec(cv),
                              sspec, row, row, row],
        out_specs=[pl.BlockSpec((rows, wid), lambda b, h, c: (ob + b * nc + c, h)), sspec],
        out_shape=[jax.ShapeDtypeStruct((total, n_heads * dk), BF16),
                   jax.ShapeDtypeStruct((nb, n_heads, dk, dk), F32)],
        scratch_shapes=[pltpu.VMEM((nseg, SUBLANE, wid), F32)] * 3 + [pltpu.VMEM((nseg, hb, dk, dk), F32)],
        name="delta_heads",
        **alias,
    )(*pre_ops, z, z, z, z, z, conv8, conv8, conv8, conv_w, conv_w, conv_w, s0, alog_row, dt_row, out_norm)


def _pool_kernel(u_ref, st_ref, w_ref, sc_ref, o_ref, xp_ref, *, tt, pos0, halo, gw):
    tb = pl.program_id(1)

    @pl.when(tb == 0)
    def _():
        xp_ref[0:halo, :] = st_ref[0]

    u = u_ref[...]
    xp_ref[halo:halo + tt, :] = u
    pos = pos0 + tb * tt + lax.broadcasted_iota(jnp.int32, (tt, 1), 0)
    for gi, wsz in enumerate(POOL_WINDOWS):
        lo, hi = gi * gw, (gi + 1) * gw
        acc = u[:, lo:hi]
        for i in range(1, wsz):
            acc = acc + xp_ref[halo - i:halo - i + tt, lo:hi]
        cnt = jnp.minimum(wsz, pos + 1).astype(F32)
        d = acc * (1.0 / cnt) - u[:, lo:hi]
        y = _dot(d, w_ref[gi]) * sc_ref[:, lo:hi]
        o_ref[:, lo:hi] = y.astype(o_ref.dtype)
    xp_ref[0:halo, :] = xp_ref[tt:tt + halo, :]


def pool_groups(z, row0, nb, t, state16, pool_w, pool_scale, *, col_block, pos0, tt_target=256, dest=None):
    db = pool_scale.shape[-1]
    halo = state16.shape[1]
    tt = _tile(t, tt_target, SUBLANE)
    assert row0 % tt == 0
    nt, r0 = t // tt, row0 // tt
    gw = db // len(POOL_WINDOWS)
    kern = functools.partial(_pool_kernel, tt=tt, pos0=pos0, halo=halo, gw=gw)
    total, ob, kern, pre_specs, pre_ops, alias = _row_dest(dest, nb * t, tt, kern, db)
    return pl.pallas_call(
        kern,
        grid=(nb, nt),
        in_specs=pre_specs + [pl.BlockSpec((tt, db), lambda b, i: (r0 + b * nt + i, col_block)),
                              pl.BlockSpec((1, halo, db), lambda b, i: (b, 0, 0)),
                              pl.BlockSpec(pool_w.shape, lambda b, i: (0, 0, 0)),
                              pl.BlockSpec((1, db), lambda b, i: (0, 0))],
        out_specs=pl.BlockSpec((tt, db), lambda b, i: (ob + b * nt + i, 0)),
        out_shape=jax.ShapeDtypeStruct((total, db), BF16),
        scratch_shapes=[pltpu.VMEM((halo + tt, db), F32)],
        name="pool_groups",
        **alias,
    )(*pre_ops, z, state16, pool_w, pool_scale.reshape(1, db))


def _rglru_kernel(xc_ref, gc_ref, c8_ref, cw_ref, cb_ref, wa_ref, wx_ref, ba_ref, bx_ref, lam_ref, h0_ref,
                  y_ref, hl_ref, xp_ref, h_ref, *, taps, n_blocks, bs):
    tb = pl.program_id(1)
    first = tb == 0

    @pl.when(first)
    def _():
        h_ref[...] = h0_ref[0]

    x = _conv_rows(xp_ref, xc_ref[...], cw_ref[...], first, c8_ref[0]) + cb_ref[...]
    tt = x.shape[0]
    rs, is_ = [], []
    for n in range(n_blocks):
        xb = x[:, n * bs:(n + 1) * bs]
        rs.append(_dot(xb, wa_ref[n]))
        is_.append(_dot(xb, wx_ref[n]))
    r = _sigmoid(jnp.concatenate(rs, axis=1) + ba_ref[...])
    i = _sigmoid(jnp.concatenate(is_, axis=1) + bx_ref[...])
    log_a = -LRU_C * r * _softplus(-lam_ref[...])
    a = jnp.exp(log_a)
    b = jnp.sqrt(-_expm1(2.0 * log_a, a * a)) * (i * x)
    wid = a.shape[1]
    a = a.reshape(tt // SUBLANE, SUBLANE, wid)
    b = b.reshape(tt // SUBLANE, SUBLANE, wid)
    row = lax.broadcasted_iota(jnp.int32, a.shape, 1)
    s = 1
    while s < SUBLANE:
        keep = row >= s
        b = jnp.where(keep, a * pltpu.roll(b, s, 1) + b, b)
        a = jnp.where(keep, a * pltpu.roll(a, s, 1), a)
        s *= 2
    a = a.reshape(tt, wid)
    b = b.reshape(tt, wid)
    carry = h_ref[...]
    groups = []
    for r0 in range(0, tt, SUBLANE):
        hg = b[r0:r0 + SUBLANE] + a[r0:r0 + SUBLANE] * carry
        carry = hg[SUBLANE - 1:SUBLANE, :]
        groups.append(hg)
    hs = groups[0] if len(groups) == 1 else jnp.concatenate(groups, axis=0)
    h_last = carry
    h_ref[...] = h_last
    y_ref[...] = (hs * _gelu_tanh(gc_ref[...])).astype(y_ref.dtype)

    @pl.when(tb == pl.num_programs(1) - 1)
    def _():
        hl_ref[0] = h_last


def rglru_blocks(z, row0, nb, t, conv8, h0, conv_w, conv_b, wa, ba, wx, bx, lam, *, tt_target=256, dest=None):
    dc = lam.shape[-1]
    taps = conv_w.shape[0]
    n_blocks, bs = wa.shape[0], wa.shape[1]
    tt = _tile(t, tt_target, SUBLANE)
    assert row0 % tt == 0
    nt, r0 = t // tt, row0 // tt
    vec = pl.BlockSpec((1, dc), lambda b, i: (0, 0))
    kern = functools.partial(_rglru_kernel, taps=taps, n_blocks=n_blocks, bs=bs)
    total, ob, kern, pre_specs, pre_ops, alias = _row_dest(dest, nb * t, tt, kern, dc)
    return pl.pallas_call(
        kern,
        grid=(nb, nt),
        in_specs=pre_specs + [pl.BlockSpec((tt, dc), lambda b, i: (r0 + b * nt + i, 0)),
                              pl.BlockSpec((tt, dc), lambda b, i: (r0 + b * nt + i, 1)),
                              pl.BlockSpec((1, SUBLANE, dc), lambda b, i: (b, 0, 0)),
                              pl.BlockSpec((taps, dc), lambda b, i: (0, 0)),
                              vec,
                              pl.BlockSpec(wa.shape, lambda b, i: (0, 0, 0)),
                              pl.BlockSpec(wx.shape, lambda b, i: (0, 0, 0)),
                              vec, vec, vec,
                              pl.BlockSpec((1, 1, dc), lambda b, i: (b, 0, 0))],
        out_specs=[pl.BlockSpec((tt, dc), lambda b, i: (ob + b * nt + i, 0)),
                   pl.BlockSpec((1, 1, dc), lambda b, i: (b, 0, 0))],
        out_shape=[jax.ShapeDtypeStruct((total, dc), BF16),
                   jax.ShapeDtypeStruct((nb, 1, dc), F32)],
        scratch_shapes=[pltpu.VMEM((SUBLANE, dc), F32), pltpu.VMEM((1, dc), F32)],
        name="rglru_blocks",
        **alias,
    )(*pre_ops, z, z, conv8, conv_w, conv_b.reshape(1, dc), wa, wx, ba.reshape(1, dc), bx.reshape(1, dc),
      lam.reshape(1, dc), h0.reshape(nb, 1, dc))


def _fox_prep_kernel(f_ref, bf_ref, lf_ref, cum_ref, cumt_ref, *, n_heads):
    lf = -_softplus(-(f_ref[...] + bf_ref[...]))
    lf_ref[...] = lf
    cum = _cumsum_rows(lf)
    cum_ref[...] = cum
    cumt_ref[0] = cum.T[0:n_heads, :]


def fox_prep(z, nb, t, bf_row, *, col_block, n_heads):
    return pl.pallas_call(
        functools.partial(_fox_prep_kernel, n_heads=n_heads),
        grid=(nb,),
        in_specs=[pl.BlockSpec((t, LANE), lambda b: (b, col_block)),
                  pl.BlockSpec((1, LANE), lambda b: (0, 0))],
        out_specs=[pl.BlockSpec((t, LANE), lambda b: (b, 0)),
                   pl.BlockSpec((t, LANE), lambda b: (b, 0)),
                   pl.BlockSpec((1, n_heads, t), lambda b: (b, 0, 0))],
        out_shape=[jax.ShapeDtypeStruct((nb * t, LANE), F32),
                   jax.ShapeDtypeStruct((nb * t, LANE), F32),
                   jax.ShapeDtypeStruct((nb, n_heads, t), F32)],
        name="fox_prep",
    )(z, bf_row)


def _fox_flash_kernel(qi_ref, ki_ref, q_ref, k_ref, v_ref, fq_ref, fk_ref, o_ref, m_ref, l_ref, acc_ref, fqc_ref,
                      *, scale, hb, hd):
    hblk = pl.program_id(1)
    pair = pl.program_id(2)
    qi, ki = qi_ref[pair], ki_ref[pair]
    tq, tk = q_ref.shape[0], k_ref.shape[0]
    log2e = math.log2(math.e)

    @pl.when(ki == 0)
    def _():
        m_ref[...] = jnp.full(m_ref.shape, NEG, F32)
        l_ref[...] = jnp.zeros(l_ref.shape, F32)
        acc_ref[...] = jnp.zeros(acc_ref.shape, F32)
        fq = fq_ref[...] * log2e
        lane = lax.broadcasted_iota(jnp.int32, fq.shape, 1)
        for hh in range(hb):
            col = jnp.sum(jnp.where(lane == hblk * hb + hh, fq, 0.0), axis=1, keepdims=True)
            fqc_ref[hh] = jnp.broadcast_to(col, fqc_ref.shape[1:])

    def attend(diagonal):
        heads = range(hb)
        kb = k_ref[...].astype(BF16)
        vb = v_ref[...].astype(BF16)
        fks = [fk_ref[0, pl.ds(hblk * hb + hh, 1), :] * log2e for hh in heads]
        band = FLASH_ROW_BAND
        chains = [(r0, hh) for r0 in range(0, tq, band) for hh in heads]
        groups = [chains[c0:c0 + FLASH_CHAINS] for c0 in range(0, len(chains), FLASH_CHAINS)]

        def score_tiles(group):
            ts, nks = [], []
            for r0, hh in group:
                sl = slice(hh * hd, (hh + 1) * hd)
                nk = min(tk, -(-(r0 + band) // LANE) * LANE) if diagonal else tk
                qb = q_ref[r0:r0 + band, sl].astype(BF16)
                t2 = _dot_nt(qb, kb[:nk, sl]) * (scale * log2e) - fks[hh][:, :nk]
                if diagonal:
                    ri = lax.broadcasted_iota(jnp.int32, (band, nk), 0) + r0
                    ci = lax.broadcasted_iota(jnp.int32, (band, nk), 1)
                    t2 = jnp.where(ci <= ri, t2, NEG)
                ts.append(t2)
                nks.append(nk)
            return ts, nks

        pending = score_tiles(groups[0])
        for gi, group in enumerate(groups):
            ts, nks = pending
            if gi + 1 < len(groups):
                pending = score_tiles(groups[gi + 1])
            alphas, ps, m_news = [], [], []
            for (r0, hh), t2, nk in zip(group, ts, nks):
                rs = slice(r0, r0 + band)
                fq = fqc_ref[hh, rs, :]
                m_old = m_ref[hh, rs, :]
                m_new = jnp.maximum(m_old, jnp.max(t2, axis=1, keepdims=True) + fq)
                shift = fq - m_new
                alphas.append(jnp.exp2(m_old - m_new))
                ps.append([jnp.exp2(t2[:, j:j + LANE] + shift) for j in range(0, nk, LANE)])
                m_news.append(m_new)
            for (r0, hh), p, alpha, m_new, nk in zip(group, ps, alphas, m_news, nks):
                rs = slice(r0, r0 + band)
                sl = slice(hh * hd, (hh + 1) * hd)
                psum = p[0]
                for pj in p[1:]:
                    psum = psum + pj
                l_ref[hh, rs, :] = alpha * l_ref[hh, rs, :] + jnp.sum(psum, axis=1, keepdims=True)
                pb = jnp.concatenate([pj.astype(BF16) for pj in p], axis=1) if len(p) > 1 else p[0].astype(BF16)
                acc_ref[rs, sl] = alpha * acc_ref[rs, sl] + jnp.dot(pb, vb[:nk, sl], preferred_element_type=F32)
                m_ref[hh, rs, :] = m_new

    @pl.when(ki < qi)
    def _():
        attend(False)

    @pl.when(ki == qi)
    def _():
        attend(True)
        for hh in range(hb):
            sl = slice(hh * hd, (hh + 1) * hd)
            o_ref[:, sl] = (acc_ref[:, sl] / l_ref[hh]).astype(o_ref.dtype)


def fox_prompt(z, cum, cumt, nb, t, *, n_heads, hd, cols, tq_target=2048, hb=2, dest=None):
    cq, ck, cv = cols
    tq = _tile(t, tq_target, LANE)
    nq = t // tq
    assert n_heads % hb == 0 and all(cb % hb == 0 for cb in cols)
    pairs = [(i, j) for i in range(nq) for j in range(i + 1)]
    qi_tab = jnp.asarray([p[0] for p in pairs], jnp.int32)
    ki_tab = jnp.asarray([p[1] for p in pairs], jnp.int32)
    wid = hb * hd
    kern = functools.partial(_fox_flash_kernel, scale=hd ** -0.5, hb=hb, hd=hd)
    total, ob, kern, pre_specs, pre_ops, alias = _row_dest(dest, nb * t, tq, kern, n_heads * hd, n_prefetch=2)

    def kv(cb):
        return pl.BlockSpec((tq, wid), lambda b, h, p, qt, kt: (b * nq + kt[p], cb // hb + h))

    grid_spec = pltpu.PrefetchScalarGridSpec(
        num_scalar_prefetch=2,
        grid=(nb, n_heads // hb, len(pairs)),
        in_specs=pre_specs + [pl.BlockSpec((tq, wid), lambda b, h, p, qt, kt: (b * nq + qt[p], cq // hb + h)),
                              kv(ck), kv(cv),
                              pl.BlockSpec((tq, LANE), lambda b, h, p, qt, kt: (b * nq + qt[p], 0)),
                              pl.BlockSpec((1, n_heads, tq), lambda b, h, p, qt, kt: (b, 0, kt[p]))],
        out_specs=pl.BlockSpec((tq, wid), lambda b, h, p, qt, kt: (ob + b * nq + qt[p], h)),
        scratch_shapes=[pltpu.VMEM((hb, tq, LANE), F32), pltpu.VMEM((hb, tq, LANE), F32), pltpu.VMEM((tq, wid), F32),
                        pltpu.VMEM((hb, tq, LANE), F32)],
    )
    assert hd == LANE
    return pl.pallas_call(
        kern,
        grid_spec=grid_spec,
        out_shape=jax.ShapeDtypeStruct((total, n_heads * hd), BF16),
        name="fox_prompt",
        **alias,
    )(qi_tab, ki_tab, *pre_ops, z, z, z, cum, cumt)


def _fox_sample_kernel(pt_ref, q_ref, kn_ref, vn_ref, f_ref, bf_ref, *rest, n_heads, hd, pages_per_step, scale):
    g_pages = pages_per_step
    k_refs = rest[0:g_pages]
    v_refs = rest[g_pages:2 * g_pages]
    lf_refs = rest[2 * g_pages:3 * g_pages]
    o_ref, lfo_ref, m_ref, l_ref, acc_ref, carry_ref, qall_ref, b0_ref = rest[3 * g_pages:]
    step = pl.program_id(1)
    t = q_ref.shape[0]
    page = lf_refs[0].shape[3]
    wide = page * n_heads
    rows = n_heads * t
    log2e = math.log2(math.e)

    @pl.when(step == 0)
    def _():
        lf = -_softplus(-(f_ref[...] + bf_ref[...]))
        lfo_ref[...] = lf
        fn = _cumsum_rows(lf) * log2e
        carry_ref[...] = jnp.zeros(carry_ref.shape, F32)
        q = q_ref[...].astype(BF16)
        kn = jnp.concatenate([kn_ref[...], jnp.zeros((page - t, n_heads * hd), F32)], axis=0)
        vn = jnp.concatenate([vn_ref[...], jnp.zeros((page - t, n_heads * hd), F32)], axis=0)
        ri = lax.broadcasted_iota(jnp.int32, (t, page), 0)
        ci = lax.broadcasted_iota(jnp.int32, (t, page), 1)
        lane_head = lax.broadcasted_iota(jnp.int32, (t, wide), 1) % n_heads
        for h in range(n_heads):
            rs = slice(h * t, (h + 1) * t)
            sl = slice(h * hd, (h + 1) * hd)
            col = lf[:, h:h + 1] * log2e
            fnrow = jnp.sum(jnp.where(ri <= ci, col, 0.0), axis=0, keepdims=True)
            fnc = fn[:, h:h + 1]
            s2 = _dot_nt(q[:, sl], kn[:, sl]) * (scale * log2e) + fnc - fnrow
            s2 = jnp.where(ci <= ri, s2, NEG)
            m0 = jnp.max(s2, axis=1, keepdims=True)
            p = jnp.exp2(s2 - m0)
            m_ref[rs, :] = jnp.broadcast_to(m0, (t, LANE))
            l_ref[rs, :] = jnp.broadcast_to(jnp.sum(p, axis=1, keepdims=True), (t, LANE))
            acc_ref[rs, :] = _dot(p, vn[:, sl])
            qall_ref[rs, :] = q[:, sl]
            b0_ref[rs, :] = jnp.where(lane_head == h, fnc, NEG)

    lft = jnp.concatenate([lf_refs[g][0, 0] for g in range(g_pages)], axis=0) * log2e
    ri = lax.broadcasted_iota(jnp.int32, (page, wide), 0)
    ci = lax.broadcasted_iota(jnp.int32, (page, wide), 1)
    rr = _dot_mask(lft, (ri > ci // n_heads).astype(BF16))
    tot = jnp.sum(lft, axis=1, keepdims=True)
    run = carry_ref[:, 0:1]
    rc = [None] * g_pages
    for g in reversed(range(g_pages)):
        rc[g] = rr[g * n_heads:(g + 1) * n_heads] + run
        run = run + tot[g * n_heads:(g + 1) * n_heads]
    carry_ref[...] = jnp.broadcast_to(run, carry_ref.shape)

    qall = qall_ref[...]
    tiles = range(0, wide, LANE)
    for g0 in range(0, g_pages, SAMPLE_PAGE_GROUP):
        group = range(g0, min(g0 + SAMPLE_PAGE_GROUP, g_pages))
        scores = []
        for g in group:
            bias = b0_ref[...] + jnp.broadcast_to(rc[g][:, None, :], (n_heads, t, wide)).reshape(rows, wide)
            scores.append(_dot_nt(qall, k_refs[g][0, 0]) * (scale * log2e) + bias)
        smax = scores[0]
        for s2 in scores[1:]:
            smax = jnp.maximum(smax, s2)
        m_old = m_ref[...]
        m_new = jnp.maximum(m_old, jnp.max(smax, axis=1, keepdims=True))
        alpha = jnp.exp2(m_old - m_new)
        psum, pv = None, None
        for g, s2 in zip(group, scores):
            p = [jnp.exp2(s2[:, j:j + LANE] - m_new) for j in tiles]
            for pj in p:
                psum = pj if psum is None else psum + pj
            pb = jnp.concatenate([pj.astype(BF16) for pj in p], axis=1)
            d = jnp.dot(pb, v_refs[g][0, 0].astype(BF16), preferred_element_type=F32)
            pv = d if pv is None else pv + d
        l_ref[...] = alpha * l_ref[...] + jnp.sum(psum, axis=1, keepdims=True)
        acc_ref[...] = alpha * acc_ref[...] + pv
        m_ref[...] = m_new

    @pl.when(step == pl.num_programs(1) - 1)
    def _():
        o = acc_ref[...] / l_ref[...]
        for h in range(n_heads):
            o_ref[:, h * hd:(h + 1) * hd] = o[h * t:(h + 1) * t, :].astype(o_ref.dtype)


def fox_sample(z, row0, nb, t, page_table, kc, vc, lfc, layer, bf_row, *, n_heads, hd, cols, pages_per_step=16, dest=None):
    cq, ck, cv, cf = cols
    n_pages = page_table.shape[1]
    page = lfc.shape[3]
    g_pages = pages_per_step
    while n_pages % g_pages:
        g_pages //= 2
    ns = n_pages // g_pages
    dd = n_heads * hd
    assert row0 % t == 0 and t % SUBLANE == 0
    r0 = row0 // t

    def zspec(cb, w):
        return pl.BlockSpec((t, w), lambda b, s, pt: (r0 + b, cb))

    def pspec(g, r, w):
        return pl.BlockSpec((1, 1, r, w), lambda b, s, pt, g=g: (layer, pt[b, (ns - 1 - s) * g_pages + g], 0, 0))

    kern = functools.partial(_fox_sample_kernel, n_heads=n_heads, hd=hd, pages_per_step=g_pages, scale=hd ** -0.5)
    total, ob, kern, pre_specs, pre_ops, alias = _row_dest(dest, nb * t, t, kern, dd, n_prefetch=1)
    grid_spec = pltpu.PrefetchScalarGridSpec(
        num_scalar_prefetch=1,
        grid=(nb, ns),
        in_specs=pre_specs + [zspec(cq, dd), zspec(ck, dd), zspec(cv, dd), zspec(cf, LANE),
                              pl.BlockSpec((1, LANE), lambda b, s, pt: (0, 0))]
        + [pspec(g, page * n_heads, hd) for g in range(g_pages)] * 2
        + [pspec(g, n_heads, page) for g in range(g_pages)],
        out_specs=[pl.BlockSpec((t, dd), lambda b, s, pt: (ob + b, 0)),
                   pl.BlockSpec((t, LANE), lambda b, s, pt: (b, 0))],
        scratch_shapes=[pltpu.VMEM((n_heads * t, LANE), F32), pltpu.VMEM((n_heads * t, LANE), F32),
                        pltpu.VMEM((n_heads * t, hd), F32), pltpu.VMEM((n_heads, LANE), F32),
                        pltpu.VMEM((n_heads * t, hd), BF16), pltpu.VMEM((n_heads * t, page * n_heads), F32)],
    )
    assert hd == LANE
    return pl.pallas_call(
        kern,
        grid_spec=grid_spec,
        out_shape=[jax.ShapeDtypeStruct((total, dd), BF16), jax.ShapeDtypeStruct((nb * t, LANE), F32)],
        name="fox_sample",
        **alias,
    )(page_table, *pre_ops, z, z, z, z, bf_row, *([kc] * g_pages), *([vc] * g_pages), *([lfc] * g_pages))


def _xattn_kernel(q_ref, k_ref, v_ref, o_ref, *, n_heads, hd, interleaved, scale):
    q = q_ref[...]
    for h in range(n_heads):
        sl = slice(h * hd, (h + 1) * hd)
        if interleaved:
            n_mem = k_ref.shape[2] // n_heads
            kh = k_ref[0, 0, pl.ds(h, n_mem, stride=n_heads), :]
            vh = v_ref[0, 0, pl.ds(h, n_mem, stride=n_heads), :]
        else:
            kh, vh = k_ref[:, sl], v_ref[:, sl]
        s = _dot_nt(q[:, sl], kh) * scale
        p = jnp.exp(s - jnp.max(s, axis=1, keepdims=True))
        o = _dot(p, vh) / jnp.sum(p, axis=1, keepdims=True)
        o_ref[:, sl] = o.astype(o_ref.dtype)


def xattn(qx, row0, nb, t, mk, mv, kspec, vspec, *, n_heads, hd, interleaved, tt_target=1024, dest=None):
    dx = n_heads * hd
    tt = _tile(t, tt_target, SUBLANE)
    assert row0 % tt == 0
    nt, r0 = t // tt, row0 // tt
    kern = functools.partial(_xattn_kernel, n_heads=n_heads, hd=hd, interleaved=interleaved, scale=hd ** -0.5)
    total, ob, kern, pre_specs, pre_ops, alias = _row_dest(dest, nb * t, tt, kern, dx)
    return pl.pallas_call(
        kern,
        grid=(nb, nt),
        in_specs=pre_specs + [pl.BlockSpec((tt, dx), lambda b, i: (r0 + b * nt + i, 0)), kspec, vspec],
        out_specs=pl.BlockSpec((tt, dx), lambda b, i: (ob + b * nt + i, 0)),
        out_shape=jax.ShapeDtypeStruct((total, dx), BF16),
        name="xattn",
        **alias,
    )(*pre_ops, qx, mk, mv)


def _pad_cols(w, n):
    return jnp.pad(w, ((0, 0), (0, n - w.shape[1])))


def _halo_rows(buf, rows):
    return jnp.pad(buf, ((0, 0), (rows - buf.shape[1], 0), (0, 0)))


def kernel(x_prompt, x_sample, state_delta, state_conv_a, state_pool, state_lru, state_conv_c, cache_fox_k, cache_fox_v, cache_fox_logf, cache_mem_k, cache_mem_v, page_table, mem_prompt, ffn1_wg, ffn1_wu, ffn1_wd, ffn2_wg, ffn2_wu, ffn2_wd, norm_ffn1, norm_mix, norm_xattn, norm_ffn2, norm_mem, final_norm, xattn_wq, xattn_wk, xattn_wv, xattn_wo, even_w_in, even_w_out, delta_conv_w, delta_a_log, delta_dt_bias, delta_out_norm, pool_w, pool_scale, odd_w_in, odd_w_out, lru_conv_w, lru_conv_b, lru_wa, lru_ba, lru_wx, lru_bx, lru_lambda, fox_bf):
    bp, tp, d = x_prompt.shape
    bs, ts, _ = x_sample.shape
    mp, ms = bp * tp, bs * ts
    depth = ffn1_wg.shape[0]
    h_a, dk_a = state_delta.shape[2], state_delta.shape[3]
    d_a = h_a * dk_a
    d_b = state_pool.shape[-1]
    d_c = state_lru.shape[-1]
    h_d, hd_d = cache_fox_k.shape[3], cache_fox_k.shape[4]
    d_d = h_d * hd_d
    n_mem, h_x, hd_x = cache_mem_k.shape[2], cache_mem_k.shape[3], cache_mem_k.shape[4]
    d_x = h_x * hd_x
    page = cache_fox_k.shape[2]
    pos0_s = page_table.shape[1] * page
    assert d_a == d_b and d_c == d_d and 2 * h_a <= LANE and h_d <= LANE
    tn_z = 768
    nz = -(-(max(4 * d_a + d_b, 2 * d_c + 3 * d_d) + LANE) // tn_z) * tn_z
    bc =(4 * d_a + d_b) // LANE
    assert bc == (2 * d_c + 3 * d_d) // LANE

    x = jnp.concatenate([x_prompt.reshape(mp, d), x_sample.reshape(ms, d)], axis=0)

    mem2 = mem_prompt.reshape(bp * n_mem, d)
    mem_k = [norm_mm(mem2, norm_mem[l], xattn_wk, l, tn=d_x) for l in range(depth)]
    mem_v = [norm_mm(mem2, norm_mem[l], xattn_wv, l, tn=d_x) for l in range(depth)]
    kc4 = cache_fox_k.reshape(cache_fox_k.shape[0], cache_fox_k.shape[1], page * h_d, hd_d)
    vc4 = cache_fox_v.reshape(kc4.shape)
    lfc4 = jnp.swapaxes(cache_fox_logf, 2, 3)
    mk4 = cache_mem_k.reshape(depth, bs, n_mem * h_x, hd_x)
    mv4 = cache_mem_v.reshape(depth, bs, n_mem * h_x, hd_x)

    new_p = {n: [] for n in ('delta', 'conv_a', 'pool', 'lru', 'conv_c', 'fox_k', 'fox_v', 'fox_logf')}
    new_s = {n: [] for n in new_p}

    def seq_cols(zz, row0, nb, t, c0, c1, last=None):
        if last is None or last >= t:
            return lax.slice(zz, (row0, c0), (row0 + nb * t, c1)).reshape(nb, t, c1 - c0)
        if t >= 64 * last:
            return jnp.stack([lax.slice(zz, (row0 + (b + 1) * t - last, c0), (row0 + (b + 1) * t, c1))
                              for b in range(nb)])
        return lax.slice(zz, (row0, c0), (row0 + nb * t, c1)).reshape(nb, t, c1 - c0)[:, t - last:]

    def new_buf(buf, zz, row0, nb, t, c0, c1):
        r = buf.shape[1]
        if t >= r:
            return seq_cols(zz, row0, nb, t, c0, c1, last=r)
        return jnp.concatenate([buf[:, t:], seq_cols(zz, row0, nb, t, c0, c1)], axis=1)

    for l in range(depth):
        j = l // 2
        a = ffn_up(x, norm_ffn1[l], ffn1_wg, ffn1_wu, l)
        x = mm_res_wres(a, ffn1_wd, l, x)

        if l % 2 == 0:
            w = even_w_in[j]
            w_in = _pad_cols(jnp.concatenate([w[:, :4 * d_a], w[:, 4 * d_a + 2 * h_a:], w[:, 4 * d_a:4 * d_a + 2 * h_a]],
                                             axis=1), nz).astype(BF16)
            z = norm_mm(x, norm_mix[l], w_in)
            hb = dk_a // LANE
            alog_row = jnp.zeros((1, LANE), F32).at[0, h_a:2 * h_a].set(delta_a_log[j])
            dt_row = jnp.zeros((1, LANE), F32).at[0, h_a:2 * h_a].set(delta_dt_bias[j])
            groups = ((0, bp, tp, jnp.zeros((bp,) + state_conv_a.shape[2:], F32),
                       jnp.zeros((bp,) + state_delta.shape[2:], F32), jnp.zeros((bp,) + state_pool.shape[2:], F32), 0, new_p),
                      (mp, bs, ts, state_conv_a[j], state_delta[j], state_pool[j], pos0_s, new_s))
            mix_a, mix_b = None, None
            for gi, (row0, nb, t, cbuf, s0, pbuf, pos0, new) in enumerate(groups):
                o, s_new = delta_heads(z, row0, nb, t, _halo_rows(cbuf, SUBLANE), s0, delta_conv_w[j], alog_row, dt_row,
                                       delta_out_norm[j].reshape(1, dk_a), n_heads=h_a, dk=dk_a,
                                       cols=(0, h_a * hb, 2 * h_a * hb, 3 * h_a * hb, bc),
                                       hb=DELTA_HEADS_PER_STEP if t >= DELTA_CHUNK else 1,
                                       dest=(mp + ms, row0, mix_a))
                yb = pool_groups(z, row0, nb, t, _halo_rows(pbuf, 2 * SUBLANE), pool_w[j].astype(BF16), pool_scale[j],
                                 col_block=4 * d_a // d_b, pos0=pos0, dest=(mp + ms, row0, mix_b))
                mix_a, mix_b = o, yb
                new['delta'].append(s_new)
                new['conv_a'].append(new_buf(cbuf, z, row0, nb, t, 0, 3 * d_a))
                new['pool'].append(new_buf(pbuf, z, row0, nb, t, 4 * d_a, 4 * d_a + d_b))
            w_out = even_w_out
        else:
            w_in = _pad_cols(odd_w_in[j], nz).astype(BF16)
            z = norm_mm(x, norm_mix[l], w_in)
            bf_row = jnp.zeros((1, LANE), F32).at[0, :h_d].set(fox_bf[j])
            qb, kb, vb = 2 * d_c // hd_d, (2 * d_c + d_d) // hd_d, (2 * d_c + 2 * d_d) // hd_d
            groups = ((0, bp, tp, jnp.zeros((bp,) + state_conv_c.shape[2:], F32), jnp.zeros((bp, d_c), F32), new_p),
                      (mp, bs, ts, state_conv_c[j], state_lru[j], new_s))
            mix_a, mix_b = None, None
            for gi, (row0, nb, t, cbuf, h0, new) in enumerate(groups):
                yc, h_last = rglru_blocks(z, row0, nb, t, _halo_rows(cbuf, SUBLANE), h0, lru_conv_w[j], lru_conv_b[j],
                                          lru_wa[j].astype(BF16), lru_ba[j], lru_wx[j].astype(BF16), lru_bx[j],
                                          lru_lambda[j], dest=(mp + ms, row0, mix_a))
                if gi == 0:
                    lf, cum, cumt = fox_prep(z, nb, t, bf_row, col_block=bc, n_heads=h_d)
                    o = fox_prompt(z, cum, cumt, nb, t, n_heads=h_d, hd=hd_d, cols=(qb, kb, vb), dest=(mp + ms, row0, mix_b))
                else:
                    o, lf = fox_sample(z, row0, nb, t, page_table, kc4, vc4, lfc4, j, bf_row,
                                       n_heads=h_d, hd=hd_d,
                                       cols=(2 * d_c // d_d, (2 * d_c + d_d) // d_d, (2 * d_c + 2 * d_d) // d_d, bc),
                                       dest=(mp + ms, row0, mix_b))
                mix_a, mix_b = yc, o
                new['lru'].append(h_last.reshape(nb, d_c))
                new['conv_c'].append(new_buf(cbuf, z, row0, nb, t, 0, d_c))
                new['fox_k'].append(seq_cols(z, row0, nb, t, 2 * d_c + d_d, 2 * d_c + 2 * d_d).reshape(nb, t, h_d, hd_d))
                new['fox_v'].append(seq_cols(z, row0, nb, t, 2 * d_c + 2 * d_d, 2 * d_c + 3 * d_d).reshape(nb, t, h_d, hd_d))
                new['fox_logf'].append(lf[:, :h_d].reshape(nb, t, h_d))
            w_out = odd_w_out

        x = mm_res([mix_a, mix_b], w_out, j, x)

        qx = norm_mm(x, norm_xattn[l], xattn_wq, l, tn=d_x)
        o_x = xattn(qx, 0, bp, tp, mem_k[l], mem_v[l],
                    pl.BlockSpec((n_mem, d_x), lambda b, i: (b, 0)), pl.BlockSpec((n_mem, d_x), lambda b, i: (b, 0)),
                    n_heads=h_x, hd=hd_x, interleaved=False, dest=(mp + ms, 0, None))
        o_x = xattn(qx, mp, bs, ts, mk4, mv4,
                    pl.BlockSpec((1, 1, n_mem * h_x, hd_x), lambda b, i, l=l: (l, b, 0, 0)),
                    pl.BlockSpec((1, 1, n_mem * h_x, hd_x), lambda b, i, l=l: (l, b, 0, 0)),
                    n_heads=h_x, hd=hd_x, interleaved=True, dest=(mp + ms, mp, o_x))
        x = mm_res([o_x], xattn_wo, l, x)

        a = ffn_up(x, norm_ffn2[l], ffn2_wg, ffn2_wu, l)
        x = mm_res_wres(a, ffn2_wd, l, x)

    y_p = rms_rows(x, final_norm, 0, mp).reshape(bp, tp, d)
    y_s = rms_rows(x, final_norm, mp, ms).reshape(bs, ts, d)
    mem_k_p = jnp.stack([m.reshape(bp, n_mem, h_x, hd_x) for m in mem_k])
    mem_v_p = jnp.stack([m.reshape(bp, n_mem, h_x, hd_x) for m in mem_v])
    names = ('delta', 'conv_a', 'pool', 'lru', 'conv_c', 'fox_k', 'fox_v', 'fox_logf')
    return ((y_p, y_s) + tuple(jnp.stack(new_p[n]) for n in names) + (mem_k_p, mem_v_p)
            + tuple(jnp.stack(new_s[n]) for n in names))
```
